```python
import math
import jax, jax.numpy as jnp
from jax import lax
import numpy as np

D_MODEL = 1024
BATCH = 8
SEQ = 4096
DEPTH = 2

CHUNK = 64
N_MIXERS = 2
N_S5 = (DEPTH + 1) // 2
N_GLA = DEPTH // 2
S5_GROUP = 16
S5_GROUPS = D_MODEL // S5_GROUP
S5_STATE = 64
DT_MIN = 1e-3
DT_MAX = 1e-1
GLA_HEADS = 4
GLA_QK = D_MODEL // 2
GLA_DK = GLA_QK // GLA_HEADS
GLA_DV = D_MODEL // GLA_HEADS
GLA_GATE_RANK = 16
GLA_GATE_TAU = 16.0
GLA_IN = 2 * GLA_QK + D_MODEL + GLA_GATE_RANK + D_MODEL
D_FF = 4 * D_MODEL
EPS = 1e-6

kernel_name = "chunk_causal_s5_gla_hybrid"


def rmsnorm(x, g):
    xf = x.astype(jnp.float32)
    y = xf * lax.rsqrt(jnp.mean(xf * xf, axis=-1, keepdims=True) + EPS) * g.astype(jnp.float32)
    return y.astype(x.dtype)


def modulate(h, shift, scale):
    return h * (1.0 + scale[:, None, :]) + shift[:, None, :]


def s5_mixer(u, a_re, a_im, log_dt, b_re, b_im, c_re, c_im, d_skip, w_glu):
    bsz, seq, _ = u.shape
    f32 = jnp.float32
    uf = u.astype(f32).reshape(bsz, seq, S5_GROUPS, S5_GROUP)
    dt = jnp.exp(log_dt.astype(f32))[:, None]
    ar = a_re.astype(f32)
    ai = a_im.astype(f32)
    mag = jnp.exp(ar * dt)
    ph = ai * dt
    lb_re = mag * jnp.cos(ph)
    lb_im = mag * jnp.sin(ph)
    den = ar * ar + ai * ai
    nr = lb_re - 1.0
    ni = lb_im
    f_re = (nr * ar + ni * ai) / den
    f_im = (ni * ar - nr * ai) / den
    br = b_re.astype(f32)
    bi = b_im.astype(f32)
    bb_re = f_re[..., None] * br - f_im[..., None] * bi
    bb_im = f_re[..., None] * bi + f_im[..., None] * br
    bu_re = jnp.einsum('blgh,gph->blgp', uf, bb_re)
    bu_im = jnp.einsum('blgh,gph->blgp', uf, bb_im)
    la_re = jnp.broadcast_to(lb_re, bu_re.shape)
    la_im = jnp.broadcast_to(lb_im, bu_im.shape)

    def combine(left, right):
        a1r, a1i, b1r, b1i = left
        a2r, a2i, b2r, b2i = right
        return (a2r * a1r - a2i * a1i,
                a2r * a1i + a2i * a1r,
                a2r * b1r - a2i * b1i + b2r,
                a2r * b1i + a2i * b1r + b2i)

    _, _, xr, xi = lax.associative_scan(combine, (la_re, la_im, bu_re, bu_im), axis=1)
    y = (jnp.einsum('blgp,ghp->blgh', xr, c_re.astype(f32))
         - jnp.einsum('blgp,ghp->blgh', xi, c_im.astype(f32))
         + d_skip.astype(f32).reshape(S5_GROUPS, S5_GROUP) * uf)
    z = jax.nn.gelu(y.reshape(bsz, seq, D_MODEL)).astype(u.dtype)
    val, gate = jnp.split(z @ w_glu, 2, axis=-1)
    return val * jax.nn.sigmoid(gate)


def gla_mixer(h, w_in, w_gate2, b_gate, g_norm, w_out):
    bsz, seq, _ = h.shape
    n = seq // CHUNK
    f32 = jnp.float32
    proj = h @ w_in
    q, k, v, glr, r = jnp.split(
        proj, [GLA_QK, 2 * GLA_QK, 2 * GLA_QK + D_MODEL, 2 * GLA_QK + D_MODEL + GLA_GATE_RANK], axis=-1)
    log_a = jax.nn.log_sigmoid((glr @ w_gate2 + b_gate).astype(f32)) / GLA_GATE_TAU

    def heads(t, dh):
        return t.reshape(bsz, n, CHUNK, GLA_HEADS, dh).transpose(0, 3, 1, 2, 4).astype(f32)

    q = heads(q, GLA_DK) * (GLA_DK ** -0.5)
    k = heads(k, GLA_DK)
    v = heads(v, GLA_DV)
    gc = jnp.cumsum(heads(log_a, GLA_DK), axis=3)
    g_end = gc[:, :, :, -1:, :]
    k_dec = k * jnp.exp(g_end - gc)
    scores = jnp.einsum('bhncd,bhnsd->bhncs', q, k_dec)
    o_intra = jnp.einsum('bhncs,bhnse->bhnce', scores, v)
    kv = jnp.einsum('bhnsd,bhnse->bhnde', k_dec, v)
    decay = jnp.exp(g_end[:, :, :, 0, :])

    def step(state, inp):
        kv_c, dec_c = inp
        return dec_c[..., None] * state + kv_c, state

    s0 = jnp.zeros((bsz, GLA_HEADS, GLA_DK, GLA_DV), f32)
    _, s_prev = lax.scan(step, s0, (kv.transpose(2, 0, 1, 3, 4), decay.transpose(2, 0, 1, 3)))
    s_prev = s_prev.transpose(1, 2, 0, 3, 4)
    o_inter = jnp.einsum('bhncd,bhnde->bhnce', q * jnp.exp(g_end), s_prev)
    o = o_intra + o_inter
    o = o * lax.rsqrt(jnp.mean(o * o, axis=-1, keepdims=True) + EPS)
    o = o.transpose(0, 2, 3, 1, 4).reshape(bsz, seq, D_MODEL) * g_norm.astype(f32)
    o = o.astype(h.dtype) * jax.nn.silu(r)
    return o @ w_out


def sqrelu_mlp(h, w1, w2):
    a = jax.nn.relu(h @ w1)
    return (a * a) @ w2


def _fwd_setup_inputs(seed: int = 0) -> dict:
    key = jax.random.key(seed)
    ks = jax.random.split(key, 24)
    f32 = jnp.float32
    nrm = lambda k, shape, s: jax.random.normal(k, shape, f32) * s
    x = jax.random.normal(ks[0], (BATCH, SEQ, D_MODEL), f32)
    c = jax.random.normal(ks[1], (BATCH, D_MODEL), f32)
    w_ada = nrm(ks[2], (DEPTH, D_MODEL, 6 * D_MODEL), 0.5 * D_MODEL ** -0.5)
    b_ada = nrm(ks[3], (DEPTH, 6 * D_MODEL), 0.02)
    norm_mix = 1.0 + nrm(ks[4], (DEPTH, D_MODEL), 0.02)
    norm_mlp = 1.0 + nrm(ks[5], (DEPTH, D_MODEL), 0.02)
    n_idx = jnp.arange(S5_STATE, dtype=f32)
    s5_a_re = -0.5 * jnp.exp(nrm(ks[6], (N_S5, S5_GROUPS, S5_STATE), 0.05))
    s5_a_im = math.pi * n_idx + nrm(ks[7], (N_S5, S5_GROUPS, S5_STATE), 0.05)
    s5_log_dt = jax.random.uniform(ks[8], (N_S5, S5_GROUPS), f32, math.log(DT_MIN), math.log(DT_MAX))
    s5_b_re = nrm(ks[9], (N_S5, S5_GROUPS, S5_STATE, S5_GROUP), (2.0 * S5_GROUP) ** -0.5)
    s5_b_im = nrm(ks[10], (N_S5, S5_GROUPS, S5_STATE, S5_GROUP), (2.0 * S5_GROUP) ** -0.5)
    s5_c_re = nrm(ks[11], (N_S5, S5_GROUPS, S5_GROUP, S5_STATE), S5_STATE ** -0.5)
    s5_c_im = nrm(ks[12], (N_S5, S5_GROUPS, S5_GROUP, S5_STATE), S5_STATE ** -0.5)
    s5_d = nrm(ks[13], (N_S5, D_MODEL), 1.0)
    s5_w_glu = nrm(ks[14], (N_S5, D_MODEL, 2 * D_MODEL), D_MODEL ** -0.5)
    gla_w_in = nrm(ks[15], (N_GLA, D_MODEL, GLA_IN), D_MODEL ** -0.5)
    gla_w_gate2 = nrm(ks[16], (N_GLA, GLA_GATE_RANK, GLA_QK), GLA_GATE_RANK ** -0.5)
    gla_b_gate = nrm(ks[17], (N_GLA, GLA_QK), 0.1)
    gla_g_norm = 1.0 + nrm(ks[18], (N_GLA, D_MODEL), 0.02)
    gla_w_out = nrm(ks[19], (N_GLA, D_MODEL, D_MODEL), D_MODEL ** -0.5)
    w_ff1 = nrm(ks[20], (DEPTH, D_MODEL, D_FF), D_MODEL ** -0.5)
    w_ff2 = nrm(ks[21], (DEPTH, D_FF, D_MODEL), D_FF ** -0.5)
    norm_final = 1.0 + nrm(ks[22], (D_MODEL,), 0.02)
    return {"x": x, "c": c, "w_ada": w_ada, "b_ada": b_ada, "norm_mix": norm_mix, "norm_mlp": norm_mlp,
            "s5_a_re": s5_a_re, "s5_a_im": s5_a_im, "s5_log_dt": s5_log_dt, "s5_b_re": s5_b_re,
            "s5_b_im": s5_b_im, "s5_c_re": s5_c_re, "s5_c_im": s5_c_im, "s5_d": s5_d, "s5_w_glu": s5_w_glu,
            "gla_w_in": gla_w_in, "gla_w_gate2": gla_w_gate2, "gla_b_gate": gla_b_gate,
            "gla_g_norm": gla_g_norm, "gla_w_out": gla_w_out, "w_ff1": w_ff1, "w_ff2": w_ff2,
            "norm_final": norm_final}


def _fwd_reference(x, c, w_ada, b_ada, norm_mix, norm_mlp, s5_a_re, s5_a_im, s5_log_dt, s5_b_re, s5_b_im,
              s5_c_re, s5_c_im, s5_d, s5_w_glu, gla_w_in, gla_w_gate2, gla_b_gate, gla_g_norm, gla_w_out,
              w_ff1, w_ff2, norm_final):
    cs = jax.nn.silu(c)
    for i in range(DEPTH):
        mod = cs @ w_ada[i] + b_ada[i]
        sh1, sc1, gt1, sh2, sc2, gt2 = jnp.split(mod, 6, axis=-1)
        h = modulate(rmsnorm(x, norm_mix[i]), sh1, sc1)
        j = i // N_MIXERS
        if i % N_MIXERS == 0:
            y = s5_mixer(h, s5_a_re[j], s5_a_im[j], s5_log_dt[j], s5_b_re[j], s5_b_im[j],
                         s5_c_re[j], s5_c_im[j], s5_d[j], s5_w_glu[j])
        else:
            y = gla_mixer(h, gla_w_in[j], gla_w_gate2[j], gla_b_gate[j], gla_g_norm[j], gla_w_out[j])
        x = x + gt1[:, None, :] * y
        h = modulate(rmsnorm(x, norm_mlp[i]), sh2, sc2)
        x = x + gt2[:, None, :] * sqrelu_mlp(h, w_ff1[i], w_ff2[i])
    return rmsnorm(x, norm_final)


import jax as _jax
import jax.numpy as _jnp

TWIN_FORMAT = 'train_step'
FWD_PARAMS = ['x', 'c', 'w_ada', 'b_ada', 'norm_mix', 'norm_mlp', 's5_a_re', 's5_a_im', 's5_log_dt', 's5_b_re', 's5_b_im', 's5_c_re', 's5_c_im', 's5_d', 's5_w_glu', 'gla_w_in', 'gla_w_gate2', 'gla_b_gate', 'gla_g_norm', 'gla_w_out', 'w_ff1', 'w_ff2', 'norm_final']
TWIN_WEIGHTS = ['w_ada', 'b_ada', 'norm_mix', 'norm_mlp', 's5_a_re', 's5_a_im', 's5_log_dt', 's5_b_re', 's5_b_im', 's5_c_re', 's5_c_im', 's5_d', 's5_w_glu', 'gla_w_in', 'gla_w_gate2', 'gla_b_gate', 'gla_g_norm', 'gla_w_out', 'w_ff1', 'w_ff2', 'norm_final']
TWIN_DIFF_INPUT = 'x'
TWIN_INPUTS = ['x', 'c', 'w_ada', 'b_ada', 'norm_mix', 'norm_mlp', 's5_a_re', 's5_a_im', 's5_log_dt', 's5_b_re', 's5_b_im', 's5_c_re', 's5_c_im', 's5_d', 's5_w_glu', 'gla_w_in', 'gla_w_gate2', 'gla_b_gate', 'gla_g_norm', 'gla_w_out', 'w_ff1', 'w_ff2', 'norm_final', 'loss_target', 'm_w_ada', 'm_b_ada', 'm_norm_mix', 'm_norm_mlp', 'm_s5_a_re', 'm_s5_a_im', 'm_s5_log_dt', 'm_s5_b_re', 'm_s5_b_im', 'm_s5_c_re', 'm_s5_c_im', 'm_s5_d', 'm_s5_w_glu', 'm_gla_w_in', 'm_gla_w_gate2', 'm_gla_b_gate', 'm_gla_g_norm', 'm_gla_w_out', 'm_w_ff1', 'm_w_ff2', 'm_norm_final', 'v_w_ada', 'v_b_ada', 'v_norm_mix', 'v_norm_mlp', 'v_s5_a_re', 'v_s5_a_im', 'v_s5_log_dt', 'v_s5_b_re', 'v_s5_b_im', 'v_s5_c_re', 'v_s5_c_im', 'v_s5_d', 'v_s5_w_glu', 'v_gla_w_in', 'v_gla_w_gate2', 'v_gla_b_gate', 'v_gla_g_norm', 'v_gla_w_out', 'v_w_ff1', 'v_w_ff2', 'v_norm_final']
TWIN_OUTPUTS = ['loss', 'grad_x', 'grad_w_ada', 'grad_b_ada', 'grad_norm_mix', 'grad_norm_mlp', 'grad_s5_a_re', 'grad_s5_a_im', 'grad_s5_log_dt', 'grad_s5_b_re', 'grad_s5_b_im', 'grad_s5_c_re', 'grad_s5_c_im', 'grad_s5_d', 'grad_s5_w_glu', 'grad_gla_w_in', 'grad_gla_w_gate2', 'grad_gla_b_gate', 'grad_gla_g_norm', 'grad_gla_w_out', 'grad_w_ff1', 'grad_w_ff2', 'grad_norm_final', 'delta_w_ada', 'delta_b_ada', 'delta_norm_mix', 'delta_norm_mlp', 'delta_s5_a_re', 'delta_s5_a_im', 'delta_s5_log_dt', 'delta_s5_b_re', 'delta_s5_b_im', 'delta_s5_c_re', 'delta_s5_c_im', 'delta_s5_d', 'delta_s5_w_glu', 'delta_gla_w_in', 'delta_gla_w_gate2', 'delta_gla_b_gate', 'delta_gla_g_norm', 'delta_gla_w_out', 'delta_w_ff1', 'delta_w_ff2', 'delta_norm_final', 'new_m_w_ada', 'new_m_b_ada', 'new_m_norm_mix', 'new_m_norm_mlp', 'new_m_s5_a_re', 'new_m_s5_a_im', 'new_m_s5_log_dt', 'new_m_s5_b_re', 'new_m_s5_b_im', 'new_m_s5_c_re', 'new_m_s5_c_im', 'new_m_s5_d', 'new_m_s5_w_glu', 'new_m_gla_w_in', 'new_m_gla_w_gate2', 'new_m_gla_b_gate', 'new_m_gla_g_norm', 'new_m_gla_w_out', 'new_m_w_ff1', 'new_m_w_ff2', 'new_m_norm_final', 'new_v_w_ada', 'new_v_b_ada', 'new_v_norm_mix', 'new_v_norm_mlp', 'new_v_s5_a_re', 'new_v_s5_a_im', 'new_v_s5_log_dt', 'new_v_s5_b_re', 'new_v_s5_b_im', 'new_v_s5_c_re', 'new_v_s5_c_im', 'new_v_s5_d', 'new_v_s5_w_glu', 'new_v_gla_w_in', 'new_v_gla_w_gate2', 'new_v_gla_b_gate', 'new_v_gla_g_norm', 'new_v_gla_w_out', 'new_v_w_ff1', 'new_v_w_ff2', 'new_v_norm_final']
TWIN_LEAF_KINDS = {'loss': 'loss', 'grad_x': 'grad_x', 'grad_w_ada': 'grad_w', 'grad_b_ada': 'grad_w', 'grad_norm_mix': 'grad_w', 'grad_norm_mlp': 'grad_w', 'grad_s5_a_re': 'grad_w', 'grad_s5_a_im': 'grad_w', 'grad_s5_log_dt': 'grad_w', 'grad_s5_b_re': 'grad_w', 'grad_s5_b_im': 'grad_w', 'grad_s5_c_re': 'grad_w', 'grad_s5_c_im': 'grad_w', 'grad_s5_d': 'grad_w', 'grad_s5_w_glu': 'grad_w', 'grad_gla_w_in': 'grad_w', 'grad_gla_w_gate2': 'grad_w', 'grad_gla_b_gate': 'grad_w', 'grad_gla_g_norm': 'grad_w', 'grad_gla_w_out': 'grad_w', 'grad_w_ff1': 'grad_w', 'grad_w_ff2': 'grad_w', 'grad_norm_final': 'grad_w', 'delta_w_ada': 'delta_w', 'delta_b_ada': 'delta_w', 'delta_norm_mix': 'delta_w', 'delta_norm_mlp': 'delta_w', 'delta_s5_a_re': 'delta_w', 'delta_s5_a_im': 'delta_w', 'delta_s5_log_dt': 'delta_w', 'delta_s5_b_re': 'delta_w', 'delta_s5_b_im': 'delta_w', 'delta_s5_c_re': 'delta_w', 'delta_s5_c_im': 'delta_w', 'delta_s5_d': 'delta_w', 'delta_s5_w_glu': 'delta_w', 'delta_gla_w_in': 'delta_w', 'delta_gla_w_gate2': 'delta_w', 'delta_gla_b_gate': 'delta_w', 'delta_gla_g_norm': 'delta_w', 'delta_gla_w_out': 'delta_w', 'delta_w_ff1': 'delta_w', 'delta_w_ff2': 'delta_w', 'delta_norm_final': 'delta_w', 'new_m_w_ada': 'new_m', 'new_m_b_ada': 'new_m', 'new_m_norm_mix': 'new_m', 'new_m_norm_mlp': 'new_m', 'new_m_s5_a_re': 'new_m', 'new_m_s5_a_im': 'new_m', 'new_m_s5_log_dt': 'new_m', 'new_m_s5_b_re': 'new_m', 'new_m_s5_b_im': 'new_m', 'new_m_s5_c_re': 'new_m', 'new_m_s5_c_im': 'new_m', 'new_m_s5_d': 'new_m', 'new_m_s5_w_glu': 'new_m', 'new_m_gla_w_in': 'new_m', 'new_m_gla_w_gate2': 'new_m', 'new_m_gla_b_gate': 'new_m', 'new_m_gla_g_norm': 'new_m', 'new_m_gla_w_out': 'new_m', 'new_m_w_ff1': 'new_m', 'new_m_w_ff2': 'new_m', 'new_m_norm_final': 'new_m', 'new_v_w_ada': 'new_v', 'new_v_b_ada': 'new_v', 'new_v_norm_mix': 'new_v', 'new_v_norm_mlp': 'new_v', 'new_v_s5_a_re': 'new_v', 'new_v_s5_a_im': 'new_v', 'new_v_s5_log_dt': 'new_v', 'new_v_s5_b_re': 'new_v', 'new_v_s5_b_im': 'new_v', 'new_v_s5_c_re': 'new_v', 'new_v_s5_c_im': 'new_v', 'new_v_s5_d': 'new_v', 'new_v_s5_w_glu': 'new_v', 'new_v_gla_w_in': 'new_v', 'new_v_gla_w_gate2': 'new_v', 'new_v_gla_b_gate': 'new_v', 'new_v_gla_g_norm': 'new_v', 'new_v_gla_w_out': 'new_v', 'new_v_w_ff1': 'new_v', 'new_v_w_ff2': 'new_v', 'new_v_norm_final': 'new_v'}


def _forward(args):
    return _fwd_reference(*[args[k] for k in FWD_PARAMS])


def _output_shape():
    def fwd():
        inp = _fwd_setup_inputs(0)
        return _fwd_reference(*[inp[k] for k in FWD_PARAMS])
    out = _jax.eval_shape(fwd)
    return out.shape, out.dtype

N_MICROBATCH = 1
ADAM_LR = 0.001
ADAM_B1 = 0.9
ADAM_B2 = 0.999
ADAM_EPS = 1e-08
ADAM_WD = 0.01
ADAM_STEP = 10
PER_EXAMPLE_BATCH_AXIS = {'x': 0, 'c': 0, 'loss_target': 0}
SHARED_INPUTS = []
_WEIGHT_DTYPES = {'w_ada': _jnp.float32, 'b_ada': _jnp.float32, 'norm_mix': _jnp.float32, 'norm_mlp': _jnp.float32, 's5_a_re': _jnp.float32, 's5_a_im': _jnp.float32, 's5_log_dt': _jnp.float32, 's5_b_re': _jnp.float32, 's5_b_im': _jnp.float32, 's5_c_re': _jnp.float32, 's5_c_im': _jnp.float32, 's5_d': _jnp.float32, 's5_w_glu': _jnp.float32, 'gla_w_in': _jnp.float32, 'gla_w_gate2': _jnp.float32, 'gla_b_gate': _jnp.float32, 'gla_g_norm': _jnp.float32, 'gla_w_out': _jnp.float32, 'w_ff1': _jnp.float32, 'w_ff2': _jnp.float32, 'norm_final': _jnp.float32}
MOMENT_SCALE = {'w_ada': 7.956642e-02, 'b_ada': 1.415042e-01, 'norm_mix': 5.180594e-02, 'norm_mlp': 7.940977e-02, 's5_a_re': 2.135361e-03, 's5_a_im': 1.846806e-03, 's5_log_dt': 6.405436e-01, 's5_b_re': 1.248691e-03, 's5_b_im': 1.326151e-03, 's5_c_re': 1.934156e-03, 's5_c_im': 1.815320e-03, 's5_d': 2.523195e-02, 's5_w_glu': 1.763186e-02, 'gla_w_in': 3.951627e-02, 'gla_w_gate2': 9.854160e-03, 'gla_b_gate': 2.509806e-02, 'gla_g_norm': 3.318015e-02, 'gla_w_out': 3.255066e-02, 'w_ff1': 4.075456e-02, 'w_ff2': 7.428793e-02, 'norm_final': 3.220999e+01}


def _to_microbatches(a, axis):
    t = _jnp.moveaxis(a, axis, 0)
    t = t.reshape((N_MICROBATCH, t.shape[0] // N_MICROBATCH) + t.shape[1:])
    return _jnp.moveaxis(t, 1, axis + 1)


def setup_inputs(seed: int = 0) -> dict:
    inp = _fwd_setup_inputs(seed)
    key = _jax.random.fold_in(_jax.random.key(seed), 7919)
    shape, _ = _output_shape()
    out = dict(inp)
    out["loss_target"] = _jax.random.normal(_jax.random.fold_in(key, 0), shape, _jnp.float32)
    for i, name in enumerate(TWIN_WEIGHTS):
        w = inp[name].astype(_jnp.float32)
        if MOMENT_SCALE is None:
            s = _jnp.sqrt(_jnp.mean(_jnp.square(w)) + 1e-30)
        else:
            s = MOMENT_SCALE[name]
        km, kv = _jax.random.split(_jax.random.fold_in(key, i + 1))
        out[name] = w
        out["m_" + name] = s * _jax.random.normal(km, w.shape, _jnp.float32)
        out["v_" + name] = (s * s) * _jax.random.uniform(kv, w.shape, _jnp.float32, 0.5, 1.5)
    if N_MICROBATCH > 1:
        for name, axis in PER_EXAMPLE_BATCH_AXIS.items():
            out[name] = _to_microbatches(out[name], axis)
    return {'x': out['x'], 'c': out['c'], 'w_ada': out['w_ada'], 'b_ada': out['b_ada'], 'norm_mix': out['norm_mix'], 'norm_mlp': out['norm_mlp'], 's5_a_re': out['s5_a_re'], 's5_a_im': out['s5_a_im'], 's5_log_dt': out['s5_log_dt'], 's5_b_re': out['s5_b_re'], 's5_b_im': out['s5_b_im'], 's5_c_re': out['s5_c_re'], 's5_c_im': out['s5_c_im'], 's5_d': out['s5_d'], 's5_w_glu': out['s5_w_glu'], 'gla_w_in': out['gla_w_in'], 'gla_w_gate2': out['gla_w_gate2'], 'gla_b_gate': out['gla_b_gate'], 'gla_g_norm': out['gla_g_norm'], 'gla_w_out': out['gla_w_out'], 'w_ff1': out['w_ff1'], 'w_ff2': out['w_ff2'], 'norm_final': out['norm_final'], 'loss_target': out['loss_target'], 'm_w_ada': out['m_w_ada'], 'm_b_ada': out['m_b_ada'], 'm_norm_mix': out['m_norm_mix'], 'm_norm_mlp': out['m_norm_mlp'], 'm_s5_a_re': out['m_s5_a_re'], 'm_s5_a_im': out['m_s5_a_im'], 'm_s5_log_dt': out['m_s5_log_dt'], 'm_s5_b_re': out['m_s5_b_re'], 'm_s5_b_im': out['m_s5_b_im'], 'm_s5_c_re': out['m_s5_c_re'], 'm_s5_c_im': out['m_s5_c_im'], 'm_s5_d': out['m_s5_d'], 'm_s5_w_glu': out['m_s5_w_glu'], 'm_gla_w_in': out['m_gla_w_in'], 'm_gla_w_gate2': out['m_gla_w_gate2'], 'm_gla_b_gate': out['m_gla_b_gate'], 'm_gla_g_norm': out['m_gla_g_norm'], 'm_gla_w_out': out['m_gla_w_out'], 'm_w_ff1': out['m_w_ff1'], 'm_w_ff2': out['m_w_ff2'], 'm_norm_final': out['m_norm_final'], 'v_w_ada': out['v_w_ada'], 'v_b_ada': out['v_b_ada'], 'v_norm_mix': out['v_norm_mix'], 'v_norm_mlp': out['v_norm_mlp'], 'v_s5_a_re': out['v_s5_a_re'], 'v_s5_a_im': out['v_s5_a_im'], 'v_s5_log_dt': out['v_s5_log_dt'], 'v_s5_b_re': out['v_s5_b_re'], 'v_s5_b_im': out['v_s5_b_im'], 'v_s5_c_re': out['v_s5_c_re'], 'v_s5_c_im': out['v_s5_c_im'], 'v_s5_d': out['v_s5_d'], 'v_s5_w_glu': out['v_s5_w_glu'], 'v_gla_w_in': out['v_gla_w_in'], 'v_gla_w_gate2': out['v_gla_w_gate2'], 'v_gla_b_gate': out['v_gla_b_gate'], 'v_gla_g_norm': out['v_gla_g_norm'], 'v_gla_w_out': out['v_gla_w_out'], 'v_w_ff1': out['v_w_ff1'], 'v_w_ff2': out['v_w_ff2'], 'v_norm_final': out['v_norm_final']}


def _loss(weights, diff, rest, loss_target):
    with _jax.named_scope("forward"):
        args = {**rest, TWIN_DIFF_INPUT: diff, **{k: w.astype(_WEIGHT_DTYPES[k]) for k, w in weights.items()}}
        y = _forward(args)
    with _jax.named_scope("loss_head"):
        err = _jnp.square(y.astype(_jnp.float32) - loss_target)
        return 0.5 * _jnp.sum(_jnp.mean(err, axis=-1)) if err.ndim else 0.5 * err


def _adamw(w, g, m, v):
    m = ADAM_B1 * m + (1.0 - ADAM_B1) * g
    v = ADAM_B2 * v + (1.0 - ADAM_B2) * _jnp.square(g)
    m_hat = m / (1.0 - ADAM_B1 ** ADAM_STEP)
    v_hat = v / (1.0 - ADAM_B2 ** ADAM_STEP)
    delta = -ADAM_LR * (m_hat / (_jnp.sqrt(v_hat) + ADAM_EPS) + ADAM_WD * w)
    return delta, m, v


def reference(x, c, w_ada, b_ada, norm_mix, norm_mlp, s5_a_re, s5_a_im, s5_log_dt, s5_b_re, s5_b_im, s5_c_re, s5_c_im, s5_d, s5_w_glu, gla_w_in, gla_w_gate2, gla_b_gate, gla_g_norm, gla_w_out, w_ff1, w_ff2, norm_final, loss_target, m_w_ada, m_b_ada, m_norm_mix, m_norm_mlp, m_s5_a_re, m_s5_a_im, m_s5_log_dt, m_s5_b_re, m_s5_b_im, m_s5_c_re, m_s5_c_im, m_s5_d, m_s5_w_glu, m_gla_w_in, m_gla_w_gate2, m_gla_b_gate, m_gla_g_norm, m_gla_w_out, m_w_ff1, m_w_ff2, m_norm_final, v_w_ada, v_b_ada, v_norm_mix, v_norm_mlp, v_s5_a_re, v_s5_a_im, v_s5_log_dt, v_s5_b_re, v_s5_b_im, v_s5_c_re, v_s5_c_im, v_s5_d, v_s5_w_glu, v_gla_w_in, v_gla_w_gate2, v_gla_b_gate, v_gla_g_norm, v_gla_w_out, v_w_ff1, v_w_ff2, v_norm_final):
    given = dict(x=x, c=c, w_ada=w_ada, b_ada=b_ada, norm_mix=norm_mix, norm_mlp=norm_mlp, s5_a_re=s5_a_re, s5_a_im=s5_a_im, s5_log_dt=s5_log_dt, s5_b_re=s5_b_re, s5_b_im=s5_b_im, s5_c_re=s5_c_re, s5_c_im=s5_c_im, s5_d=s5_d, s5_w_glu=s5_w_glu, gla_w_in=gla_w_in, gla_w_gate2=gla_w_gate2, gla_b_gate=gla_b_gate, gla_g_norm=gla_g_norm, gla_w_out=gla_w_out, w_ff1=w_ff1, w_ff2=w_ff2, norm_final=norm_final, loss_target=loss_target, m_w_ada=m_w_ada, m_b_ada=m_b_ada, m_norm_mix=m_norm_mix, m_norm_mlp=m_norm_mlp, m_s5_a_re=m_s5_a_re, m_s5_a_im=m_s5_a_im, m_s5_log_dt=m_s5_log_dt, m_s5_b_re=m_s5_b_re, m_s5_b_im=m_s5_b_im, m_s5_c_re=m_s5_c_re, m_s5_c_im=m_s5_c_im, m_s5_d=m_s5_d, m_s5_w_glu=m_s5_w_glu, m_gla_w_in=m_gla_w_in, m_gla_w_gate2=m_gla_w_gate2, m_gla_b_gate=m_gla_b_gate, m_gla_g_norm=m_gla_g_norm, m_gla_w_out=m_gla_w_out, m_w_ff1=m_w_ff1, m_w_ff2=m_w_ff2, m_norm_final=m_norm_final, v_w_ada=v_w_ada, v_b_ada=v_b_ada, v_norm_mix=v_norm_mix, v_norm_mlp=v_norm_mlp, v_s5_a_re=v_s5_a_re, v_s5_a_im=v_s5_a_im, v_s5_log_dt=v_s5_log_dt, v_s5_b_re=v_s5_b_re, v_s5_b_im=v_s5_b_im, v_s5_c_re=v_s5_c_re, v_s5_c_im=v_s5_c_im, v_s5_d=v_s5_d, v_s5_w_glu=v_s5_w_glu, v_gla_w_in=v_gla_w_in, v_gla_w_gate2=v_gla_w_gate2, v_gla_b_gate=v_gla_b_gate, v_gla_g_norm=v_gla_g_norm, v_gla_w_out=v_gla_w_out, v_w_ff1=v_w_ff1, v_w_ff2=v_w_ff2, v_norm_final=v_norm_final)
    weights = {n: given[n] for n in TWIN_WEIGHTS}
    shared = {n: given[n] for n in SHARED_INPUTS}
    per_example = {n: given[n] for n in ['x', 'c']}
    grad_fn = _jax.value_and_grad(_loss, argnums=(0, 1))

    def one_microbatch(ex, loss_target):
        ex = dict(ex)
        diff = ex.pop(TWIN_DIFF_INPUT)
        return grad_fn(weights, diff, {**shared, **ex}, loss_target)

    if N_MICROBATCH == 1:
        loss, (grad_w, grad_x) = one_microbatch(per_example, given["loss_target"])
    else:
        def body(carry, xs):
            loss_sum, grad_sum = carry
            l_k, (gw_k, gx_k) = one_microbatch(xs[0], xs[1])
            with _jax.named_scope("update"):
                return (loss_sum + l_k, _jax.tree.map(_jnp.add, grad_sum, gw_k)), gx_k

        init = (_jnp.zeros((), _jnp.float32), _jax.tree.map(_jnp.zeros_like, weights))
        (loss, grad_w), grad_x = _jax.lax.scan(body, init, (per_example, given["loss_target"]))
    with _jax.named_scope("update"):
        delta_w, new_m, new_v = {}, {}, {}
        for n in TWIN_WEIGHTS:
            delta_w[n], new_m[n], new_v[n] = _adamw(weights[n], grad_w[n], given["m_" + n], given["v_" + n])
    return (loss, grad_x, *[grad_w[n] for n in TWIN_WEIGHTS], *[delta_w[n] for n in TWIN_WEIGHTS],
            *[new_m[n] for n in TWIN_WEIGHTS], *[new_v[n] for n in TWIN_WEIGHTS])
```

```python
import functools
import math

import jax
import jax.numpy as jnp
from jax import lax
from jax.experimental import pallas as pl
from jax.experimental.pallas import tpu as pltpu

F32 = jnp.float32
BF16 = jnp.bfloat16
MXU_DT = BF16
EPS = 1e-6
N_DEV = 8
VMEM_LIMIT = 56 * 1024 * 1024

D_MODEL = 1024
S5_GROUP = 16
S5_GROUPS = 64
S5_STATE = 64
S5_NSTATE = S5_GROUPS * S5_STATE
S5_GB = 16
S5_NB = S5_GROUPS // S5_GB
S5_BC = S5_GB * S5_GROUP
S5_BS = S5_GB * S5_STATE
GLA_HEADS = 4
GLA_QK = 512
GLA_DK = 128
GLA_DV = 256
GLA_RANK = 16
GLA_RANK_PAD = 128
GLA_TAU = 16.0
GLA_CHUNK = 64
GLA_IN = 3088
GLA_INP = 2 * GLA_QK + 2 * D_MODEL + GLA_RANK_PAD
D_FF = 4096

ADAM_LR = 0.001
ADAM_B1 = 0.9
ADAM_B2 = 0.999
ADAM_EPS = 1e-08
ADAM_WD = 0.01
ADAM_STEP = 10

_NN = (((1,), (0,)), ((), ()))
_NT = (((1,), (1,)), ((), ()))
_TN = (((0,), (0,)), ((), ()))


def _dot(a, b, dn=_NN):
    return lax.dot_general(a.astype(MXU_DT), b.astype(MXU_DT), dn, preferred_element_type=F32)


def _dot_exact01(x, m01, dn=_NN):
    x1 = x.astype(BF16)
    r1 = x - x1.astype(F32)
    x2 = r1.astype(BF16)
    x3 = (r1 - x2.astype(F32)).astype(BF16)
    m = m01.astype(BF16)
    d = lambda u: lax.dot_general(u, m, dn, preferred_element_type=F32)
    return d(x1) + d(x2) + d(x3)


def _dot_f32(a, b, dn=_NN):
    def split(x):
        x1 = x.astype(BF16)
        r1 = x - x1.astype(F32)
        x2 = r1.astype(BF16)
        x3 = (r1 - x2.astype(F32)).astype(BF16)
        return x1, x2, x3
    a1, a2, a3 = split(a)
    b1, b2, b3 = split(b)
    d = lambda u, v: lax.dot_general(u, v, dn, preferred_element_type=F32)
    return (d(a1, b1) + (d(a1, b2) + d(a2, b1)) + (d(a2, b2) + d(a1, b3) + d(a3, b1)))


def _sigmoid(x):
    return 1.0 / (1.0 + jnp.exp(-x))


def _silu(x):
    return x * _sigmoid(x)


def _gelu(x):
    return 0.5 * x * (1.0 + jnp.tanh(math.sqrt(2.0 / math.pi) * (x + 0.044715 * (x * x * x))))


def _logsig(x):
    return jnp.minimum(x, 0.0) - jnp.log(1.0 + jnp.exp(-jnp.abs(x)))


def _rms(x):
    return lax.rsqrt(jnp.mean(x * x, axis=-1, keepdims=True) + EPS)


def _params(sem):
    return pltpu.CompilerParams(dimension_semantics=sem, vmem_limit_bytes=VMEM_LIMIT)


def _mm(a, b, dims, out_dtype, *, tm, tn, name, a_fn=None, out_fn=None, extra=None, by_owner=False):
    if dims == "tn":
        k, m = a.shape
        n = b.shape[1]
    else:
        m, k = a.shape
        n = b.shape[0] if dims == "nt" else b.shape[1]
    tm, tn = min(tm, m), min(tn, n)
    assert m % tm == 0 and n % tn == 0, (name, m, n, tm, tn)
    dn = {"nn": _NN, "nt": _NT, "tn": _TN}[dims]

    def body(*refs):
        a_ref, b_ref = refs[0], refs[1]
        o_ref = refs[-1]
        av = a_ref[...]
        if a_fn is not None:
            av = a_fn(av.astype(F32))
        acc = _dot(av, b_ref[...], dn)
        if extra is not None:
            acc = out_fn(acc, refs[2][...].astype(F32))
        elif out_fn is not None:
            acc = out_fn(acc)
        if by_owner:
            o_ref[0] = acc.astype(o_ref.dtype)
        else:
            o_ref[...] = acc.astype(o_ref.dtype)

    a_spec = pl.BlockSpec((k, tm), lambda i, j: (0, i)) if dims == "tn" else pl.BlockSpec((tm, k), lambda i, j: (i, 0))
    b_spec = pl.BlockSpec((tn, k), lambda i, j: (j, 0)) if dims == "nt" else pl.BlockSpec((k, tn), lambda i, j: (0, j))
    if by_owner:
        o_spec = pl.BlockSpec((1, tm, tn), lambda i, j: (j, i, 0))
        out_shape = jax.ShapeDtypeStruct((n // tn, m, tn), out_dtype)
    else:
        o_spec = pl.BlockSpec((tm, tn), lambda i, j: (i, j))
        out_shape = jax.ShapeDtypeStruct((m, n), out_dtype)
    in_specs, args = [a_spec, b_spec], [a, b]
    if extra is not None:
        in_specs.append(o_spec)
        args.append(extra)
    return pl.pallas_call(
        body, grid=(m // tm, n // tn), in_specs=in_specs, out_specs=o_spec, out_shape=out_shape, name=name,
        compiler_params=_params(("parallel", "parallel")),
    )(*args)


def _rowcall(f, rows, pars, outs, accs, *, tile, name):
    length = rows[0][0].shape[0]
    tile = min(tile, length)
    nr, npar, no = len(rows), len(pars), len(outs)

    def body(*refs):
        vals = [r[...].astype(F32) for r in refs[:nr + npar]]
        res = f(*vals)
        o_refs = refs[nr + npar:nr + npar + no]
        a_refs = refs[nr + npar + no:]
        for o, v in zip(o_refs, res[:no]):
            o[...] = v.astype(o.dtype)
        if a_refs:
            @pl.when(pl.program_id(0) == 0)
            def _():
                for a in a_refs:
                    a[...] = jnp.zeros(a.shape, F32)
            for a, v in zip(a_refs, res[no:]):
                a[...] += jnp.broadcast_to(v, a.shape)

    in_specs = [pl.BlockSpec((tile, w), lambda i, cb=cb: (i, cb)) for (_, w, cb) in rows]
    in_specs += [pl.BlockSpec(p.shape, lambda i: (0, 0)) for p in pars]
    out_specs = [pl.BlockSpec((tile, w), lambda i: (i, 0)) for (w, _) in outs]
    out_specs += [pl.BlockSpec(s, lambda i: (0, 0)) for s in accs]
    out_shape = [jax.ShapeDtypeStruct((length, w), dt) for (w, dt) in outs]
    out_shape += [jax.ShapeDtypeStruct(s, F32) for s in accs]
    return pl.pallas_call(
        body, grid=(length // tile,), in_specs=in_specs, out_specs=out_specs, out_shape=out_shape, name=name,
        compiler_params=_params(("arbitrary",)),
    )(*[r[0] for r in rows], *pars)


def _vjp_of(f, n_row, n_cot, row_want):
    def g(*a):
        prow, cots, par = a[:n_row], a[n_row:n_row + n_cot], a[n_row + n_cot:]
        _, vjp = jax.vjp(f, *prow, *par)
        grads = vjp(tuple(cots))
        return tuple(grads[i] for i in row_want) + tuple(grads[n_row:])
    return g


def _f_pn(x, g, sc, sh):
    return (x, x * _rms(x) * g * (1.0 + sc) + sh)


def _f_res_pn(x, y, gt, g, sc, sh):
    xn = x + gt * y
    return (xn, xn * _rms(xn) * g * (1.0 + sc) + sh)


def _f_glu_res_pn(x, val, gate, gt, g, sc, sh):
    xn = x + gt * (val * _sigmoid(gate))
    return (xn, xn * _rms(xn) * g * (1.0 + sc) + sh)


def _f_final(x, y, tgt, gt, g):
    xn = x + gt * y
    err = xn * _rms(xn) * g - tgt
    return 0.5 * jnp.mean(err * err, axis=-1, keepdims=True)


def _g_final(x, y, tgt, gt, g):
    lrow, vjp = jax.vjp(_f_final, x, y, tgt, gt, g)
    dx, dy, _, dgt, dg = vjp(jnp.ones_like(lrow))
    return dx, dy, dgt, dg, jnp.sum(lrow)


def _full(a):
    return (a, a.shape[1], 0)


def _s5_prep_f(a_re, a_im, log_dt, bt_re, bt_im, e01):
    dt = jnp.exp(_dot_exact01(log_dt, e01))
    mag = jnp.exp(a_re * dt)
    ph = a_im * dt
    lb_re = mag * jnp.cos(ph)
    lb_im = mag * jnp.sin(ph)
    den = a_re * a_re + a_im * a_im
    nr = lb_re - 1.0
    ni = lb_im
    f_re = (nr * a_re + ni * a_im) / den
    f_im = (ni * a_re - nr * a_im) / den
    bb_re = f_re * bt_re - f_im * bt_im
    bb_im = f_re * bt_im + f_im * bt_re
    return lb_re, lb_im, bb_re, bb_im


def _s5_prep_outs():
    return [jax.ShapeDtypeStruct((1, S5_NSTATE), F32)] * 2 + [jax.ShapeDtypeStruct((S5_GROUP, S5_NSTATE), F32)] * 2


def _s5_prep(a_re, a_im, log_dt, bt_re, bt_im, e01):
    def body(*refs):
        res = _s5_prep_f(*[r[...] for r in refs[:6]])
        for o, v in zip(refs[6:], res):
            o[...] = v
    return pl.pallas_call(body, out_shape=_s5_prep_outs(), name="s5_prep",
                          compiler_params=pltpu.CompilerParams(vmem_limit_bytes=VMEM_LIMIT))(a_re, a_im, log_dt, bt_re, bt_im, e01)


def _s5_prep_bwd(a_re, a_im, log_dt, bt_re, bt_im, e01, d_lb_re, d_lb_im, d_bb_re, d_bb_im):
    def f(a_re, a_im, log_dt, bt_re, bt_im, e01):
        @jax.custom_vjp
        def expand(v):
            return _dot_exact01(v, e01)
        expand.defvjp(lambda v: (_dot_exact01(v, e01), None), lambda _, ct: (_dot_exact01(ct, e01, _NT),))
        dt = jnp.exp(expand(log_dt))
        mag = jnp.exp(a_re * dt)
        ph = a_im * dt
        lb_re = mag * jnp.cos(ph)
        lb_im = mag * jnp.sin(ph)
        den = a_re * a_re + a_im * a_im
        nr = lb_re - 1.0
        f_re = (nr * a_re + lb_im * a_im) / den
        f_im = (lb_im * a_re - nr * a_im) / den
        return lb_re, lb_im, f_re * bt_re - f_im * bt_im, f_re * bt_im + f_im * bt_re

    def body(*refs):
        ins = [r[...] for r in refs[:5]]
        e = refs[5][...]
        cots = tuple(r[...] for r in refs[6:10])
        _, vjp = jax.vjp(lambda *p: f(*p, e), *ins)
        for o, v in zip(refs[10:], vjp(cots)):
            o[...] = v
    outs = [jax.ShapeDtypeStruct(v.shape, F32) for v in (a_re, a_im, log_dt, bt_re, bt_im)]
    return pl.pallas_call(body, out_shape=outs, name="s5_prep_bwd",
                          compiler_params=pltpu.CompilerParams(vmem_limit_bytes=VMEM_LIMIT))(
        a_re, a_im, log_dt, bt_re, bt_im, e01, d_lb_re, d_lb_im, d_bb_re, d_bb_im)


def _s5_scan(x_re, x_im, a_r, a_i, c_r, c_i, n_tiles, reverse):
    sgn = -1.0 if reverse else 1.0

    def tile(k, carry):
        cr, ci = carry
        i = (n_tiles - 1 - k) if reverse else k
        order = range(7, -1, -1) if reverse else range(8)
        for j in order:
            br = x_re[i, pl.ds(j, 1), :]
            bi = x_im[i, pl.ds(j, 1), :]
            nr = a_r * cr - (sgn * a_i) * ci + br
            ni = a_r * ci + (sgn * a_i) * cr + bi
            x_re[i, pl.ds(j, 1), :] = nr
            x_im[i, pl.ds(j, 1), :] = ni
            cr, ci = nr, ni
        return cr, ci

    return lax.fori_loop(0, n_tiles, tile, (c_r, c_i))


def _s5_fwd(u, lam_re, lam_im, bbd_re, bbd_im, cbd_re, cbd_im, d_skip, *, tc):
    length = u.shape[0]
    tc = min(tc, length)
    nt = length // tc

    def body(u_ref, lr_ref, li_ref, br_ref, bi_ref, cr_ref, ci_ref, d_ref, z_ref, sr_ref, si_ref, xr, xi, car_r, car_i):
        @pl.when(pl.program_id(1) == 0)
        def _():
            car_r[...] = jnp.zeros_like(car_r)
            car_i[...] = jnp.zeros_like(car_i)
        sr_ref[0] = car_r[...]
        si_ref[0] = car_i[...]
        uv = u_ref[...]
        xr[...] = _dot(uv, br_ref[0]).reshape(tc // 8, 8, S5_BS)
        xi[...] = _dot(uv, bi_ref[0]).reshape(tc // 8, 8, S5_BS)
        cr, ci = _s5_scan(xr, xi, lr_ref[...], li_ref[...], car_r[...], car_i[...], tc // 8, False)
        car_r[...] = cr
        car_i[...] = ci
        y = (_dot(xr[...].reshape(tc, S5_BS), cr_ref[0]) - _dot(xi[...].reshape(tc, S5_BS), ci_ref[0]) + d_ref[...] * uv)
        z_ref[...] = _gelu(y).astype(z_ref.dtype)

    blk_u = pl.BlockSpec((tc, S5_BC), lambda g, t: (t, g))
    blk_l = pl.BlockSpec((1, S5_BS), lambda g, t: (0, g))
    blk_b = pl.BlockSpec((1, S5_BC, S5_BS), lambda g, t: (g, 0, 0))
    blk_c = pl.BlockSpec((1, S5_BS, S5_BC), lambda g, t: (g, 0, 0))
    blk_d = pl.BlockSpec((1, S5_BC), lambda g, t: (0, g))
    blk_s = pl.BlockSpec((1, 1, S5_BS), lambda g, t: (t, 0, g))
    return pl.pallas_call(
        body, grid=(S5_NB, nt),
        in_specs=[blk_u, blk_l, blk_l, blk_b, blk_b, blk_c, blk_c, blk_d],
        out_specs=[blk_u, blk_s, blk_s],
        out_shape=[jax.ShapeDtypeStruct((length, D_MODEL), MXU_DT),
                   jax.ShapeDtypeStruct((nt, 1, S5_NSTATE), F32), jax.ShapeDtypeStruct((nt, 1, S5_NSTATE), F32)],
        scratch_shapes=[pltpu.VMEM((tc // 8, 8, S5_BS), F32), pltpu.VMEM((tc // 8, 8, S5_BS), F32),
                        pltpu.VMEM((1, S5_BS), F32), pltpu.VMEM((1, S5_BS), F32)],
        name="s5_fwd", compiler_params=_params(("arbitrary", "arbitrary")),
    )(u, lam_re, lam_im, bbd_re, bbd_im, cbd_re, cbd_im, d_skip)


def _s5_bwd(u, dz, st_re, st_im, lam_re, lam_im, bbd_re, bbd_im, cbd_re, cbd_im, d_skip, *, tc):
    length = u.shape[0]
    tc = min(tc, length)
    nt = length // tc

    def body(u_ref, dz_ref, sr_ref, si_ref, lr_ref, li_ref, br_ref, bi_ref, cr_ref, ci_ref, d_ref,
             du_ref, dbr_ref, dbi_ref, dcr_ref, dci_ref, dlr_ref, dli_ref, dd_ref, xr, xi, gr, gi, car_r, car_i):
        @pl.when(pl.program_id(1) == 0)
        def _():
            car_r[...] = jnp.zeros_like(car_r)
            car_i[...] = jnp.zeros_like(car_i)
            for r in (dbr_ref, dbi_ref, dcr_ref, dci_ref, dlr_ref, dli_ref, dd_ref):
                r[...] = jnp.zeros(r.shape, F32)
        a_r, a_i = lr_ref[...], li_ref[...]
        uv = u_ref[...]
        xr[...] = _dot(uv, br_ref[0]).reshape(tc // 8, 8, S5_BS)
        xi[...] = _dot(uv, bi_ref[0]).reshape(tc // 8, 8, S5_BS)
        _s5_scan(xr, xi, a_r, a_i, sr_ref[0], si_ref[0], tc // 8, False)
        xrv = xr[...].reshape(tc, S5_BS)
        xiv = xi[...].reshape(tc, S5_BS)
        y = _dot(xrv, cr_ref[0]) - _dot(xiv, ci_ref[0]) + d_ref[...] * uv
        _, gelu_vjp = jax.vjp(_gelu, y)
        dy = gelu_vjp(dz_ref[...].astype(F32))[0]
        dd_ref[...] += jnp.sum(dy * uv, axis=0, keepdims=True)
        dcr_ref[0] += _dot(xrv, dy, _TN)
        dci_ref[0] -= _dot(xiv, dy, _TN)
        gr[...] = _dot(dy, cr_ref[0], _NT).reshape(tc // 8, 8, S5_BS)
        gi[...] = (-_dot(dy, ci_ref[0], _NT)).reshape(tc // 8, 8, S5_BS)
        cr, ci = _s5_scan(gr, gi, a_r, a_i, car_r[...], car_i[...], tc // 8, True)
        car_r[...] = cr
        car_i[...] = ci
        grv = gr[...].reshape(tc, S5_BS)
        giv = gi[...].reshape(tc, S5_BS)
        first = lax.broadcasted_iota(jnp.int32, (tc, 1), 0) == 0
        xpr = jnp.where(first, sr_ref[0], pltpu.roll(xrv, 1, 0))
        xpi = jnp.where(first, si_ref[0], pltpu.roll(xiv, 1, 0))
        dlr_ref[...] += jnp.sum(grv * xpr + giv * xpi, axis=0, keepdims=True)
        dli_ref[...] += jnp.sum(giv * xpr - grv * xpi, axis=0, keepdims=True)
        dbr_ref[0] += _dot(uv, grv, _TN)
        dbi_ref[0] += _dot(uv, giv, _TN)
        du_ref[...] = _dot(grv, br_ref[0], _NT) + _dot(giv, bi_ref[0], _NT) + d_ref[...] * dy

    rev = lambda t: nt - 1 - t
    blk_u = pl.BlockSpec((tc, S5_BC), lambda g, t: (rev(t), g))
    blk_l = pl.BlockSpec((1, S5_BS), lambda g, t: (0, g))
    blk_b = pl.BlockSpec((1, S5_BC, S5_BS), lambda g, t: (g, 0, 0))
    blk_c = pl.BlockSpec((1, S5_BS, S5_BC), lambda g, t: (g, 0, 0))
    blk_d = pl.BlockSpec((1, S5_BC), lambda g, t: (0, g))
    blk_s = pl.BlockSpec((1, 1, S5_BS), lambda g, t: (rev(t), 0, g))
    return pl.pallas_call(
        body, grid=(S5_NB, nt),
        in_specs=[blk_u, blk_u, blk_s, blk_s, blk_l, blk_l, blk_b, blk_b, blk_c, blk_c, blk_d],
        out_specs=[blk_u, blk_b, blk_b, blk_c, blk_c, blk_l, blk_l, blk_d],
        out_shape=[jax.ShapeDtypeStruct((length, D_MODEL), F32),
                   jax.ShapeDtypeStruct((S5_NB, S5_BC, S5_BS), F32), jax.ShapeDtypeStruct((S5_NB, S5_BC, S5_BS), F32),
                   jax.ShapeDtypeStruct((S5_NB, S5_BS, S5_BC), F32), jax.ShapeDtypeStruct((S5_NB, S5_BS, S5_BC), F32),
                   jax.ShapeDtypeStruct((1, S5_NSTATE), F32), jax.ShapeDtypeStruct((1, S5_NSTATE), F32),
                   jax.ShapeDtypeStruct((1, D_MODEL), F32)],
        scratch_shapes=[pltpu.VMEM((tc // 8, 8, S5_BS), F32), pltpu.VMEM((tc // 8, 8, S5_BS), F32),
                        pltpu.VMEM((tc // 8, 8, S5_BS), F32), pltpu.VMEM((tc // 8, 8, S5_BS), F32),
                        pltpu.VMEM((1, S5_BS), F32), pltpu.VMEM((1, S5_BS), F32)],
        name="s5_bwd", compiler_params=_params(("arbitrary", "arbitrary")),
    )(u, dz, st_re, st_im, lam_re, lam_im, bbd_re, bbd_im, cbd_re, cbd_im, d_skip)


def _blockdiag_b(bt):
    eye = jnp.eye(S5_GB, dtype=bt.dtype)
    t = bt.reshape(S5_GROUP, S5_NB, S5_GB, S5_STATE)
    return jnp.einsum("ab,hnbp->nahbp", eye, t).reshape(S5_NB, S5_BC, S5_BS)


def _unblockdiag_b(m):
    eye = jnp.eye(S5_GB, dtype=m.dtype)
    t = m.reshape(S5_NB, S5_GB, S5_GROUP, S5_GB, S5_STATE)
    return jnp.einsum("ab,nahbp->hnbp", eye, t).reshape(S5_GROUP, S5_NSTATE)


def _blockdiag_c(c):
    eye = jnp.eye(S5_GB, dtype=c.dtype)
    t = c.reshape(S5_NB, S5_GB, S5_GROUP, S5_STATE)
    return jnp.einsum("ab,nbhp->napbh", eye, t).reshape(S5_NB, S5_BS, S5_BC)


def _unblockdiag_c(m):
    eye = jnp.eye(S5_GB, dtype=m.dtype)
    t = m.reshape(S5_NB, S5_GB, S5_STATE, S5_GB, S5_GROUP)
    return jnp.einsum("ab,napbh->nbhp", eye, t).reshape(S5_GROUPS, S5_GROUP, S5_STATE)


def _gla_gates(glr, wg2, bg, tri):
    pre = _dot(glr, wg2) + bg
    la = _logsig(pre) * (1.0 / GLA_TAU)
    gc = _dot_f32(tri, la)
    gend = gc[GLA_CHUNK - 1:GLA_CHUNK, :]
    e = jnp.exp(gend - gc)
    return pre, e, jnp.exp(gend)


def _gla_specs(nc, rev):
    ix = (lambda n: nc - 1 - n) if rev else (lambda n: n)
    c = GLA_CHUNK
    return dict(
        q=pl.BlockSpec((c, GLA_QK), lambda n: (ix(n), 0)),
        k=pl.BlockSpec((c, GLA_QK), lambda n: (ix(n), 1)),
        v=pl.BlockSpec((c, D_MODEL), lambda n: (ix(n), 1)),
        r=pl.BlockSpec((c, D_MODEL), lambda n: (ix(n), 2)),
        glr=pl.BlockSpec((c, GLA_RANK_PAD), lambda n: (ix(n), (2 * GLA_QK + 2 * D_MODEL) // GLA_RANK_PAD)),
        wg2=pl.BlockSpec((GLA_RANK_PAD, GLA_QK), lambda n: (0, 0)),
        bg=pl.BlockSpec((1, GLA_QK), lambda n: (0, 0)),
        gn=pl.BlockSpec((1, D_MODEL), lambda n: (0, 0)),
        tri=pl.BlockSpec((c, c), lambda n: (0, 0)),
        row=pl.BlockSpec((c, D_MODEL), lambda n: (ix(n), 0)),
        rowqk=pl.BlockSpec((c, GLA_QK), lambda n: (ix(n), 0)),
        rowlr=pl.BlockSpec((c, GLA_RANK_PAD), lambda n: (ix(n), 0)),
        st=pl.BlockSpec((1, GLA_HEADS, GLA_DV, GLA_DK), lambda n: (ix(n), 0, 0, 0)),
    )


def _gla_fwd(proj, wg2, bg, gn, tri):
    length = proj.shape[0]
    nc = length // GLA_CHUNK
    scale = GLA_DK ** -0.5

    def body(q_ref, k_ref, v_ref, r_ref, glr_ref, wg2_ref, bg_ref, gn_ref, tri_ref, og_ref, sp_ref, st):
        @pl.when(pl.program_id(0) == 0)
        def _():
            st[...] = jnp.zeros_like(st)
        _, e, dec = _gla_gates(glr_ref[...], wg2_ref[...], bg_ref[...], tri_ref[...])
        kd = k_ref[...].astype(F32) * e
        q = q_ref[...].astype(F32) * scale
        for h in range(GLA_HEADS):
            sk = slice(h * GLA_DK, (h + 1) * GLA_DK)
            sv = slice(h * GLA_DV, (h + 1) * GLA_DV)
            sp_ref[0, h] = st[h]
            stn = dec[:, sk] * st[h] + _dot(v_ref[:, sv], kd[:, sk], _TN)
            st[h] = stn
            o = _dot(q[:, sk], stn, _NT)
            on = o * _rms(o)
            og_ref[:, sv] = (on * gn_ref[:, sv] * _silu(r_ref[:, sv].astype(F32))).astype(og_ref.dtype)

    s = _gla_specs(nc, False)
    return pl.pallas_call(
        body, grid=(nc,),
        in_specs=[s["q"], s["k"], s["v"], s["r"], s["glr"], s["wg2"], s["bg"], s["gn"], s["tri"]],
        out_specs=[s["row"], s["st"]],
        out_shape=[jax.ShapeDtypeStruct((length, D_MODEL), MXU_DT),
                   jax.ShapeDtypeStruct((nc, GLA_HEADS, GLA_DV, GLA_DK), F32)],
        scratch_shapes=[pltpu.VMEM((GLA_HEADS, GLA_DV, GLA_DK), F32)],
        name="gla_fwd", compiler_params=_params(("arbitrary",)),
    )(proj, proj, proj, proj, proj, wg2, bg, gn, tri)


def _gla_bwd(proj, d_og, s_prev, wg2, bg, gn, tri):
    length = proj.shape[0]
    nc = length // GLA_CHUNK
    scale = GLA_DK ** -0.5

    def body(q_ref, k_ref, v_ref, r_ref, glr_ref, dog_ref, sp_ref, wg2_ref, bg_ref, gn_ref, tri_ref,
             dq_ref, dk_ref, dv_ref, dr_ref, dglr_ref, dwg2_ref, dbg_ref, dgn_ref, dst):
        @pl.when(pl.program_id(0) == 0)
        def _():
            dst[...] = jnp.zeros_like(dst)
            dwg2_ref[...] = jnp.zeros_like(dwg2_ref)
            dbg_ref[...] = jnp.zeros_like(dbg_ref)
            dgn_ref[...] = jnp.zeros_like(dgn_ref)
        glr = glr_ref[...]
        pre, e, dec = _gla_gates(glr, wg2_ref[...], bg_ref[...], tri_ref[...])
        k = k_ref[...].astype(F32)
        kd = k * e
        q = q_ref[...].astype(F32) * scale
        dkd_parts, ddec_parts = [], []
        for h in range(GLA_HEADS):
            sk = slice(h * GLA_DK, (h + 1) * GLA_DK)
            sv = slice(h * GLA_DV, (h + 1) * GLA_DV)
            stp = sp_ref[0, h]
            vh = v_ref[:, sv]
            stn = dec[:, sk] * stp + _dot(vh, kd[:, sk], _TN)
            o = _dot(q[:, sk], stn, _NT)
            rinv = _rms(o)
            on = o * rinv
            rv = r_ref[:, sv].astype(F32)
            sg = _sigmoid(rv)
            sr = rv * sg
            dog = dog_ref[:, sv].astype(F32)
            gnh = gn_ref[:, sv]
            d_ong = dog * sr
            dr_ref[:, sv] = (dog * (on * gnh) * (sg * (1.0 + rv * (1.0 - sg)))).astype(dr_ref.dtype)
            dgn_ref[:, sv] += jnp.sum(d_ong * on, axis=0, keepdims=True)
            d_on = d_ong * gnh
            do = rinv * (d_on - on * jnp.mean(d_on * on, axis=-1, keepdims=True))
            dq_ref[:, sk] = (_dot(do, stn) * scale).astype(dq_ref.dtype)
            dstn = dst[h] + _dot(do, q[:, sk], _TN)
            dst[h] = dec[:, sk] * dstn
            ddec_parts.append(jnp.sum(dstn * stp, axis=0, keepdims=True))
            dv_ref[:, sv] = _dot(kd[:, sk], dstn, _NT).astype(dv_ref.dtype)
            dkd_parts.append(_dot(vh, dstn))
        dkd = jnp.concatenate(dkd_parts, axis=1)
        ddec = jnp.concatenate(ddec_parts, axis=1)
        dk_ref[...] = (dkd * e).astype(dk_ref.dtype)
        w = dkd * kd
        dgend = jnp.sum(w, axis=0, keepdims=True) + ddec * dec
        dla = dgend - _dot_f32(tri_ref[...], w, _TN)
        dpre = dla * (1.0 - _sigmoid(pre)) * (1.0 / GLA_TAU)
        dwg2_ref[...] += _dot(glr, dpre, _TN)
        dbg_ref[...] += jnp.sum(dpre, axis=0, keepdims=True)
        dglr_ref[...] = _dot(dpre, wg2_ref[...], _NT).astype(dglr_ref.dtype)

    s = _gla_specs(nc, True)
    return pl.pallas_call(
        body, grid=(nc,),
        in_specs=[s["q"], s["k"], s["v"], s["r"], s["glr"], s["row"], s["st"], s["wg2"], s["bg"], s["gn"], s["tri"]],
        out_specs=[s["rowqk"], s["rowqk"], s["row"], s["row"], s["rowlr"], s["wg2"], s["bg"], s["gn"]],
        out_shape=[jax.ShapeDtypeStruct((length, GLA_QK), MXU_DT), jax.ShapeDtypeStruct((length, GLA_QK), MXU_DT),
                   jax.ShapeDtypeStruct((length, D_MODEL), MXU_DT), jax.ShapeDtypeStruct((length, D_MODEL), MXU_DT),
                   jax.ShapeDtypeStruct((length, GLA_RANK_PAD), MXU_DT),
                   jax.ShapeDtypeStruct((GLA_RANK_PAD, GLA_QK), F32), jax.ShapeDtypeStruct((1, GLA_QK), F32),
                   jax.ShapeDtypeStruct((1, D_MODEL), F32)],
        scratch_shapes=[pltpu.VMEM((GLA_HEADS, GLA_DV, GLA_DK), F32)],
        name="gla_bwd", compiler_params=_params(("arbitrary",)),
    )(proj, proj, proj, proj, proj, d_og, s_prev, wg2, bg, gn, tri)


def _local_step(x, tgt, mods, nrm, s5p, w, *, row_tile=256, s5_tc=256):
    length = x.shape[0]
    tmm = 512
    rc = functools.partial(_rowcall, tile=row_tile)
    (sh1a, sc1a, gt1a, sh2a, sc2a, gt2a), (sh1b, sc1b, gt1b, sh2b, sc2b, gt2b) = mods
    vec = (1, D_MODEL)
    row32, row16 = (D_MODEL, F32), (D_MODEL, MXU_DT)

    (h0,) = rc(lambda *a: _f_pn(*a)[1:], [_full(x)], [nrm["mix"][0], sc1a, sh1a], [row32], [], name="pn0")
    lam_re, lam_im, bb_re, bb_im = _s5_prep(s5p["a_re"], s5p["a_im"], s5p["log_dt"], s5p["bt_re"], s5p["bt_im"], s5p["e01"])
    bbd_re = _blockdiag_b(bb_re.reshape(S5_GROUP, S5_GROUPS, S5_STATE)).astype(MXU_DT)
    bbd_im = _blockdiag_b(bb_im.reshape(S5_GROUP, S5_GROUPS, S5_STATE)).astype(MXU_DT)
    cbd_re = _blockdiag_c(s5p["c_re"]).astype(MXU_DT)
    cbd_im = _blockdiag_c(s5p["c_im"]).astype(MXU_DT)
    z0, st_re, st_im = _s5_fwd(h0, lam_re, lam_im, bbd_re, bbd_im, cbd_re, cbd_im, s5p["d"], tc=s5_tc)
    vg = _mm(z0, w["glu"], "nn", F32, tm=tmm, tn=2048, name="glu_mm")
    x1, h1 = rc(_f_glu_res_pn, [_full(x), (vg, D_MODEL, 0), (vg, D_MODEL, 1)], [gt1a, nrm["mlp"][0], sc2a, sh2a],
                [row32, row16], [], name="node1")
    relu = lambda acc: jnp.maximum(acc, 0.0)
    sq = lambda a: a * a
    a0 = _mm(h1, w["ff1"][0], "nn", MXU_DT, tm=tmm, tn=2048, name="ff1a", out_fn=relu)
    f0 = _mm(a0, w["ff2"][0], "nn", F32, tm=tmm, tn=1024, name="ff2a", a_fn=sq)
    x2, h2 = rc(_f_res_pn, [_full(x1), _full(f0)], [gt2a, nrm["mix"][1], sc1b, sh1b], [row32, row16], [], name="node2")
    proj = _mm(h2, w["gin"], "nn", MXU_DT, tm=tmm, tn=GLA_INP, name="gla_in")
    og, s_prev = _gla_fwd(proj, w["wg2"], w["bg"], w["gn"], w["tri"])
    y1 = _mm(og, w["gout"], "nn", F32, tm=tmm, tn=1024, name="gla_out")
    x3, h3 = rc(_f_res_pn, [_full(x2), _full(y1)], [gt1b, nrm["mlp"][1], sc2b, sh2b], [row32, row16], [], name="node3")
    a1 = _mm(h3, w["ff1"][1], "nn", MXU_DT, tm=tmm, tn=2048, name="ff1b", out_fn=relu)
    f1 = _mm(a1, w["ff2"][1], "nn", F32, tm=tmm, tn=1024, name="ff2b", a_fn=sq)

    g = {}
    dx, df, g["gt2b"], g["nf"], loss = rc(_g_final, [_full(x3), _full(f1), _full(tgt)], [gt2b, nrm["final"]],
                                          [row32, row16], [vec, vec, (8, 128)], name="final")

    def mlp_bwd(df, a, h, w1, w2, tag):
        dw2 = _mm(a, df, "tn", MXU_DT, tm=1024, tn=1024, name="dff2" + tag, a_fn=sq)
        du = _mm(df, w2, "nt", MXU_DT, tm=tmm, tn=2048, name="dact" + tag, extra=a, out_fn=lambda acc, e: acc * (2.0 * e))
        dw1 = _mm(h, du, "tn", MXU_DT, tm=1024, tn=D_FF // N_DEV, name="dff1" + tag, by_owner=True)
        dh = _mm(du, w1, "nt", F32, tm=tmm, tn=1024, name="dh" + tag)
        return dw1, dw2, dh

    def node_bwd(f, prim_rows, cots, pars, row_want, outs, name):
        nrow = len(prim_rows)
        return rc(_vjp_of(f, nrow, len(cots), row_want), prim_rows + cots, pars, outs, [vec] * len(pars), name=name)

    g["ff1b"], g["ff2b"], dh3 = mlp_bwd(df, a1, h3, w["ff1"][1], w["ff2"][1], "b")
    dx, dy1, g["gt1b"], g["mlp1"], g["sc2b"], g["sh2b"] = node_bwd(
        _f_res_pn, [_full(x2), _full(y1)], [_full(dx), _full(dh3)], [gt1b, nrm["mlp"][1], sc2b, sh2b], (0, 1),
        [row32, row16], "node3_bwd")
    g["gout"] = _mm(og, dy1, "tn", MXU_DT, tm=512, tn=1024, name="dgout")
    d_og = _mm(dy1, w["gout"], "nt", MXU_DT, tm=tmm, tn=1024, name="dog")
    dq, dk, dv, dr, dglr, g["wg2"], g["bg"], g["gn"] = _gla_bwd(proj, d_og, s_prev, w["wg2"], w["bg"], w["gn"], w["tri"])
    dproj = jnp.concatenate([dq, dk, dv, dr, dglr], axis=1)
    g["gin"] = _mm(h2, dproj, "tn", MXU_DT, tm=512, tn=640, name="dgin")
    dh2 = _mm(dproj, w["gin"], "nt", F32, tm=tmm, tn=1024, name="dh2")
    dx, df0, g["gt2a"], g["mix1"], g["sc1b"], g["sh1b"] = node_bwd(
        _f_res_pn, [_full(x1), _full(f0)], [_full(dx), _full(dh2)], [gt2a, nrm["mix"][1], sc1b, sh1b], (0, 1),
        [row32, row16], "node2_bwd")
    g["ff1a"], g["ff2a"], dh1 = mlp_bwd(df0, a0, h1, w["ff1"][0], w["ff2"][0], "a")
    dx, dval, dgate, g["gt1a"], g["mlp0"], g["sc2a"], g["sh2a"] = node_bwd(
        _f_glu_res_pn, [_full(x), (vg, D_MODEL, 0), (vg, D_MODEL, 1)], [_full(dx), _full(dh1)],
        [gt1a, nrm["mlp"][0], sc2a, sh2a], (0, 1, 2), [row32, row16, row16], "node1_bwd")
    dvg = jnp.concatenate([dval, dgate], axis=1)
    g["glu"] = _mm(z0, dvg, "tn", MXU_DT, tm=1024, tn=2 * D_MODEL // N_DEV, name="dglu", by_owner=True)
    dz0 = _mm(dvg, w["glu"], "nt", MXU_DT, tm=tmm, tn=1024, name="dz0")
    du0, dbbd_re, dbbd_im, dcbd_re, dcbd_im, dlam_re, dlam_im, g["s5_d"] = _s5_bwd(
        h0, dz0, st_re, st_im, lam_re, lam_im, bbd_re, bbd_im, cbd_re, cbd_im, s5p["d"], tc=s5_tc)
    g["s5_c_re"] = _unblockdiag_c(dcbd_re)
    g["s5_c_im"] = _unblockdiag_c(dcbd_im)
    g["s5_a_re"], g["s5_a_im"], g["s5_log_dt"], g["s5_bt_re"], g["s5_bt_im"] = _s5_prep_bwd(
        s5p["a_re"], s5p["a_im"], s5p["log_dt"], s5p["bt_re"], s5p["bt_im"], s5p["e01"],
        dlam_re, dlam_im, _unblockdiag_b(dbbd_re), _unblockdiag_b(dbbd_im))
    grad_x, g["mix0"], g["sc1a"], g["sh1a"] = node_bwd(
        _f_pn, [_full(x)], [_full(dx), _full(du0)], [nrm["mix"][0], sc1a, sh1a], (0,), [row32], "node0_bwd")
    return loss[0, 0], grad_x, g


_MESH = pl.DeviceIdType.MESH
_VMEM_SPEC = pl.BlockSpec(memory_space=pltpu.VMEM)
_ANY_SPEC = pl.BlockSpec(memory_space=pl.ANY)


def _my_place():
    ix, iy, ic = lax.axis_index("x"), lax.axis_index("y"), lax.axis_index("c")
    return ix, iy, ic


def _exchange(x, *, gather, name):
    r = x.shape[-2]

    def body(x_ref, o_ref, ssem, rsem):
        ix, iy, ic = _my_place()
        me = 4 * ix + 2 * iy + ic
        if gather:
            o_ref[me] = x_ref[...]
        else:
            o_ref[me] = x_ref[me]
        copies = []
        for k in range(1, N_DEV):
            tx, ty, tc = ix ^ (k >> 2), iy ^ ((k >> 1) & 1), ic ^ (k & 1)
            src = x_ref if gather else x_ref.at[4 * tx + 2 * ty + tc]
            cp = pltpu.make_async_remote_copy(src_ref=src, dst_ref=o_ref.at[me], send_sem=ssem.at[k - 1],
                                              recv_sem=rsem.at[k - 1], device_id=(tx, ty, tc), device_id_type=_MESH)
            cp.start()
            copies.append(cp)
        for cp in copies:
            cp.wait()

    return pl.pallas_call(
        body, out_shape=jax.ShapeDtypeStruct((N_DEV, r, 128), x.dtype), in_specs=[_VMEM_SPEC], out_specs=_VMEM_SPEC,
        scratch_shapes=[pltpu.SemaphoreType.DMA((N_DEV - 1,)), pltpu.SemaphoreType.DMA((N_DEV - 1,))], name=name,
    )(x)


def _all_gather_big(xs):
    n = len(xs)

    def body(*refs):
        x_refs, out_refs = refs[:n], refs[n:2 * n]
        send_sems, recv_sems, local_sems = refs[2 * n:]
        ix, iy, ic = _my_place()
        me, sibling = (ix, iy, ic), (ix, iy, 1 - ic)
        chips = [(1 - ix, iy), (ix, 1 - iy), (1 - ix, 1 - iy)]

        def copy(a, k, block, to, src=None):
            px, py, pc = block
            slot = out_refs[a].at[4 * px + 2 * py + pc]
            return pltpu.make_async_remote_copy(
                src_ref=slot if src is None else src, dst_ref=slot, send_sem=send_sems.at[7 * a + k],
                recv_sem=recv_sems.at[7 * a + k], device_id=to, device_id_type=_MESH)

        mine, first, passed = [], [], []
        for a in range(n):
            mine.append(pltpu.make_async_copy(x_refs[a], out_refs[a].at[4 * ix + 2 * iy + ic], local_sems.at[a]))
            mine[-1].start()
            first.append(copy(a, 0, me, sibling, src=x_refs[a]))
            first += [copy(a, 1 + j, me, (*chip, ic), src=x_refs[a]) for j, chip in enumerate(chips)]
        for cp in first:
            cp.start()
        for a in range(n):
            for j, chip in enumerate(chips):
                copy(a, 1 + j, (*chip, ic), me).wait_recv()
                passed.append(copy(a, 4 + j, (*chip, ic), sibling))
                passed[-1].start()
        for a in range(n):
            copy(a, 0, sibling, me).wait_recv()
            for j, chip in enumerate(chips):
                copy(a, 4 + j, (*chip, 1 - ic), me).wait_recv()
        for cp in first + passed:
            cp.wait_send()
        for cp in mine:
            cp.wait()

    return pl.pallas_call(
        body, out_shape=[jax.ShapeDtypeStruct((N_DEV,) + x.shape, x.dtype) for x in xs], in_specs=[_ANY_SPEC] * n,
        out_specs=[_ANY_SPEC] * n,
        scratch_shapes=[pltpu.SemaphoreType.DMA((7 * n,)), pltpu.SemaphoreType.DMA((7 * n,)), pltpu.SemaphoreType.DMA((n,))],
        name="all_gather_big",
    )(*xs)


def _swap_sibling(gs):
    n = len(gs)

    def body(*refs):
        g_refs, o_refs = refs[:n], refs[n:2 * n]
        ssem, rsem = refs[2 * n:]
        ix, iy, ic = _my_place()
        copies = []
        for a in range(n):
            for q in range(4):
                cp = pltpu.make_async_remote_copy(src_ref=g_refs[a].at[2 * q + 1 - ic], dst_ref=o_refs[a].at[q],
                                                  send_sem=ssem.at[4 * a + q], recv_sem=rsem.at[4 * a + q],
                                                  device_id=(ix, iy, 1 - ic), device_id_type=_MESH)
                cp.start()
                copies.append(cp)
        for cp in copies:
            cp.wait()

    return pl.pallas_call(
        body, out_shape=[jax.ShapeDtypeStruct((4,) + g.shape[1:], g.dtype) for g in gs], in_specs=[_ANY_SPEC] * n,
        out_specs=[_ANY_SPEC] * n,
        scratch_shapes=[pltpu.SemaphoreType.DMA((4 * n,)), pltpu.SemaphoreType.DMA((4 * n,))], name="rs_sibling",
    )(*gs)


def _exchange_chips(ps):
    n = len(ps)

    def body(*refs):
        p_refs, o_refs = refs[:n], refs[n:2 * n]
        ssem, rsem = refs[2 * n:]
        ix, iy, ic = _my_place()
        copies = []
        for a in range(n):
            for k in range(1, 4):
                tx, ty = ix ^ (k >> 1), iy ^ (k & 1)
                cp = pltpu.make_async_remote_copy(src_ref=p_refs[a].at[2 * tx + ty], dst_ref=o_refs[a].at[k - 1],
                                                  send_sem=ssem.at[3 * a + k - 1], recv_sem=rsem.at[3 * a + k - 1],
                                                  device_id=(tx, ty, ic), device_id_type=_MESH)
                cp.start()
                copies.append(cp)
        for cp in copies:
            cp.wait()

    return pl.pallas_call(
        body, out_shape=[jax.ShapeDtypeStruct((3,) + p.shape[1:], p.dtype) for p in ps], in_specs=[_ANY_SPEC] * n,
        out_specs=[_ANY_SPEC] * n,
        scratch_shapes=[pltpu.SemaphoreType.DMA((3 * n,)), pltpu.SemaphoreType.DMA((3 * n,))], name="rs_chips",
    )(*ps)


def _ada_fwd(c_all, w_ada, b_cols):
    def body(c_ref, w_ref, b_ref, o_ref):
        cs = _silu(c_ref[...])
        for i in range(2):
            o_ref[i] = _dot(cs, w_ref[i]) + b_ref[pl.ds(i, 1), :]
    return pl.pallas_call(body, out_shape=jax.ShapeDtypeStruct((2, N_DEV, w_ada.shape[2]), F32), name="ada_fwd",
                          compiler_params=pltpu.CompilerParams(vmem_limit_bytes=VMEM_LIMIT))(c_all, w_ada, b_cols)


def _ada_bwd(c_all, dm):
    def body(c_ref, d_ref, o_ref):
        cs = _silu(c_ref[...])
        for i in range(2):
            o_ref[i] = _dot(cs, d_ref[i], _TN)
    return pl.pallas_call(body, out_shape=jax.ShapeDtypeStruct((2, D_MODEL, dm.shape[2]), F32), name="ada_bwd",
                          compiler_params=pltpu.CompilerParams(vmem_limit_bytes=VMEM_LIMIT))(c_all, dm)


def _pair_add(g, recv, core, *, name):
    _, r, cdim = g.shape

    def body(core_ref, g_ref, r_ref, o_ref):
        o_ref[...] = (g_ref[...].astype(F32) + r_ref[...].astype(F32)).astype(o_ref.dtype)

    grid_spec = pltpu.PrefetchScalarGridSpec(
        num_scalar_prefetch=1, grid=(4,),
        in_specs=[pl.BlockSpec((1, r, cdim), lambda q, core_ref: (2 * q + core_ref[0], 0, 0)),
                  pl.BlockSpec((1, r, cdim), lambda q, core_ref: (q, 0, 0))],
        out_specs=pl.BlockSpec((1, r, cdim), lambda q, core_ref: (q, 0, 0)))
    return pl.pallas_call(body, grid_spec=grid_spec, out_shape=jax.ShapeDtypeStruct((4, r, cdim), g.dtype), name=name,
                          compiler_params=_params(("parallel",)))(core, g, recv)


def _sum_slots(x, *, name):
    def body(x_ref, o_ref):
        acc = x_ref[0]
        for s in range(1, N_DEV):
            acc = acc + x_ref[s]
        o_ref[...] = acc
    return pl.pallas_call(body, out_shape=jax.ShapeDtypeStruct(x.shape[1:], F32), name=name)(x)


def _adamw(w, m, v, gparts, *, tile, name, sel=None):
    r, cdim = w.shape
    ng = len(gparts)
    sel = jnp.zeros((1,), jnp.int32) if sel is None else sel

    def body(*refs):
        w_ref, m_ref, v_ref = refs[1:4]
        g = None
        for p, part in zip(refs[4:4 + ng], gparts):
            pv = (p[0] if isinstance(part, tuple) else p[...]).astype(F32)
            g = pv if g is None else g + pv
        g_ref, d_ref, nm_ref, nv_ref = refs[4 + ng:]
        mn = ADAM_B1 * m_ref[...] + (1.0 - ADAM_B1) * g
        vn = ADAM_B2 * v_ref[...] + (1.0 - ADAM_B2) * (g * g)
        m_hat = mn / (1.0 - ADAM_B1 ** ADAM_STEP)
        v_hat = vn / (1.0 - ADAM_B2 ** ADAM_STEP)
        g_ref[...] = g
        d_ref[...] = -ADAM_LR * (m_hat / (jnp.sqrt(v_hat) + ADAM_EPS) + ADAM_WD * w_ref[...])
        nm_ref[...] = mn
        nv_ref[...] = vn

    spec = pl.BlockSpec((tile, cdim), lambda i, s: (i, 0))

    def part_spec(part):
        if not isinstance(part, tuple):
            return spec
        slab = part[1]
        if slab is None:
            return pl.BlockSpec((1, tile, cdim), lambda i, s: (s[0], i, 0))
        return pl.BlockSpec((1, tile, cdim), lambda i, s: (slab, i, 0))

    grid_spec = pltpu.PrefetchScalarGridSpec(
        num_scalar_prefetch=1, grid=(r // tile,), in_specs=[spec] * 3 + [part_spec(p) for p in gparts], out_specs=[spec] * 4)
    return pl.pallas_call(
        body, grid_spec=grid_spec, out_shape=[jax.ShapeDtypeStruct(w.shape, F32)] * 4, name=name,
        compiler_params=_params(("parallel",)),
    )(sel, w, m, v, *[p[0] if isinstance(p, tuple) else p for p in gparts])


_REP_ROWS = 272
_REP_SIZE = 2 * 1024 * 2 + 4096 * 2 + 64 + 65536 * 4 + 1024 + 1024


def _pad_rows(v, rows):
    return jnp.pad(v.reshape(-1), (0, rows * 128 - v.size)).reshape(rows, 128)


def kernel(x, c, w_ada, b_ada, norm_mix, norm_mlp, s5_a_re, s5_a_im, s5_log_dt, s5_b_re, s5_b_im, s5_c_re, s5_c_im, s5_d, s5_w_glu, gla_w_in, gla_w_gate2, gla_b_gate, gla_g_norm, gla_w_out, w_ff1, w_ff2, norm_final, loss_target, m_w_ada, m_b_ada, m_norm_mix, m_norm_mlp, m_s5_a_re, m_s5_a_im, m_s5_log_dt, m_s5_b_re, m_s5_b_im, m_s5_c_re, m_s5_c_im, m_s5_d, m_s5_w_glu, m_gla_w_in, m_gla_w_gate2, m_gla_b_gate, m_gla_g_norm, m_gla_w_out, m_w_ff1, m_w_ff2, m_norm_final, v_w_ada, v_b_ada, v_norm_mix, v_norm_mlp, v_s5_a_re, v_s5_a_im, v_s5_log_dt, v_s5_b_re, v_s5_b_im, v_s5_c_re, v_s5_c_im, v_s5_d, v_s5_w_glu, v_gla_w_in, v_gla_w_gate2, v_gla_b_gate, v_gla_g_norm, v_gla_w_out, v_w_ff1, v_w_ff2, v_norm_final):
    ix, iy, ic = _my_place()
    me = 4 * ix + 2 * iy + ic
    ada_w = w_ada.shape[2]

    msg = jnp.concatenate([c.reshape(8, 128), gla_w_gate2[0].reshape(8, 128), _pad_rows(gla_b_gate, 1),
                           gla_g_norm.reshape(1, 128), jnp.zeros((6, 128), F32)])
    got = _exchange(msg, gather=True, name="gather_small")
    c_all = got[:, 0:8].reshape(N_DEV, D_MODEL)
    wg2 = got[:, 8:16].reshape(N_DEV, GLA_RANK, 64).transpose(1, 0, 2).reshape(GLA_RANK, GLA_QK)
    bg = got[:, 16, :64].reshape(1, GLA_QK)
    gn = got[:, 17, :].reshape(1, D_MODEL)

    b_cols = lax.dynamic_slice_in_dim(b_ada, me * ada_w, ada_w, axis=1)
    mod_cols = _ada_fwd(c_all, w_ada, b_cols)
    pay = jnp.pad(mod_cols.transpose(1, 0, 2).reshape(N_DEV, 12, 128), ((0, 0), (0, 4), (0, 0)))
    mod = _exchange(pay, gather=False, name="a2a_mod")[:, :12].reshape(N_DEV, 2, ada_w).transpose(1, 0, 2).reshape(2, 6 * D_MODEL)
    mods = [[mod[i:i + 1, j * D_MODEL:(j + 1) * D_MODEL] for j in range(6)] for i in range(2)]

    big_w = [s5_w_glu[0], gla_w_in[0], gla_w_out[0], w_ff1[0], w_ff1[1], w_ff2[0], w_ff2[1]]
    big_m = [m_s5_w_glu[0], m_gla_w_in[0], m_gla_w_out[0], m_w_ff1[0], m_w_ff1[1], m_w_ff2[0], m_w_ff2[1]]
    big_v = [v_s5_w_glu[0], v_gla_w_in[0], v_gla_w_out[0], v_w_ff1[0], v_w_ff1[1], v_w_ff2[0], v_w_ff2[1]]
    glu_s, gin_s, gout_s, ff1a_s, ff1b_s, ff2a_s, ff2b_s = _all_gather_big([a.astype(MXU_DT) for a in big_w])
    gin_full = gin_s.transpose(1, 0, 2).reshape(D_MODEL, GLA_IN)
    q_, k_, v_, glr_, r_ = jnp.split(gin_full, [GLA_QK, 2 * GLA_QK, 2 * GLA_QK + D_MODEL, 2 * GLA_QK + D_MODEL + GLA_RANK], axis=1)
    w = dict(
        glu=glu_s.transpose(1, 0, 2).reshape(D_MODEL, 2 * D_MODEL),
        gin=jnp.concatenate([q_, k_, v_, r_, glr_, jnp.zeros((D_MODEL, GLA_RANK_PAD - GLA_RANK), MXU_DT)], axis=1),
        gout=gout_s.reshape(D_MODEL, D_MODEL),
        ff1=[s.transpose(1, 0, 2).reshape(D_MODEL, D_FF) for s in (ff1a_s, ff1b_s)],
        ff2=[s.reshape(D_FF, D_MODEL) for s in (ff2a_s, ff2b_s)],
        wg2=jnp.pad(wg2, ((0, GLA_RANK_PAD - GLA_RANK), (0, 0))), bg=bg, gn=gn,
        tri=jnp.tril(jnp.ones((GLA_CHUNK, GLA_CHUNK), F32)),
    )
    nrm = dict(mix=[norm_mix[i:i + 1] for i in range(2)], mlp=[norm_mlp[i:i + 1] for i in range(2)], final=norm_final.reshape(1, D_MODEL))
    e01 = (lax.broadcasted_iota(jnp.int32, (S5_GROUPS, S5_NSTATE), 1) // S5_STATE
           == lax.broadcasted_iota(jnp.int32, (S5_GROUPS, S5_NSTATE), 0)).astype(F32)
    s5p = dict(a_re=s5_a_re.reshape(1, S5_NSTATE), a_im=s5_a_im.reshape(1, S5_NSTATE), log_dt=s5_log_dt,
               bt_re=s5_b_re[0].transpose(2, 0, 1).reshape(S5_GROUP, S5_NSTATE),
               bt_im=s5_b_im[0].transpose(2, 0, 1).reshape(S5_GROUP, S5_NSTATE),
               c_re=s5_c_re[0], c_im=s5_c_im[0], d=s5_d, e01=e01)

    loss_local, grad_x, g = _local_step(x[0], loss_target[0], mods, nrm, s5p, w)
    loss = lax.psum(loss_local, ("x", "y", "c"))

    gin_g = g["gin"]
    gin_g = jnp.concatenate([gin_g[:, :2 * GLA_QK + D_MODEL], gin_g[:, GLA_INP - GLA_RANK_PAD:GLA_INP - GLA_RANK_PAD + GLA_RANK],
                             gin_g[:, 2 * GLA_QK + D_MODEL:2 * GLA_QK + 2 * D_MODEL]], axis=1)
    per_owner = [g["glu"], gin_g.reshape(D_MODEL, N_DEV, GLA_IN // N_DEV).transpose(1, 0, 2),
                 g["gout"].reshape(N_DEV, D_MODEL // N_DEV, D_MODEL), g["ff1a"], g["ff1b"],
                 g["ff2a"].reshape(N_DEV, D_FF // N_DEV, D_MODEL), g["ff2b"].reshape(N_DEV, D_FF // N_DEV, D_MODEL)]
    core = ic.reshape(1).astype(jnp.int32)
    chip = (2 * ix + iy).reshape(1).astype(jnp.int32)
    from_sibling = _swap_sibling(per_owner)
    chip_sum = [_pair_add(a, b, core, name="rs_add%d" % i) for i, (a, b) in enumerate(zip(per_owner, from_sibling))]
    from_chips = _exchange_chips(chip_sum)
    big = []
    for i in range(len(big_w)):
        parts = [(chip_sum[i], None), (from_chips[i], 0), (from_chips[i], 1), (from_chips[i], 2)]
        big.append(_adamw(big_w[i], big_m[i], big_v[i], parts, tile=min(512, big_w[i].shape[0]), name="adamw_big%d" % i, sel=chip))

    rep = [jnp.concatenate([g["mix0"], g["mix1"]]), jnp.concatenate([g["mlp0"], g["mlp1"]]), g["s5_a_re"], g["s5_a_im"], g["s5_log_dt"],
           g["s5_bt_re"].reshape(S5_GROUP, S5_GROUPS, S5_STATE).transpose(1, 2, 0), g["s5_bt_im"].reshape(S5_GROUP, S5_GROUPS, S5_STATE).transpose(1, 2, 0),
           g["s5_c_re"], g["s5_c_im"], g["s5_d"], g["nf"]]
    rep_shapes = [(2, D_MODEL), (2, D_MODEL), (1, 64, 64), (1, 64, 64), (1, 64), (1, 64, 64, 16), (1, 64, 64, 16), (1, 64, 16, 64), (1, 64, 16, 64), (1, D_MODEL), (D_MODEL,)]
    rep_flat = jnp.concatenate([a.reshape(-1) for a in rep])
    rep_blk = jnp.pad(rep_flat, (0, N_DEV * _REP_ROWS * 128 - _REP_SIZE)).reshape(N_DEV, _REP_ROWS, 128)
    dmod = jnp.stack([jnp.concatenate([g["sh1" + t], g["sc1" + t], g["gt1" + t], g["sh2" + t], g["sc2" + t], g["gt2" + t]], axis=1)[0] for t in "ab"])
    msg = jnp.concatenate([
        rep_blk,
        g["wg2"][:GLA_RANK].reshape(GLA_RANK, N_DEV, 64).transpose(1, 0, 2).reshape(N_DEV, 8, 128),
        jnp.pad(g["bg"].reshape(N_DEV, 1, 64), ((0, 0), (0, 0), (0, 64))),
        g["gn"].reshape(N_DEV, 1, 128),
        dmod.reshape(2, N_DEV, ada_w).transpose(1, 0, 2).reshape(N_DEV, 12, 128),
        jnp.zeros((N_DEV, 2, 128), F32),
    ], axis=1)
    got = _exchange(msg, gather=False, name="a2a_small_grads")
    tot = _sum_slots(got, name="sum_small_grads")
    dm = got[:, 282:294].reshape(N_DEV, 2, ada_w).transpose(1, 0, 2)
    g_w_ada = _ada_bwd(c_all, dm)
    back = _exchange(jnp.concatenate([tot[0:_REP_ROWS], tot[282:294], jnp.zeros((4, 128), F32)]), gather=True, name="gather_small_grads")
    rep_sum = back[:, :_REP_ROWS].reshape(-1)[:_REP_SIZE]
    g_b_ada = back[:, _REP_ROWS:_REP_ROWS + 12].reshape(N_DEV, 2, ada_w).transpose(1, 0, 2).reshape(2, 6 * D_MODEL)
    g_rep, off = [], 0
    for s in rep_shapes:
        n = math.prod(s)
        g_rep.append(rep_sum[off:off + n].reshape(s))
        off += n
    g_small = g_rep + [g_b_ada, tot[272:280].reshape(GLA_RANK, 64)[None], tot[280, :64][None], tot[281][None]]
    p_small = [norm_mix, norm_mlp, s5_a_re, s5_a_im, s5_log_dt, s5_b_re, s5_b_im, s5_c_re, s5_c_im, s5_d, norm_final, b_ada, gla_w_gate2, gla_b_gate, gla_g_norm]
    m_small = [m_norm_mix, m_norm_mlp, m_s5_a_re, m_s5_a_im, m_s5_log_dt, m_s5_b_re, m_s5_b_im, m_s5_c_re, m_s5_c_im, m_s5_d, m_norm_final, m_b_ada, m_gla_w_gate2, m_gla_b_gate, m_gla_g_norm]
    v_small = [v_norm_mix, v_norm_mlp, v_s5_a_re, v_s5_a_im, v_s5_log_dt, v_s5_b_re, v_s5_b_im, v_s5_c_re, v_s5_c_im, v_s5_d, v_norm_final, v_b_ada, v_gla_w_gate2, v_gla_b_gate, v_gla_g_norm]
    n_small = sum(a.size for a in p_small)
    rows_small = -(-n_small // 128)
    cat = lambda lst: _pad_rows(jnp.concatenate([a.reshape(-1) for a in lst]), rows_small)
    small = _adamw(cat(p_small), cat(m_small), cat(v_small), [cat(g_small)], tile=rows_small, name="adamw_small")

    def split_small(a):
        flat, out, off = a.reshape(-1), [], 0
        for p in p_small:
            out.append(flat[off:off + p.size].reshape(p.shape))
            off += p.size
        return out
    small = [split_small(a) for a in small]
    ada = _adamw(w_ada.reshape(2 * D_MODEL, ada_w), m_w_ada.reshape(2 * D_MODEL, ada_w), v_w_ada.reshape(2 * D_MODEL, ada_w),
                 [g_w_ada.reshape(2 * D_MODEL, ada_w)], tile=512, name="adamw_ada")
    ada = [a.reshape(w_ada.shape) for a in ada]

    def leaves(k):
        nm, nl, a_re, a_im, ldt, b_re, b_im, c_re, c_im, dsk, nf, bada, wg2_, bg_, gn_ = small[k]
        glu_, gin_, gout_, ff1a_, ff1b_, ff2a_, ff2b_ = [b[k] for b in big]
        return [ada[k], bada, nm, nl, a_re, a_im, ldt, b_re, b_im, c_re, c_im, dsk, glu_[None], gin_[None], wg2_, bg_, gn_, gout_[None],
                jnp.stack([ff1a_, ff1b_]), jnp.stack([ff2a_, ff2b_]), nf]

    return (loss, grad_x[None], *leaves(0), *leaves(1), *leaves(2), *leaves(3))
```

```python
import functools
import math

import jax
import jax.numpy as jnp
from jax import lax
from jax.experimental import pallas as pl
from jax.experimental.pallas import tpu as pltpu

F32 = jnp.float32
BF16 = jnp.bfloat16
MXU_DT = BF16
EPS = 1e-6
N_DEV = 8
VMEM_LIMIT = 56 * 1024 * 1024

D_MODEL = 1024
S5_GROUP = 16
S5_GROUPS = 64
S5_STATE = 64
S5_NSTATE = S5_GROUPS * S5_STATE
S5_GB = 16
S5_NB = S5_GROUPS // S5_GB
S5_BC = S5_GB * S5_GROUP
S5_BS = S5_GB * S5_STATE
GLA_HEADS = 4
GLA_QK = 512
GLA_DK = 128
GLA_DV = 256
GLA_RANK = 16
GLA_RANK_PAD = 128
GLA_TAU = 16.0
GLA_CHUNK = 64
GLA_IN = 3088
GLA_INP = 2 * GLA_QK + 2 * D_MODEL + GLA_RANK_PAD
D_FF = 4096

ADAM_LR = 0.001
ADAM_B1 = 0.9
ADAM_B2 = 0.999
ADAM_EPS = 1e-08
ADAM_WD = 0.01
ADAM_STEP = 10

_NN = (((1,), (0,)), ((), ()))
_NT = (((1,), (1,)), ((), ()))
_TN = (((0,), (0,)), ((), ()))


def _dot(a, b, dn=_NN):
    return lax.dot_general(a.astype(MXU_DT), b.astype(MXU_DT), dn, preferred_element_type=F32)


def _dot_exact01(x, m01, dn=_NN):
    x1 = x.astype(BF16)
    r1 = x - x1.astype(F32)
    x2 = r1.astype(BF16)
    x3 = (r1 - x2.astype(F32)).astype(BF16)
    m = m01.astype(BF16)
    d = lambda u: lax.dot_general(u, m, dn, preferred_element_type=F32)
    return d(x1) + d(x2) + d(x3)


def _dot_f32(a, b, dn=_NN):
    def split(x):
        x1 = x.astype(BF16)
        r1 = x - x1.astype(F32)
        x2 = r1.astype(BF16)
        x3 = (r1 - x2.astype(F32)).astype(BF16)
        return x1, x2, x3
    a1, a2, a3 = split(a)
    b1, b2, b3 = split(b)
    d = lambda u, v: lax.dot_general(u, v, dn, preferred_element_type=F32)
    return (d(a1, b1) + (d(a1, b2) + d(a2, b1)) + (d(a2, b2) + d(a1, b3) + d(a3, b1)))


def _sigmoid(x):
    return 1.0 / (1.0 + jnp.exp(-x))


def _silu(x):
    return x * _sigmoid(x)


def _gelu(x):
    return 0.5 * x * (1.0 + jnp.tanh(math.sqrt(2.0 / math.pi) * (x + 0.044715 * (x * x * x))))


def _logsig(x):
    return jnp.minimum(x, 0.0) - jnp.log(1.0 + jnp.exp(-jnp.abs(x)))


def _rms(x):
    return lax.rsqrt(jnp.mean(x * x, axis=-1, keepdims=True) + EPS)


def _params(sem):
    return pltpu.CompilerParams(dimension_semantics=sem, vmem_limit_bytes=VMEM_LIMIT)


def _mm(a, b, dims, out_dtype, *, tm, tn, name, a_fn=None, out_fn=None, extra=None, by_owner=False):
    if dims == "tn":
        k, m = a.shape
        n = b.shape[1]
    else:
        m, k = a.shape
        n = b.shape[0] if dims == "nt" else b.shape[1]
    tm, tn = min(tm, m), min(tn, n)
    assert m % tm == 0 and n % tn == 0, (name, m, n, tm, tn)
    dn = {"nn": _NN, "nt": _NT, "tn": _TN}[dims]

    def body(*refs):
        a_ref, b_ref = refs[0], refs[1]
        o_ref = refs[-1]
        av = a_ref[...]
        if a_fn is not None:
            av = a_fn(av.astype(F32))
        acc = _dot(av, b_ref[...], dn)
        if extra is not None:
            acc = out_fn(acc, refs[2][...].astype(F32))
        elif out_fn is not None:
            acc = out_fn(acc)
        if by_owner:
            o_ref[0] = acc.astype(o_ref.dtype)
        else:
            o_ref[...] = acc.astype(o_ref.dtype)

    a_spec = pl.BlockSpec((k, tm), lambda i, j: (0, i)) if dims == "tn" else pl.BlockSpec((tm, k), lambda i, j: (i, 0))
    b_spec = pl.BlockSpec((tn, k), lambda i, j: (j, 0)) if dims == "nt" else pl.BlockSpec((k, tn), lambda i, j: (0, j))
    if by_owner:
        o_spec = pl.BlockSpec((1, tm, tn), lambda i, j: (j, i, 0))
        out_shape = jax.ShapeDtypeStruct((n // tn, m, tn), out_dtype)
    else:
        o_spec = pl.BlockSpec((tm, tn), lambda i, j: (i, j))
        out_shape = jax.ShapeDtypeStruct((m, n), out_dtype)
    in_specs, args = [a_spec, b_spec], [a, b]
    if extra is not None:
        in_specs.append(o_spec)
        args.append(extra)
    return pl.pallas_call(
        body, grid=(m // tm, n // tn), in_specs=in_specs, out_specs=o_spec, out_shape=out_shape, name=name,
        compiler_params=_params(("parallel", "parallel")),
    )(*args)


def _rowcall(f, rows, pars, outs, accs, *, tile, name):
    length = rows[0][0].shape[0]
    tile = min(tile, length)
    nr, npar, no = len(rows), len(pars), len(outs)

    def body(*refs):
        vals = [r[...].astype(F32) for r in refs[:nr + npar]]
        res = f(*vals)
        o_refs = refs[nr + npar:nr + npar + no]
        a_refs = refs[nr + npar + no:]
        for o, v in zip(o_refs, res[:no]):
            o[...] = v.astype(o.dtype)
        if a_refs:
            @pl.when(pl.program_id(0) == 0)
            def _():
                for a in a_refs:
                    a[...] = jnp.zeros(a.shape, F32)
            for a, v in zip(a_refs, res[no:]):
                a[...] += jnp.broadcast_to(v, a.shape)

    in_specs = [pl.BlockSpec((tile, w), lambda i, cb=cb: (i, cb)) for (_, w, cb) in rows]
    in_specs += [pl.BlockSpec(p.shape, lambda i: (0, 0)) for p in pars]
    out_specs = [pl.BlockSpec((tile, w), lambda i: (i, 0)) for (w, _) in outs]
    out_specs += [pl.BlockSpec(s, lambda i: (0, 0)) for s in accs]
    out_shape = [jax.ShapeDtypeStruct((length, w), dt) for (w, dt) in outs]
    out_shape += [jax.ShapeDtypeStruct(s, F32) for s in accs]
    return pl.pallas_call(
        body, grid=(length // tile,), in_specs=in_specs, out_specs=out_specs, out_shape=out_shape, name=name,
        compiler_params=_params(("arbitrary",)),
    )(*[r[0] for r in rows], *pars)


def _vjp_of(f, n_row, n_cot, row_want):
    def g(*a):
        prow, cots, par = a[:n_row], a[n_row:n_row + n_cot], a[n_row + n_cot:]
        _, vjp = jax.vjp(f, *prow, *par)
        grads = vjp(tuple(cots))
        return tuple(grads[i] for i in row_want) + tuple(grads[n_row:])
    return g


def _f_pn(x, g, sc, sh):
    return (x, x * _rms(x) * g * (1.0 + sc) + sh)


def _f_res_pn(x, y, gt, g, sc, sh):
    xn = x + gt * y
    return (xn, xn * _rms(xn) * g * (1.0 + sc) + sh)


def _f_glu_res_pn(x, val, gate, gt, g, sc, sh):
    xn = x + gt * (val * _sigmoid(gate))
    return (xn, xn * _rms(xn) * g * (1.0 + sc) + sh)


def _f_final(x, y, tgt, gt, g):
    xn = x + gt * y
    err = xn * _rms(xn) * g - tgt
    return 0.5 * jnp.mean(err * err, axis=-1, keepdims=True)


def _g_final(x, y, tgt, gt, g):
    lrow, vjp = jax.vjp(_f_final, x, y, tgt, gt, g)
    dx, dy, _, dgt, dg = vjp(jnp.ones_like(lrow))
    return dx, dy, dgt, dg, jnp.sum(lrow)


def _full(a):
    return (a, a.shape[1], 0)


def _s5_prep_f(a_re, a_im, log_dt, bt_re, bt_im, e01):
    dt = jnp.exp(_dot_exact01(log_dt, e01))
    mag = jnp.exp(a_re * dt)
    ph = a_im * dt
    lb_re = mag * jnp.cos(ph)
    lb_im = mag * jnp.sin(ph)
    den = a_re * a_re + a_im * a_im
    nr = lb_re - 1.0
    ni = lb_im
    f_re = (nr * a_re + ni * a_im) / den
    f_im = (ni * a_re - nr * a_im) / den
    bb_re = f_re * bt_re - f_im * bt_im
    bb_im = f_re * bt_im + f_im * bt_re
    return lb_re, lb_im, bb_re, bb_im


def _s5_prep_outs():
    return [jax.ShapeDtypeStruct((1, S5_NSTATE), F32)] * 2 + [jax.ShapeDtypeStruct((S5_GROUP, S5_NSTATE), F32)] * 2


def _s5_prep(a_re, a_im, log_dt, bt_re, bt_im, e01):
    def body(*refs):
        res = _s5_prep_f(*[r[...] for r in refs[:6]])
        for o, v in zip(refs[6:], res):
            o[...] = v
    return pl.pallas_call(body, out_shape=_s5_prep_outs(), name="s5_prep",
                          compiler_params=pltpu.CompilerParams(vmem_limit_bytes=VMEM_LIMIT))(a_re, a_im, log_dt, bt_re, bt_im, e01)


def _s5_prep_bwd(a_re, a_im, log_dt, bt_re, bt_im, e01, d_lb_re, d_lb_im, d_bb_re, d_bb_im):
    def f(a_re, a_im, log_dt, bt_re, bt_im, e01):
        @jax.custom_vjp
        def expand(v):
            return _dot_exact01(v, e01)
        expand.defvjp(lambda v: (_dot_exact01(v, e01), None), lambda _, ct: (_dot_exact01(ct, e01, _NT),))
        dt = jnp.exp(expand(log_dt))
        mag = jnp.exp(a_re * dt)
        ph = a_im * dt
        lb_re = mag * jnp.cos(ph)
        lb_im = mag * jnp.sin(ph)
        den = a_re * a_re + a_im * a_im
        nr = lb_re - 1.0
        f_re = (nr * a_re + lb_im * a_im) / den
        f_im = (lb_im * a_re - nr * a_im) / den
        return lb_re, lb_im, f_re * bt_re - f_im * bt_im, f_re * bt_im + f_im * bt_re

    def body(*refs):
        ins = [r[...] for r in refs[:5]]
        e = refs[5][...]
        cots = tuple(r[...] for r in refs[6:10])
        _, vjp = jax.vjp(lambda *p: f(*p, e), *ins)
        for o, v in zip(refs[10:], vjp(cots)):
            o[...] = v
    outs = [jax.ShapeDtypeStruct(v.shape, F32) for v in (a_re, a_im, log_dt, bt_re, bt_im)]
    return pl.pallas_call(body, out_shape=outs, name="s5_prep_bwd",
                          compiler_params=pltpu.CompilerParams(vmem_limit_bytes=VMEM_LIMIT))(
        a_re, a_im, log_dt, bt_re, bt_im, e01, d_lb_re, d_lb_im, d_bb_re, d_bb_im)


def _s5_scan(x_re, x_im, a_r, a_i, c_r, c_i, n_tiles, reverse):
    sgn = -1.0 if reverse else 1.0

    def tile(k, carry):
        cr, ci = carry
        i = (n_tiles - 1 - k) if reverse else k
        order = range(7, -1, -1) if reverse else range(8)
        for j in order:
            br = x_re[i, pl.ds(j, 1), :]
            bi = x_im[i, pl.ds(j, 1), :]
            nr = a_r * cr - (sgn * a_i) * ci + br
            ni = a_r * ci + (sgn * a_i) * cr + bi
            x_re[i, pl.ds(j, 1), :] = nr
            x_im[i, pl.ds(j, 1), :] = ni
            cr, ci = nr, ni
        return cr, ci

    return lax.fori_loop(0, n_tiles, tile, (c_r, c_i))


def _s5_fwd(u, lam_re, lam_im, bbd_re, bbd_im, cbd_re, cbd_im, d_skip, comm, *, tc):
    length = u.shape[0]
    tc = min(tc, length)
    nt = length // tc
    nci, nco = len(comm.ins), len(comm.out_shapes)

    def body(*refs):
        u_ref, lr_ref, li_ref, br_ref, bi_ref, cr_ref, ci_ref, d_ref = refs[:8]
        z_ref, sr_ref, si_ref = refs[8 + nci:11 + nci]
        xr, xi, car_r, car_i = refs[11 + nci + nco:15 + nci + nco]
        comm.run(pl.program_id(0) * nt + pl.program_id(1), S5_NB * nt, refs[8:8 + nci], refs[11 + nci:11 + nci + nco],
                 refs[15 + nci + nco:])

        @pl.when(pl.program_id(1) == 0)
        def _():
            car_r[...] = jnp.zeros_like(car_r)
            car_i[...] = jnp.zeros_like(car_i)
        sr_ref[0] = car_r[...]
        si_ref[0] = car_i[...]
        uv = u_ref[...]
        xr[...] = _dot(uv, br_ref[0]).reshape(tc // 8, 8, S5_BS)
        xi[...] = _dot(uv, bi_ref[0]).reshape(tc // 8, 8, S5_BS)
        cr, ci = _s5_scan(xr, xi, lr_ref[...], li_ref[...], car_r[...], car_i[...], tc // 8, False)
        car_r[...] = cr
        car_i[...] = ci
        y = (_dot(xr[...].reshape(tc, S5_BS), cr_ref[0]) - _dot(xi[...].reshape(tc, S5_BS), ci_ref[0]) + d_ref[...] * uv)
        z_ref[...] = _gelu(y).astype(z_ref.dtype)

    blk_u = pl.BlockSpec((tc, S5_BC), lambda g, t: (t, g))
    blk_l = pl.BlockSpec((1, S5_BS), lambda g, t: (0, g))
    blk_b = pl.BlockSpec((1, S5_BC, S5_BS), lambda g, t: (g, 0, 0))
    blk_c = pl.BlockSpec((1, S5_BS, S5_BC), lambda g, t: (g, 0, 0))
    blk_d = pl.BlockSpec((1, S5_BC), lambda g, t: (0, g))
    blk_s = pl.BlockSpec((1, 1, S5_BS), lambda g, t: (t, 0, g))
    return pl.pallas_call(
        body, grid=(S5_NB, nt),
        in_specs=[blk_u, blk_l, blk_l, blk_b, blk_b, blk_c, blk_c, blk_d] + [_ANY_SPEC] * nci,
        out_specs=[blk_u, blk_s, blk_s] + [_ANY_SPEC] * nco,
        out_shape=[jax.ShapeDtypeStruct((length, D_MODEL), MXU_DT),
                   jax.ShapeDtypeStruct((nt, 1, S5_NSTATE), F32), jax.ShapeDtypeStruct((nt, 1, S5_NSTATE), F32)] + comm.out_shapes,
        scratch_shapes=[pltpu.VMEM((tc // 8, 8, S5_BS), F32), pltpu.VMEM((tc // 8, 8, S5_BS), F32),
                        pltpu.VMEM((1, S5_BS), F32), pltpu.VMEM((1, S5_BS), F32)] + comm.scratch,
        name="s5_fwd", compiler_params=_params(("arbitrary", "arbitrary")),
    )(u, lam_re, lam_im, bbd_re, bbd_im, cbd_re, cbd_im, d_skip, *comm.ins)


def _s5_bwd(u, dz, st_re, st_im, lam_re, lam_im, bbd_re, bbd_im, cbd_re, cbd_im, d_skip, comm, *, tc):
    length = u.shape[0]
    tc = min(tc, length)
    nt = length // tc
    nci, nco = len(comm.ins), len(comm.out_shapes)

    def body(*refs):
        u_ref, dz_ref, sr_ref, si_ref, lr_ref, li_ref, br_ref, bi_ref, cr_ref, ci_ref, d_ref = refs[:11]
        du_ref, dbr_ref, dbi_ref, dcr_ref, dci_ref, dlr_ref, dli_ref, dd_ref = refs[11 + nci:19 + nci]
        xr, xi, gr, gi, car_r, car_i = refs[19 + nci + nco:25 + nci + nco]
        comm.run(pl.program_id(0) * nt + pl.program_id(1), S5_NB * nt, refs[11:11 + nci], refs[19 + nci:19 + nci + nco],
                 refs[25 + nci + nco:])

        @pl.when(pl.program_id(1) == 0)
        def _():
            car_r[...] = jnp.zeros_like(car_r)
            car_i[...] = jnp.zeros_like(car_i)
            for r in (dbr_ref, dbi_ref, dcr_ref, dci_ref, dlr_ref, dli_ref, dd_ref):
                r[...] = jnp.zeros(r.shape, F32)
        a_r, a_i = lr_ref[...], li_ref[...]
        uv = u_ref[...]
        xr[...] = _dot(uv, br_ref[0]).reshape(tc // 8, 8, S5_BS)
        xi[...] = _dot(uv, bi_ref[0]).reshape(tc // 8, 8, S5_BS)
        _s5_scan(xr, xi, a_r, a_i, sr_ref[0], si_ref[0], tc // 8, False)
        xrv = xr[...].reshape(tc, S5_BS)
        xiv = xi[...].reshape(tc, S5_BS)
        y = _dot(xrv, cr_ref[0]) - _dot(xiv, ci_ref[0]) + d_ref[...] * uv
        _, gelu_vjp = jax.vjp(_gelu, y)
        dy = gelu_vjp(dz_ref[...].astype(F32))[0]
        dd_ref[...] += jnp.sum(dy * uv, axis=0, keepdims=True)
        dcr_ref[0] += _dot(xrv, dy, _TN)
        dci_ref[0] -= _dot(xiv, dy, _TN)
        gr[...] = _dot(dy, cr_ref[0], _NT).reshape(tc // 8, 8, S5_BS)
        gi[...] = (-_dot(dy, ci_ref[0], _NT)).reshape(tc // 8, 8, S5_BS)
        cr, ci = _s5_scan(gr, gi, a_r, a_i, car_r[...], car_i[...], tc // 8, True)
        car_r[...] = cr
        car_i[...] = ci
        grv = gr[...].reshape(tc, S5_BS)
        giv = gi[...].reshape(tc, S5_BS)
        first = lax.broadcasted_iota(jnp.int32, (tc, 1), 0) == 0
        xpr = jnp.where(first, sr_ref[0], pltpu.roll(xrv, 1, 0))
        xpi = jnp.where(first, si_ref[0], pltpu.roll(xiv, 1, 0))
        dlr_ref[...] += jnp.sum(grv * xpr + giv * xpi, axis=0, keepdims=True)
        dli_ref[...] += jnp.sum(giv * xpr - grv * xpi, axis=0, keepdims=True)
        dbr_ref[0] += _dot(uv, grv, _TN)
        dbi_ref[0] += _dot(uv, giv, _TN)
        du_ref[...] = _dot(grv, br_ref[0], _NT) + _dot(giv, bi_ref[0], _NT) + d_ref[...] * dy

    rev = lambda t: nt - 1 - t
    blk_u = pl.BlockSpec((tc, S5_BC), lambda g, t: (rev(t), g))
    blk_l = pl.BlockSpec((1, S5_BS), lambda g, t: (0, g))
    blk_b = pl.BlockSpec((1, S5_BC, S5_BS), lambda g, t: (g, 0, 0))
    blk_c = pl.BlockSpec((1, S5_BS, S5_BC), lambda g, t: (g, 0, 0))
    blk_d = pl.BlockSpec((1, S5_BC), lambda g, t: (0, g))
    blk_s = pl.BlockSpec((1, 1, S5_BS), lambda g, t: (rev(t), 0, g))
    return pl.pallas_call(
        body, grid=(S5_NB, nt),
        in_specs=[blk_u, blk_u, blk_s, blk_s, blk_l, blk_l, blk_b, blk_b, blk_c, blk_c, blk_d] + [_ANY_SPEC] * nci,
        out_specs=[blk_u, blk_b, blk_b, blk_c, blk_c, blk_l, blk_l, blk_d] + [_ANY_SPEC] * nco,
        out_shape=[jax.ShapeDtypeStruct((length, D_MODEL), F32),
                   jax.ShapeDtypeStruct((S5_NB, S5_BC, S5_BS), F32), jax.ShapeDtypeStruct((S5_NB, S5_BC, S5_BS), F32),
                   jax.ShapeDtypeStruct((S5_NB, S5_BS, S5_BC), F32), jax.ShapeDtypeStruct((S5_NB, S5_BS, S5_BC), F32),
                   jax.ShapeDtypeStruct((1, S5_NSTATE), F32), jax.ShapeDtypeStruct((1, S5_NSTATE), F32),
                   jax.ShapeDtypeStruct((1, D_MODEL), F32)] + comm.out_shapes,
        scratch_shapes=[pltpu.VMEM((tc // 8, 8, S5_BS), F32), pltpu.VMEM((tc // 8, 8, S5_BS), F32),
                        pltpu.VMEM((tc // 8, 8, S5_BS), F32), pltpu.VMEM((tc // 8, 8, S5_BS), F32),
                        pltpu.VMEM((1, S5_BS), F32), pltpu.VMEM((1, S5_BS), F32)] + comm.scratch,
        name="s5_bwd", compiler_params=_params(("arbitrary", "arbitrary")),
    )(u, dz, st_re, st_im, lam_re, lam_im, bbd_re, bbd_im, cbd_re, cbd_im, d_skip, *comm.ins)


def _blockdiag_b(bt):
    eye = jnp.eye(S5_GB, dtype=bt.dtype)
    t = bt.reshape(S5_GROUP, S5_NB, S5_GB, S5_STATE)
    return jnp.einsum("ab,hnbp->nahbp", eye, t).reshape(S5_NB, S5_BC, S5_BS)


def _unblockdiag_b(m):
    eye = jnp.eye(S5_GB, dtype=m.dtype)
    t = m.reshape(S5_NB, S5_GB, S5_GROUP, S5_GB, S5_STATE)
    return jnp.einsum("ab,nahbp->hnbp", eye, t).reshape(S5_GROUP, S5_NSTATE)


def _blockdiag_c(c):
    eye = jnp.eye(S5_GB, dtype=c.dtype)
    t = c.reshape(S5_NB, S5_GB, S5_GROUP, S5_STATE)
    return jnp.einsum("ab,nbhp->napbh", eye, t).reshape(S5_NB, S5_BS, S5_BC)


def _unblockdiag_c(m):
    eye = jnp.eye(S5_GB, dtype=m.dtype)
    t = m.reshape(S5_NB, S5_GB, S5_STATE, S5_GB, S5_GROUP)
    return jnp.einsum("ab,napbh->nbhp", eye, t).reshape(S5_GROUPS, S5_GROUP, S5_STATE)


def _gla_gates(glr, wg2, bg, tri):
    pre = _dot(glr, wg2) + bg
    la = _logsig(pre) * (1.0 / GLA_TAU)
    gc = _dot_f32(tri, la)
    gend = gc[GLA_CHUNK - 1:GLA_CHUNK, :]
    e = jnp.exp(gend - gc)
    return pre, e, jnp.exp(gend)


def _gla_specs(nc, rev):
    ix = (lambda n: nc - 1 - n) if rev else (lambda n: n)
    c = GLA_CHUNK
    return dict(
        q=pl.BlockSpec((c, GLA_QK), lambda n: (ix(n), 0)),
        k=pl.BlockSpec((c, GLA_QK), lambda n: (ix(n), 1)),
        v=pl.BlockSpec((c, D_MODEL), lambda n: (ix(n), 1)),
        r=pl.BlockSpec((c, D_MODEL), lambda n: (ix(n), 2)),
        glr=pl.BlockSpec((c, GLA_RANK_PAD), lambda n: (ix(n), (2 * GLA_QK + 2 * D_MODEL) // GLA_RANK_PAD)),
        wg2=pl.BlockSpec((GLA_RANK_PAD, GLA_QK), lambda n: (0, 0)),
        bg=pl.BlockSpec((1, GLA_QK), lambda n: (0, 0)),
        gn=pl.BlockSpec((1, D_MODEL), lambda n: (0, 0)),
        tri=pl.BlockSpec((c, c), lambda n: (0, 0)),
        row=pl.BlockSpec((c, D_MODEL), lambda n: (ix(n), 0)),
        rowp=pl.BlockSpec((c, GLA_INP), lambda n: (ix(n), 0)),
        st=pl.BlockSpec((1, GLA_HEADS, GLA_DV, GLA_DK), lambda n: (ix(n), 0, 0, 0)),
    )


def _gla_fwd(proj, wg2, bg, gn, tri, comm):
    length = proj.shape[0]
    nc = length // GLA_CHUNK
    scale = GLA_DK ** -0.5
    nci, nco = len(comm.ins), len(comm.out_shapes)

    def body(*refs):
        q_ref, k_ref, v_ref, r_ref, glr_ref, wg2_ref, bg_ref, gn_ref, tri_ref = refs[:9]
        og_ref, sp_ref = refs[9 + nci:11 + nci]
        st = refs[11 + nci + nco]
        comm.run(pl.program_id(0), nc, refs[9:9 + nci], refs[11 + nci:11 + nci + nco], refs[12 + nci + nco:])

        @pl.when(pl.program_id(0) == 0)
        def _():
            st[...] = jnp.zeros_like(st)
        _, e, dec = _gla_gates(glr_ref[...], wg2_ref[...], bg_ref[...], tri_ref[...])
        kd = k_ref[...].astype(F32) * e
        q = q_ref[...].astype(F32) * scale
        for h in range(GLA_HEADS):
            sk = slice(h * GLA_DK, (h + 1) * GLA_DK)
            sv = slice(h * GLA_DV, (h + 1) * GLA_DV)
            sp_ref[0, h] = st[h]
            stn = dec[:, sk] * st[h] + _dot(v_ref[:, sv], kd[:, sk], _TN)
            st[h] = stn
            o = _dot(q[:, sk], stn, _NT)
            on = o * _rms(o)
            og_ref[:, sv] = (on * gn_ref[:, sv] * _silu(r_ref[:, sv].astype(F32))).astype(og_ref.dtype)

    s = _gla_specs(nc, False)
    return pl.pallas_call(
        body, grid=(nc,),
        in_specs=[s["q"], s["k"], s["v"], s["r"], s["glr"], s["wg2"], s["bg"], s["gn"], s["tri"]] + [_ANY_SPEC] * nci,
        out_specs=[s["row"], s["st"]] + [_ANY_SPEC] * nco,
        out_shape=[jax.ShapeDtypeStruct((length, D_MODEL), MXU_DT),
                   jax.ShapeDtypeStruct((nc, GLA_HEADS, GLA_DV, GLA_DK), F32)] + comm.out_shapes,
        scratch_shapes=[pltpu.VMEM((GLA_HEADS, GLA_DV, GLA_DK), F32)] + comm.scratch,
        name="gla_fwd", compiler_params=_params(("arbitrary",)),
    )(proj, proj, proj, proj, proj, wg2, bg, gn, tri, *comm.ins)


def _gla_bwd(proj, d_og, s_prev, wg2, bg, gn, tri):
    length = proj.shape[0]
    nc = length // GLA_CHUNK
    scale = GLA_DK ** -0.5

    def body(q_ref, k_ref, v_ref, r_ref, glr_ref, dog_ref, sp_ref, wg2_ref, bg_ref, gn_ref, tri_ref,
             dp_ref, dwg2_ref, dbg_ref, dgn_ref, dst):
        dq_ref = dp_ref.at[:, pl.ds(0, GLA_QK)]
        dk_ref = dp_ref.at[:, pl.ds(GLA_QK, GLA_QK)]
        dv_ref = dp_ref.at[:, pl.ds(2 * GLA_QK, D_MODEL)]
        dr_ref = dp_ref.at[:, pl.ds(2 * GLA_QK + D_MODEL, D_MODEL)]
        dglr_ref = dp_ref.at[:, pl.ds(2 * GLA_QK + 2 * D_MODEL, GLA_RANK_PAD)]

        @pl.when(pl.program_id(0) == 0)
        def _():
            dst[...] = jnp.zeros_like(dst)
            dwg2_ref[...] = jnp.zeros_like(dwg2_ref)
            dbg_ref[...] = jnp.zeros_like(dbg_ref)
            dgn_ref[...] = jnp.zeros_like(dgn_ref)
        glr = glr_ref[...]
        pre, e, dec = _gla_gates(glr, wg2_ref[...], bg_ref[...], tri_ref[...])
        k = k_ref[...].astype(F32)
        kd = k * e
        q = q_ref[...].astype(F32) * scale
        dkd_parts, ddec_parts = [], []
        for h in range(GLA_HEADS):
            sk = slice(h * GLA_DK, (h + 1) * GLA_DK)
            sv = slice(h * GLA_DV, (h + 1) * GLA_DV)
            stp = sp_ref[0, h]
            vh = v_ref[:, sv]
            stn = dec[:, sk] * stp + _dot(vh, kd[:, sk], _TN)
            o = _dot(q[:, sk], stn, _NT)
            rinv = _rms(o)
            on = o * rinv
            rv = r_ref[:, sv].astype(F32)
            sg = _sigmoid(rv)
            sr = rv * sg
            dog = dog_ref[:, sv].astype(F32)
            gnh = gn_ref[:, sv]
            d_ong = dog * sr
            dr_ref[:, sv] = (dog * (on * gnh) * (sg * (1.0 + rv * (1.0 - sg)))).astype(dr_ref.dtype)
            dgn_ref[:, sv] += jnp.sum(d_ong * on, axis=0, keepdims=True)
            d_on = d_ong * gnh
            do = rinv * (d_on - on * jnp.mean(d_on * on, axis=-1, keepdims=True))
            dq_ref[:, sk] = (_dot(do, stn) * scale).astype(dq_ref.dtype)
            dstn = dst[h] + _dot(do, q[:, sk], _TN)
            dst[h] = dec[:, sk] * dstn
            ddec_parts.append(jnp.sum(dstn * stp, axis=0, keepdims=True))
            dv_ref[:, sv] = _dot(kd[:, sk], dstn, _NT).astype(dv_ref.dtype)
            dkd_parts.append(_dot(vh, dstn))
        dkd = jnp.concatenate(dkd_parts, axis=1)
        ddec = jnp.concatenate(ddec_parts, axis=1)
        dk_ref[...] = (dkd * e).astype(dk_ref.dtype)
        w = dkd * kd
        dgend = jnp.sum(w, axis=0, keepdims=True) + ddec * dec
        dla = dgend - _dot_f32(tri_ref[...], w, _TN)
        dpre = dla * (1.0 - _sigmoid(pre)) * (1.0 / GLA_TAU)
        dwg2_ref[...] += _dot(glr, dpre, _TN)
        dbg_ref[...] += jnp.sum(dpre, axis=0, keepdims=True)
        dglr_ref[...] = _dot(dpre, wg2_ref[...], _NT).astype(dglr_ref.dtype)

    s = _gla_specs(nc, True)
    return pl.pallas_call(
        body, grid=(nc,),
        in_specs=[s["q"], s["k"], s["v"], s["r"], s["glr"], s["row"], s["st"], s["wg2"], s["bg"], s["gn"], s["tri"]],
        out_specs=[s["rowp"], s["wg2"], s["bg"], s["gn"]],
        out_shape=[jax.ShapeDtypeStruct((length, GLA_INP), MXU_DT),
                   jax.ShapeDtypeStruct((GLA_RANK_PAD, GLA_QK), F32), jax.ShapeDtypeStruct((1, GLA_QK), F32),
                   jax.ShapeDtypeStruct((1, D_MODEL), F32)],
        scratch_shapes=[pltpu.VMEM((GLA_HEADS, GLA_DV, GLA_DK), F32)],
        name="gla_bwd", compiler_params=_params(("arbitrary",)),
    )(proj, proj, proj, proj, proj, d_og, s_prev, wg2, bg, gn, tri)


def _local_step(x, tgt, mods, nrm, s5p, w, shards, core, *, row_tile=256, s5_tc=256):
    length = x.shape[0]
    tmm = 512
    glu_sh, gin_sh, gout_sh, ff1a_sh, ff1b_sh, ff2a_sh, ff2b_sh = shards
    w = dict(w)
    rc = functools.partial(_rowcall, tile=row_tile)
    (sh1a, sc1a, gt1a, sh2a, sc2a, gt2a), (sh1b, sc1b, gt1b, sh2b, sc2b, gt2b) = mods
    vec = (1, D_MODEL)
    row32, row16 = (D_MODEL, F32), (D_MODEL, MXU_DT)

    (h0,) = rc(lambda *a: _f_pn(*a)[1:], [_full(x)], [nrm["mix"][0], sc1a, sh1a], [row32], [], name="pn0")
    lam_re, lam_im, bb_re, bb_im = _s5_prep(s5p["a_re"], s5p["a_im"], s5p["log_dt"], s5p["bt_re"], s5p["bt_im"], s5p["e01"])
    bbd_re = _blockdiag_b(bb_re.reshape(S5_GROUP, S5_GROUPS, S5_STATE)).astype(MXU_DT)
    bbd_im = _blockdiag_b(bb_im.reshape(S5_GROUP, S5_GROUPS, S5_STATE)).astype(MXU_DT)
    cbd_re = _blockdiag_c(s5p["c_re"]).astype(MXU_DT)
    cbd_im = _blockdiag_c(s5p["c_im"]).astype(MXU_DT)
    z0, st_re, st_im, glu_s, gin_s, ff1a_s, ff2a_s = _s5_fwd(
        h0, lam_re, lam_im, bbd_re, bbd_im, cbd_re, cbd_im, s5p["d"], _ag_comm([glu_sh, gin_sh, ff1a_sh, ff2a_sh]), tc=s5_tc)
    gin_full = gin_s.transpose(1, 0, 2).reshape(D_MODEL, GLA_IN)
    q_, k_, v_, glr_, r_ = jnp.split(gin_full, [GLA_QK, 2 * GLA_QK, 2 * GLA_QK + D_MODEL, 2 * GLA_QK + D_MODEL + GLA_RANK], axis=1)
    w["glu"] = glu_s.transpose(1, 0, 2).reshape(D_MODEL, 2 * D_MODEL)
    w["gin"] = jnp.concatenate([q_, k_, v_, r_, glr_, jnp.zeros((D_MODEL, GLA_RANK_PAD - GLA_RANK), MXU_DT)], axis=1)
    w["ff1"] = [ff1a_s.transpose(1, 0, 2).reshape(D_MODEL, D_FF), None]
    w["ff2"] = [ff2a_s.reshape(D_FF, D_MODEL), None]
    vg = _mm(z0, w["glu"], "nn", F32, tm=tmm, tn=2048, name="glu_mm")
    x1, h1 = rc(_f_glu_res_pn, [_full(x), (vg, D_MODEL, 0), (vg, D_MODEL, 1)], [gt1a, nrm["mlp"][0], sc2a, sh2a],
                [row32, row16], [], name="node1")
    relu = lambda acc: jnp.maximum(acc, 0.0)
    sq = lambda a: a * a
    a0 = _mm(h1, w["ff1"][0], "nn", MXU_DT, tm=tmm, tn=2048, name="ff1a", out_fn=relu)
    f0 = _mm(a0, w["ff2"][0], "nn", F32, tm=tmm, tn=1024, name="ff2a", a_fn=sq)
    x2, h2 = rc(_f_res_pn, [_full(x1), _full(f0)], [gt2a, nrm["mix"][1], sc1b, sh1b], [row32, row16], [], name="node2")
    proj = _mm(h2, w["gin"], "nn", MXU_DT, tm=tmm, tn=GLA_INP, name="gla_in")
    og, s_prev, gout_s, ff1b_s, ff2b_s = _gla_fwd(proj, w["wg2"], w["bg"], w["gn"], w["tri"], _ag_comm([gout_sh, ff1b_sh, ff2b_sh]))
    w["gout"] = gout_s.reshape(D_MODEL, D_MODEL)
    w["ff1"][1] = ff1b_s.transpose(1, 0, 2).reshape(D_MODEL, D_FF)
    w["ff2"][1] = ff2b_s.reshape(D_FF, D_MODEL)
    y1 = _mm(og, w["gout"], "nn", F32, tm=tmm, tn=1024, name="gla_out")
    x3, h3 = rc(_f_res_pn, [_full(x2), _full(y1)], [gt1b, nrm["mlp"][1], sc2b, sh2b], [row32, row16], [], name="node3")
    a1 = _mm(h3, w["ff1"][1], "nn", MXU_DT, tm=tmm, tn=2048, name="ff1b", out_fn=relu)
    f1 = _mm(a1, w["ff2"][1], "nn", F32, tm=tmm, tn=1024, name="ff2b", a_fn=sq)

    g = {}
    dx, df, g["gt2b"], g["nf"], loss = rc(_g_final, [_full(x3), _full(f1), _full(tgt)], [gt2b, nrm["final"]],
                                          [row32, row16], [vec, vec, (8, 128)], name="final")

    def mlp_bwd(df, a, h, w1, w2, tag):
        dw2 = _mm(a, df, "tn", MXU_DT, tm=1024, tn=1024, name="dff2" + tag, a_fn=sq)
        du = _mm(df, w2, "nt", MXU_DT, tm=tmm, tn=2048, name="dact" + tag, extra=a, out_fn=lambda acc, e: acc * (2.0 * e))
        dw1 = _mm(h, du, "tn", MXU_DT, tm=1024, tn=D_FF // N_DEV, name="dff1" + tag, by_owner=True)
        dh = _mm(du, w1, "nt", F32, tm=tmm, tn=1024, name="dh" + tag)
        return dw1, dw2, dh

    def node_bwd(f, prim_rows, cots, pars, row_want, outs, name):
        nrow = len(prim_rows)
        return rc(_vjp_of(f, nrow, len(cots), row_want), prim_rows + cots, pars, outs, [vec] * len(pars), name=name)

    g["ff1b"], g["ff2b"], dh3 = mlp_bwd(df, a1, h3, w["ff1"][1], w["ff2"][1], "b")
    dx, dy1, g["gt1b"], g["mlp1"], g["sc2b"], g["sh2b"] = node_bwd(
        _f_res_pn, [_full(x2), _full(y1)], [_full(dx), _full(dh3)], [gt1b, nrm["mlp"][1], sc2b, sh2b], (0, 1),
        [row32, row16], "node3_bwd")
    g["gout"] = _mm(og, dy1, "tn", MXU_DT, tm=512, tn=1024, name="dgout")
    d_og = _mm(dy1, w["gout"], "nt", MXU_DT, tm=tmm, tn=1024, name="dog")
    dproj, g["wg2"], g["bg"], g["gn"] = _gla_bwd(proj, d_og, s_prev, w["wg2"], w["bg"], w["gn"], w["tri"])
    g["gin"] = _mm(h2, dproj, "tn", MXU_DT, tm=512, tn=640, name="dgin")
    dh2 = _mm(dproj, w["gin"], "nt", F32, tm=tmm, tn=1024, name="dh2")
    dx, df0, g["gt2a"], g["mix1"], g["sc1b"], g["sh1b"] = node_bwd(
        _f_res_pn, [_full(x1), _full(f0)], [_full(dx), _full(dh2)], [gt2a, nrm["mix"][1], sc1b, sh1b], (0, 1),
        [row32, row16], "node2_bwd")
    g["ff1a"], g["ff2a"], dh1 = mlp_bwd(df0, a0, h1, w["ff1"][0], w["ff2"][0], "a")
    glu_vjp = _vjp_of(_f_glu_res_pn, 3, 2, (0, 1, 2))

    def glu_bwd(*a):
        r = glu_vjp(*a)
        return (r[0], jnp.concatenate([r[1], r[2]], axis=1)) + r[3:]

    dx, dvg, g["gt1a"], g["mlp0"], g["sc2a"], g["sh2a"] = rc(
        glu_bwd, [_full(x), (vg, D_MODEL, 0), (vg, D_MODEL, 1), _full(dx), _full(dh1)], [gt1a, nrm["mlp"][0], sc2a, sh2a],
        [row32, (2 * D_MODEL, MXU_DT)], [vec] * 4, name="node1_bwd")
    g["glu"] = _mm(z0, dvg, "tn", MXU_DT, tm=1024, tn=2 * D_MODEL // N_DEV, name="dglu", by_owner=True)
    dz0 = _mm(dvg, w["glu"], "nt", MXU_DT, tm=tmm, tn=1024, name="dz0")
    gin_g = g.pop("gin")
    gin_g = jnp.concatenate([gin_g[:, :2 * GLA_QK + D_MODEL], gin_g[:, GLA_INP - GLA_RANK_PAD:GLA_INP - GLA_RANK_PAD + GLA_RANK],
                             gin_g[:, 2 * GLA_QK + D_MODEL:2 * GLA_QK + 2 * D_MODEL]], axis=1)
    per_owner = [g.pop("glu"), gin_g.reshape(D_MODEL, N_DEV, GLA_IN // N_DEV).transpose(1, 0, 2),
                 g.pop("gout").reshape(N_DEV, D_MODEL // N_DEV, D_MODEL), g.pop("ff1a"), g.pop("ff1b"),
                 g.pop("ff2a").reshape(N_DEV, D_FF // N_DEV, D_MODEL), g.pop("ff2b").reshape(N_DEV, D_FF // N_DEV, D_MODEL)]
    from_sibling = _swap_sibling(per_owner)
    chip_sum = [_pair_add(a, b, core, name="rs_add%d" % i) for i, (a, b) in enumerate(zip(per_owner, from_sibling))]
    res = _s5_bwd(h0, dz0, st_re, st_im, lam_re, lam_im, bbd_re, bbd_im, cbd_re, cbd_im, s5p["d"], _chips_comm(chip_sum), tc=s5_tc)
    du0, dbbd_re, dbbd_im, dcbd_re, dcbd_im, dlam_re, dlam_im, g["s5_d"] = res[:8]
    from_chips = res[8:]
    g["s5_c_re"] = _unblockdiag_c(dcbd_re)
    g["s5_c_im"] = _unblockdiag_c(dcbd_im)
    g["s5_a_re"], g["s5_a_im"], g["s5_log_dt"], g["s5_bt_re"], g["s5_bt_im"] = _s5_prep_bwd(
        s5p["a_re"], s5p["a_im"], s5p["log_dt"], s5p["bt_re"], s5p["bt_im"], s5p["e01"],
        dlam_re, dlam_im, _unblockdiag_b(dbbd_re), _unblockdiag_b(dbbd_im))
    grad_x, g["mix0"], g["sc1a"], g["sh1a"] = node_bwd(
        _f_pn, [_full(x)], [_full(dx), _full(du0)], [nrm["mix"][0], sc1a, sh1a], (0,), [row32], "node0_bwd")
    return loss[0, 0], grad_x, g, chip_sum, from_chips


_MESH = pl.DeviceIdType.MESH
_VMEM_SPEC = pl.BlockSpec(memory_space=pltpu.VMEM)
_ANY_SPEC = pl.BlockSpec(memory_space=pl.ANY)


def _my_place():
    ix, iy, ic = lax.axis_index("x"), lax.axis_index("y"), lax.axis_index("c")
    return ix, iy, ic


def _exchange(x, *, gather, name):
    r = x.shape[-2]

    def body(x_ref, o_ref, ssem, rsem):
        ix, iy, ic = _my_place()
        me = 4 * ix + 2 * iy + ic
        if gather:
            o_ref[me] = x_ref[...]
        else:
            o_ref[me] = x_ref[me]
        copies = []
        for k in range(1, N_DEV):
            tx, ty, tc = ix ^ (k >> 2), iy ^ ((k >> 1) & 1), ic ^ (k & 1)
            src = x_ref if gather else x_ref.at[4 * tx + 2 * ty + tc]
            cp = pltpu.make_async_remote_copy(src_ref=src, dst_ref=o_ref.at[me], send_sem=ssem.at[k - 1],
                                              recv_sem=rsem.at[k - 1], device_id=(tx, ty, tc), device_id_type=_MESH)
            cp.start()
            copies.append(cp)
        for cp in copies:
            cp.wait()

    return pl.pallas_call(
        body, out_shape=jax.ShapeDtypeStruct((N_DEV, r, 128), x.dtype), in_specs=[_VMEM_SPEC], out_specs=_VMEM_SPEC,
        scratch_shapes=[pltpu.SemaphoreType.DMA((N_DEV - 1,)), pltpu.SemaphoreType.DMA((N_DEV - 1,))], name=name,
    )(x)


class _Comm:
    def __init__(self, ins, out_shapes, scratch, phases):
        self.ins, self.out_shapes, self.scratch, self.phases = list(ins), list(out_shapes), list(scratch), phases

    def run(self, step, n_steps, in_refs, out_refs, scratch_refs):
        when = {"first": 0, "mid": (3 * n_steps) // 8, "last": n_steps - 1}
        for phase, fn in self.phases:
            pl.when(step == when[phase])(functools.partial(fn, in_refs, out_refs, scratch_refs))


def _ag_comm(xs):
    n = len(xs)

    def parts():
        ix, iy, ic = _my_place()
        return ic, (ix, iy, ic), (ix, iy, 1 - ic), [(1 - ix, iy), (ix, 1 - iy), (1 - ix, 1 - iy)]

    def copy(ins, outs, sc, a, k, block, to, from_x=False):
        px, py, pc = block
        slot = outs[a].at[4 * px + 2 * py + pc]
        return pltpu.make_async_remote_copy(
            src_ref=ins[a] if from_x else slot, dst_ref=slot, send_sem=sc[0].at[7 * a + k], recv_sem=sc[1].at[7 * a + k],
            device_id=to, device_id_type=_MESH)

    def local(ins, outs, sc, a, me):
        return pltpu.make_async_copy(ins[a], outs[a].at[4 * me[0] + 2 * me[1] + me[2]], sc[2].at[a])

    def start(ins, outs, sc):
        ic, me, sibling, chips = parts()
        for a in range(n):
            local(ins, outs, sc, a, me).start()
            copy(ins, outs, sc, a, 0, me, sibling, True).start()
            for j, chip in enumerate(chips):
                copy(ins, outs, sc, a, 1 + j, me, (*chip, ic), True).start()

    def forward(ins, outs, sc):
        ic, me, sibling, chips = parts()
        for a in range(n):
            for j, chip in enumerate(chips):
                copy(ins, outs, sc, a, 1 + j, (*chip, ic), me).wait_recv()
                copy(ins, outs, sc, a, 4 + j, (*chip, ic), sibling).start()

    def finish(ins, outs, sc):
        ic, me, sibling, chips = parts()
        for a in range(n):
            copy(ins, outs, sc, a, 0, sibling, me).wait_recv()
            for j, chip in enumerate(chips):
                copy(ins, outs, sc, a, 4 + j, (*chip, 1 - ic), me).wait_recv()
        for a in range(n):
            copy(ins, outs, sc, a, 0, me, sibling, True).wait_send()
            for j, chip in enumerate(chips):
                copy(ins, outs, sc, a, 1 + j, me, (*chip, ic), True).wait_send()
                copy(ins, outs, sc, a, 4 + j, (*chip, ic), sibling).wait_send()
            local(ins, outs, sc, a, me).wait()

    return _Comm(xs, [jax.ShapeDtypeStruct((N_DEV,) + x.shape, x.dtype) for x in xs],
                 [pltpu.SemaphoreType.DMA((7 * n,)), pltpu.SemaphoreType.DMA((7 * n,)), pltpu.SemaphoreType.DMA((n,))],
                 [("first", start), ("mid", forward), ("last", finish)])


def _chips_comm(ps):
    n = len(ps)

    def copies(ins, outs, sc):
        ix, iy, ic = _my_place()
        out = []
        for a in range(n):
            for k in range(1, 4):
                tx, ty = ix ^ (k >> 1), iy ^ (k & 1)
                out.append(pltpu.make_async_remote_copy(
                    src_ref=ins[a].at[2 * tx + ty], dst_ref=outs[a].at[k - 1], send_sem=sc[0].at[3 * a + k - 1],
                    recv_sem=sc[1].at[3 * a + k - 1], device_id=(tx, ty, ic), device_id_type=_MESH))
        return out

    def start(ins, outs, sc):
        for cp in copies(ins, outs, sc):
            cp.start()

    def finish(ins, outs, sc):
        for cp in copies(ins, outs, sc):
            cp.wait()

    return _Comm(ps, [jax.ShapeDtypeStruct((3,) + p.shape[1:], p.dtype) for p in ps],
                 [pltpu.SemaphoreType.DMA((3 * n,)), pltpu.SemaphoreType.DMA((3 * n,))], [("first", start), ("last", finish)])


def _swap_sibling(gs):
    n = len(gs)

    def body(*refs):
        g_refs, o_refs = refs[:n], refs[n:2 * n]
        ssem, rsem = refs[2 * n:]
        ix, iy, ic = _my_place()
        copies = []
        for a in range(n):
            for q in range(4):
                cp = pltpu.make_async_remote_copy(src_ref=g_refs[a].at[2 * q + 1 - ic], dst_ref=o_refs[a].at[q],
                                                  send_sem=ssem.at[4 * a + q], recv_sem=rsem.at[4 * a + q],
                                                  device_id=(ix, iy, 1 - ic), device_id_type=_MESH)
                cp.start()
                copies.append(cp)
        for cp in copies:
            cp.wait()

    return pl.pallas_call(
        body, out_shape=[jax.ShapeDtypeStruct((4,) + g.shape[1:], g.dtype) for g in gs], in_specs=[_ANY_SPEC] * n,
        out_specs=[_ANY_SPEC] * n,
        scratch_shapes=[pltpu.SemaphoreType.DMA((4 * n,)), pltpu.SemaphoreType.DMA((4 * n,))], name="rs_sibling",
    )(*gs)


def _ada_fwd(c_all, w_ada, b_cols):
    def body(c_ref, w_ref, b_ref, o_ref):
        cs = _silu(c_ref[...])
        for i in range(2):
            o_ref[i] = _dot(cs, w_ref[i]) + b_ref[pl.ds(i, 1), :]
    return pl.pallas_call(body, out_shape=jax.ShapeDtypeStruct((2, N_DEV, w_ada.shape[2]), F32), name="ada_fwd",
                          compiler_params=pltpu.CompilerParams(vmem_limit_bytes=VMEM_LIMIT))(c_all, w_ada, b_cols)


def _ada_bwd(c_all, dm):
    def body(c_ref, d_ref, o_ref):
        cs = _silu(c_ref[...])
        for i in range(2):
            o_ref[i] = _dot(cs, d_ref[i], _TN)
    return pl.pallas_call(body, out_shape=jax.ShapeDtypeStruct((2, D_MODEL, dm.shape[2]), F32), name="ada_bwd",
                          compiler_params=pltpu.CompilerParams(vmem_limit_bytes=VMEM_LIMIT))(c_all, dm)


def _pair_add(g, recv, core, *, name):
    _, r, cdim = g.shape

    def body(core_ref, g_ref, r_ref, o_ref):
        o_ref[...] = (g_ref[...].astype(F32) + r_ref[...].astype(F32)).astype(o_ref.dtype)

    grid_spec = pltpu.PrefetchScalarGridSpec(
        num_scalar_prefetch=1, grid=(4,),
        in_specs=[pl.BlockSpec((1, r, cdim), lambda q, core_ref: (2 * q + core_ref[0], 0, 0)),
                  pl.BlockSpec((1, r, cdim), lambda q, core_ref: (q, 0, 0))],
        out_specs=pl.BlockSpec((1, r, cdim), lambda q, core_ref: (q, 0, 0)))
    return pl.pallas_call(body, grid_spec=grid_spec, out_shape=jax.ShapeDtypeStruct((4, r, cdim), g.dtype), name=name,
                          compiler_params=_params(("parallel",)))(core, g, recv)


def _sum_slots(x, *, name):
    def body(x_ref, o_ref):
        acc = x_ref[0]
        for s in range(1, N_DEV):
            acc = acc + x_ref[s]
        o_ref[...] = acc
    return pl.pallas_call(body, out_shape=jax.ShapeDtypeStruct(x.shape[1:], F32), name=name)(x)


def _adamw(w, m, v, gparts, *, tile, name, sel=None):
    r, cdim = w.shape
    ng = len(gparts)
    sel = jnp.zeros((1,), jnp.int32) if sel is None else sel

    def body(*refs):
        w_ref, m_ref, v_ref = refs[1:4]
        g = None
        for p, part in zip(refs[4:4 + ng], gparts):
            pv = (p[0] if isinstance(part, tuple) else p[...]).astype(F32)
            g = pv if g is None else g + pv
        g_ref, d_ref, nm_ref, nv_ref = refs[4 + ng:]
        mn = ADAM_B1 * m_ref[...] + (1.0 - ADAM_B1) * g
        vn = ADAM_B2 * v_ref[...] + (1.0 - ADAM_B2) * (g * g)
        m_hat = mn / (1.0 - ADAM_B1 ** ADAM_STEP)
        v_hat = vn / (1.0 - ADAM_B2 ** ADAM_STEP)
        g_ref[...] = g
        d_ref[...] = -ADAM_LR * (m_hat / (jnp.sqrt(v_hat) + ADAM_EPS) + ADAM_WD * w_ref[...])
        nm_ref[...] = mn
        nv_ref[...] = vn

    spec = pl.BlockSpec((tile, cdim), lambda i, s: (i, 0))

    def part_spec(part):
        if not isinstance(part, tuple):
            return spec
        slab = part[1]
        if slab is None:
            return pl.BlockSpec((1, tile, cdim), lambda i, s: (s[0], i, 0))
        return pl.BlockSpec((1, tile, cdim), lambda i, s: (slab, i, 0))

    grid_spec = pltpu.PrefetchScalarGridSpec(
        num_scalar_prefetch=1, grid=(r // tile,), in_specs=[spec] * 3 + [part_spec(p) for p in gparts], out_specs=[spec] * 4)
    return pl.pallas_call(
        body, grid_spec=grid_spec, out_shape=[jax.ShapeDtypeStruct(w.shape, F32)] * 4, name=name,
        compiler_params=_params(("parallel",)),
    )(sel, w, m, v, *[p[0] if isinstance(p, tuple) else p for p in gparts])


_REP_ROWS = 272
_REP_SIZE = 2 * 1024 * 2 + 4096 * 2 + 64 + 65536 * 4 + 1024 + 1024


def _pad_rows(v, rows):
    return jnp.pad(v.reshape(-1), (0, rows * 128 - v.size)).reshape(rows, 128)


def kernel(x, c, w_ada, b_ada, norm_mix, norm_mlp, s5_a_re, s5_a_im, s5_log_dt, s5_b_re, s5_b_im, s5_c_re, s5_c_im, s5_d, s5_w_glu, gla_w_in, gla_w_gate2, gla_b_gate, gla_g_norm, gla_w_out, w_ff1, w_ff2, norm_final, loss_target, m_w_ada, m_b_ada, m_norm_mix, m_norm_mlp, m_s5_a_re, m_s5_a_im, m_s5_log_dt, m_s5_b_re, m_s5_b_im, m_s5_c_re, m_s5_c_im, m_s5_d, m_s5_w_glu, m_gla_w_in, m_gla_w_gate2, m_gla_b_gate, m_gla_g_norm, m_gla_w_out, m_w_ff1, m_w_ff2, m_norm_final, v_w_ada, v_b_ada, v_norm_mix, v_norm_mlp, v_s5_a_re, v_s5_a_im, v_s5_log_dt, v_s5_b_re, v_s5_b_im, v_s5_c_re, v_s5_c_im, v_s5_d, v_s5_w_glu, v_gla_w_in, v_gla_w_gate2, v_gla_b_gate, v_gla_g_norm, v_gla_w_out, v_w_ff1, v_w_ff2, v_norm_final):
    ix, iy, ic = _my_place()
    me = 4 * ix + 2 * iy + ic
    ada_w = w_ada.shape[2]

    msg = jnp.concatenate([c.reshape(8, 128), gla_w_gate2[0].reshape(8, 128), _pad_rows(gla_b_gate, 1),
                           gla_g_norm.reshape(1, 128), jnp.zeros((6, 128), F32)])
    got = _exchange(msg, gather=True, name="gather_small")
    c_all = got[:, 0:8].reshape(N_DEV, D_MODEL)
    wg2 = got[:, 8:16].reshape(N_DEV, GLA_RANK, 64).transpose(1, 0, 2).reshape(GLA_RANK, GLA_QK)
    bg = got[:, 16, :64].reshape(1, GLA_QK)
    gn = got[:, 17, :].reshape(1, D_MODEL)

    b_cols = lax.dynamic_slice_in_dim(b_ada, me * ada_w, ada_w, axis=1)
    mod_cols = _ada_fwd(c_all, w_ada, b_cols)
    pay = jnp.pad(mod_cols.transpose(1, 0, 2).reshape(N_DEV, 12, 128), ((0, 0), (0, 4), (0, 0)))
    mod = _exchange(pay, gather=False, name="a2a_mod")[:, :12].reshape(N_DEV, 2, ada_w).transpose(1, 0, 2).reshape(2, 6 * D_MODEL)
    mods = [[mod[i:i + 1, j * D_MODEL:(j + 1) * D_MODEL] for j in range(6)] for i in range(2)]

    big_w = [s5_w_glu[0], gla_w_in[0], gla_w_out[0], w_ff1[0], w_ff1[1], w_ff2[0], w_ff2[1]]
    big_m = [m_s5_w_glu[0], m_gla_w_in[0], m_gla_w_out[0], m_w_ff1[0], m_w_ff1[1], m_w_ff2[0], m_w_ff2[1]]
    big_v = [v_s5_w_glu[0], v_gla_w_in[0], v_gla_w_out[0], v_w_ff1[0], v_w_ff1[1], v_w_ff2[0], v_w_ff2[1]]
    w = dict(wg2=jnp.pad(wg2, ((0, GLA_RANK_PAD - GLA_RANK), (0, 0))), bg=bg, gn=gn, tri=jnp.tril(jnp.ones((GLA_CHUNK, GLA_CHUNK), F32)))
    core = ic.reshape(1).astype(jnp.int32)
    chip = (2 * ix + iy).reshape(1).astype(jnp.int32)
    nrm = dict(mix=[norm_mix[i:i + 1] for i in range(2)], mlp=[norm_mlp[i:i + 1] for i in range(2)], final=norm_final.reshape(1, D_MODEL))
    e01 = (lax.broadcasted_iota(jnp.int32, (S5_GROUPS, S5_NSTATE), 1) // S5_STATE
           == lax.broadcasted_iota(jnp.int32, (S5_GROUPS, S5_NSTATE), 0)).astype(F32)
    s5p = dict(a_re=s5_a_re.reshape(1, S5_NSTATE), a_im=s5_a_im.reshape(1, S5_NSTATE), log_dt=s5_log_dt,
               bt_re=s5_b_re[0].transpose(2, 0, 1).reshape(S5_GROUP, S5_NSTATE),
               bt_im=s5_b_im[0].transpose(2, 0, 1).reshape(S5_GROUP, S5_NSTATE),
               c_re=s5_c_re[0], c_im=s5_c_im[0], d=s5_d, e01=e01)

    loss_local, grad_x, g, chip_sum, from_chips = _local_step(
        x[0], loss_target[0], mods, nrm, s5p, w, [a.astype(MXU_DT) for a in big_w], core)
    loss = lax.psum(loss_local, ("x", "y", "c"))
    big = []
    for i in range(len(big_w)):
        parts = [(chip_sum[i], None), (from_chips[i], 0), (from_chips[i], 1), (from_chips[i], 2)]
        big.append(_adamw(big_w[i], big_m[i], big_v[i], parts, tile=min(512, big_w[i].shape[0]), name="adamw_big%d" % i, sel=chip))

    rep = [jnp.concatenate([g["mix0"], g["mix1"]]), jnp.concatenate([g["mlp0"], g["mlp1"]]), g["s5_a_re"], g["s5_a_im"], g["s5_log_dt"],
           g["s5_bt_re"].reshape(S5_GROUP, S5_GROUPS, S5_STATE).transpose(1, 2, 0), g["s5_bt_im"].reshape(S5_GROUP, S5_GROUPS, S5_STATE).transpose(1, 2, 0),
           g["s5_c_re"], g["s5_c_im"], g["s5_d"], g["nf"]]
    rep_shapes = [(2, D_MODEL), (2, D_MODEL), (1, 64, 64), (1, 64, 64), (1, 64), (1, 64, 64, 16), (1, 64, 64, 16), (1, 64, 16, 64), (1, 64, 16, 64), (1, D_MODEL), (D_MODEL,)]
    rep_flat = jnp.concatenate([a.reshape(-1) for a in rep])
    rep_blk = jnp.pad(rep_flat, (0, N_DEV * _REP_ROWS * 128 - _REP_SIZE)).reshape(N_DEV, _REP_ROWS, 128)
    dmod = jnp.stack([jnp.concatenate([g["sh1" + t], g["sc1" + t], g["gt1" + t], g["sh2" + t], g["sc2" + t], g["gt2" + t]], axis=1)[0] for t in "ab"])
    msg = jnp.concatenate([
        rep_blk,
        g["wg2"][:GLA_RANK].reshape(GLA_RANK, N_DEV, 64).transpose(1, 0, 2).reshape(N_DEV, 8, 128),
        jnp.pad(g["bg"].reshape(N_DEV, 1, 64), ((0, 0), (0, 0), (0, 64))),
        g["gn"].reshape(N_DEV, 1, 128),
        dmod.reshape(2, N_DEV, ada_w).transpose(1, 0, 2).reshape(N_DEV, 12, 128),
        jnp.zeros((N_DEV, 2, 128), F32),
    ], axis=1)
    got = _exchange(msg, gather=False, name="a2a_small_grads")
    tot = _sum_slots(got, name="sum_small_grads")
    dm = got[:, 282:294].reshape(N_DEV, 2, ada_w).transpose(1, 0, 2)
    g_w_ada = _ada_bwd(c_all, dm)
    back = _exchange(jnp.concatenate([tot[0:_REP_ROWS], tot[282:294], jnp.zeros((4, 128), F32)]), gather=True, name="gather_small_grads")
    rep_sum = back[:, :_REP_ROWS].reshape(-1)[:_REP_SIZE]
    g_b_ada = back[:, _REP_ROWS:_REP_ROWS + 12].reshape(N_DEV, 2, ada_w).transpose(1, 0, 2).reshape(2, 6 * D_MODEL)
    g_rep, off = [], 0
    for s in rep_shapes:
        n = math.prod(s)
        g_rep.append(rep_sum[off:off + n].reshape(s))
        off += n
    g_small = g_rep + [g_b_ada, tot[272:280].reshape(GLA_RANK, 64)[None], tot[280, :64][None], tot[281][None]]
    p_small = [norm_mix, norm_mlp, s5_a_re, s5_a_im, s5_log_dt, s5_b_re, s5_b_im, s5_c_re, s5_c_im, s5_d, norm_final, b_ada, gla_w_gate2, gla_b_gate, gla_g_norm]
    m_small = [m_norm_mix, m_norm_mlp, m_s5_a_re, m_s5_a_im, m_s5_log_dt, m_s5_b_re, m_s5_b_im, m_s5_c_re, m_s5_c_im, m_s5_d, m_norm_final, m_b_ada, m_gla_w_gate2, m_gla_b_gate, m_gla_g_norm]
    v_small = [v_norm_mix, v_norm_mlp, v_s5_a_re, v_s5_a_im, v_s5_log_dt, v_s5_b_re, v_s5_b_im, v_s5_c_re, v_s5_c_im, v_s5_d, v_norm_final, v_b_ada, v_gla_w_gate2, v_gla_b_gate, v_gla_g_norm]
    n_small = sum(a.size for a in p_small)
    rows_small = -(-n_small // 128)
    cat = lambda lst: _pad_rows(jnp.concatenate([a.reshape(-1) for a in lst]), rows_small)
    small = _adamw(cat(p_small), cat(m_small), cat(v_small), [cat(g_small)], tile=rows_small, name="adamw_small")

    def split_small(a):
        flat, out, off = a.reshape(-1), [], 0
        for p in p_small:
            out.append(flat[off:off + p.size].reshape(p.shape))
            off += p.size
        return out
    small = [split_small(a) for a in small]
    ada = _adamw(w_ada.reshape(2 * D_MODEL, ada_w), m_w_ada.reshape(2 * D_MODEL, ada_w), v_w_ada.reshape(2 * D_MODEL, ada_w),
                 [g_w_ada.reshape(2 * D_MODEL, ada_w)], tile=512, name="adamw_ada")
    ada = [a.reshape(w_ada.shape) for a in ada]

    def leaves(k):
        nm, nl, a_re, a_im, ldt, b_re, b_im, c_re, c_im, dsk, nf, bada, wg2_, bg_, gn_ = small[k]
        glu_, gin_, gout_, ff1a_, ff1b_, ff2a_, ff2b_ = [b[k] for b in big]
        return [ada[k], bada, nm, nl, a_re, a_im, ldt, b_re, b_im, c_re, c_im, dsk, glu_[None], gin_[None], wg2_, bg_, gn_, gout_[None],
                jnp.stack([ff1a_, ff1b_]), jnp.stack([ff2a_, ff2b_]), nf]

    return (loss, grad_x[None], *leaves(0), *leaves(1), *leaves(2), *leaves(3))
```

```python
import functools
import math

import jax
import jax.numpy as jnp
from jax import lax
from jax.experimental import pallas as pl
from jax.experimental.pallas import tpu as pltpu

F32 = jnp.float32
BF16 = jnp.bfloat16
MXU_DT = BF16
EPS = 1e-6
N_DEV = 8
VMEM_LIMIT = 56 * 1024 * 1024

D_MODEL = 1024
S5_GROUP = 16
S5_GROUPS = 64
S5_STATE = 64
S5_NSTATE = S5_GROUPS * S5_STATE
S5_GB = 16
S5_NB = S5_GROUPS // S5_GB
S5_BC = S5_GB * S5_GROUP
S5_BS = S5_GB * S5_STATE
GLA_HEADS = 4
GLA_QK = 512
GLA_DK = 128
GLA_DV = 256
GLA_RANK = 16
GLA_RANK_PAD = 128
GLA_TAU = 16.0
GLA_CHUNK = 64
GLA_IN = 3088
GLA_INP = 2 * GLA_QK + 2 * D_MODEL + GLA_RANK_PAD
D_FF = 4096

ADAM_LR = 0.001
ADAM_B1 = 0.9
ADAM_B2 = 0.999
ADAM_EPS = 1e-08
ADAM_WD = 0.01
ADAM_STEP = 10

_NN = (((1,), (0,)), ((), ()))
_NT = (((1,), (1,)), ((), ()))
_TN = (((0,), (0,)), ((), ()))


def _dot(a, b, dn=_NN):
    return lax.dot_general(a.astype(MXU_DT), b.astype(MXU_DT), dn, preferred_element_type=F32)


def _dot_exact01(x, m01, dn=_NN):
    x1 = x.astype(BF16)
    r1 = x - x1.astype(F32)
    x2 = r1.astype(BF16)
    x3 = (r1 - x2.astype(F32)).astype(BF16)
    m = m01.astype(BF16)
    d = lambda u: lax.dot_general(u, m, dn, preferred_element_type=F32)
    return d(x1) + d(x2) + d(x3)


def _dot_f32(a, b, dn=_NN):
    def split(x):
        x1 = x.astype(BF16)
        r1 = x - x1.astype(F32)
        x2 = r1.astype(BF16)
        x3 = (r1 - x2.astype(F32)).astype(BF16)
        return x1, x2, x3
    a1, a2, a3 = split(a)
    b1, b2, b3 = split(b)
    d = lambda u, v: lax.dot_general(u, v, dn, preferred_element_type=F32)
    return (d(a1, b1) + (d(a1, b2) + d(a2, b1)) + (d(a2, b2) + d(a1, b3) + d(a3, b1)))


def _sigmoid(x):
    return 1.0 / (1.0 + jnp.exp(-x))


def _silu(x):
    return x * _sigmoid(x)


def _gelu(x):
    return 0.5 * x * (1.0 + jnp.tanh(math.sqrt(2.0 / math.pi) * (x + 0.044715 * (x * x * x))))


def _logsig(x):
    return jnp.minimum(x, 0.0) - jnp.log(1.0 + jnp.exp(-jnp.abs(x)))


def _rms(x):
    return lax.rsqrt(jnp.mean(x * x, axis=-1, keepdims=True) + EPS)


def _params(sem):
    return pltpu.CompilerParams(dimension_semantics=sem, vmem_limit_bytes=VMEM_LIMIT)


def _mm(a, b, dims, out_dtype, *, tm, tn, name, a_fn=None, out_fn=None, extra=None, by_owner=False, comm=None):
    if dims == "tn":
        k, m = a.shape
        n = b.shape[1]
    else:
        m, k = a.shape
        n = b.shape[0] if dims == "nt" else b.shape[1]
    tm, tn = min(tm, m), min(tn, n)
    assert m % tm == 0 and n % tn == 0, (name, m, n, tm, tn)
    dn = {"nn": _NN, "nt": _NT, "tn": _TN}[dims]
    n_in = 2 if extra is None else 3
    nci, nco = (len(comm.ins), len(comm.out_shapes)) if comm is not None else (0, 0)

    def body(*refs):
        a_ref, b_ref = refs[0], refs[1]
        o_ref = refs[n_in + nci]
        if comm is not None:
            comm.run(pl.program_id(0) * (n // tn) + pl.program_id(1), (m // tm) * (n // tn), refs[n_in:n_in + nci],
                     refs[n_in + nci + 1:n_in + nci + 1 + nco], refs[n_in + nci + 1 + nco:])
        av = a_ref[...]
        if a_fn is not None:
            av = a_fn(av.astype(F32))
        acc = _dot(av, b_ref[...], dn)
        if extra is not None:
            acc = out_fn(acc, refs[2][...].astype(F32))
        elif out_fn is not None:
            acc = out_fn(acc)
        if by_owner:
            o_ref[0] = acc.astype(o_ref.dtype)
        else:
            o_ref[...] = acc.astype(o_ref.dtype)

    a_spec = pl.BlockSpec((k, tm), lambda i, j: (0, i)) if dims == "tn" else pl.BlockSpec((tm, k), lambda i, j: (i, 0))
    b_spec = pl.BlockSpec((tn, k), lambda i, j: (j, 0)) if dims == "nt" else pl.BlockSpec((k, tn), lambda i, j: (0, j))
    if by_owner:
        o_spec = pl.BlockSpec((1, tm, tn), lambda i, j: (j, i, 0))
        out_shape = jax.ShapeDtypeStruct((n // tn, m, tn), out_dtype)
    else:
        o_spec = pl.BlockSpec((tm, tn), lambda i, j: (i, j))
        out_shape = jax.ShapeDtypeStruct((m, n), out_dtype)
    in_specs, args = [a_spec, b_spec], [a, b]
    if extra is not None:
        in_specs.append(o_spec)
        args.append(extra)
    if comm is None:
        return pl.pallas_call(
            body, grid=(m // tm, n // tn), in_specs=in_specs, out_specs=o_spec, out_shape=out_shape, name=name,
            compiler_params=_params(("parallel", "parallel")),
        )(*args)
    return pl.pallas_call(
        body, grid=(m // tm, n // tn), in_specs=in_specs + [_ANY_SPEC] * nci, out_specs=[o_spec] + [_ANY_SPEC] * nco,
        out_shape=[out_shape] + comm.out_shapes, scratch_shapes=comm.scratch, name=name,
        compiler_params=_params(("arbitrary", "arbitrary")),
    )(*args, *comm.ins)


def _rowcall(f, rows, pars, outs, accs, *, tile, name):
    length = rows[0][0].shape[0]
    tile = min(tile, length)
    nr, npar, no = len(rows), len(pars), len(outs)

    def body(*refs):
        vals = [r[...].astype(F32) for r in refs[:nr + npar]]
        res = f(*vals)
        o_refs = refs[nr + npar:nr + npar + no]
        a_refs = refs[nr + npar + no:]
        for o, v in zip(o_refs, res[:no]):
            o[...] = v.astype(o.dtype)
        if a_refs:
            @pl.when(pl.program_id(0) == 0)
            def _():
                for a in a_refs:
                    a[...] = jnp.zeros(a.shape, F32)
            for a, v in zip(a_refs, res[no:]):
                a[...] += jnp.broadcast_to(v, a.shape)

    in_specs = [pl.BlockSpec((tile, w), lambda i, cb=cb: (i, cb)) for (_, w, cb) in rows]
    in_specs += [pl.BlockSpec(p.shape, lambda i: (0, 0)) for p in pars]
    out_specs = [pl.BlockSpec((tile, w), lambda i: (i, 0)) for (w, _) in outs]
    out_specs += [pl.BlockSpec(s, lambda i: (0, 0)) for s in accs]
    out_shape = [jax.ShapeDtypeStruct((length, w), dt) for (w, dt) in outs]
    out_shape += [jax.ShapeDtypeStruct(s, F32) for s in accs]
    return pl.pallas_call(
        body, grid=(length // tile,), in_specs=in_specs, out_specs=out_specs, out_shape=out_shape, name=name,
        compiler_params=_params(("arbitrary",)),
    )(*[r[0] for r in rows], *pars)


def _vjp_of(f, n_row, n_cot, row_want):
    def g(*a):
        prow, cots, par = a[:n_row], a[n_row:n_row + n_cot], a[n_row + n_cot:]
        _, vjp = jax.vjp(f, *prow, *par)
        grads = vjp(tuple(cots))
        return tuple(grads[i] for i in row_want) + tuple(grads[n_row:])
    return g


def _f_pn(x, g, sc, sh):
    return (x, x * _rms(x) * g * (1.0 + sc) + sh)


def _f_res_pn(x, y, gt, g, sc, sh):
    xn = x + gt * y
    return (xn, xn * _rms(xn) * g * (1.0 + sc) + sh)


def _f_glu_res_pn(x, val, gate, gt, g, sc, sh):
    xn = x + gt * (val * _sigmoid(gate))
    return (xn, xn * _rms(xn) * g * (1.0 + sc) + sh)


def _f_final(x, y, tgt, gt, g):
    xn = x + gt * y
    err = xn * _rms(xn) * g - tgt
    return 0.5 * jnp.mean(err * err, axis=-1, keepdims=True)


def _g_final(x, y, tgt, gt, g):
    lrow, vjp = jax.vjp(_f_final, x, y, tgt, gt, g)
    dx, dy, _, dgt, dg = vjp(jnp.ones_like(lrow))
    return dx, dy, dgt, dg, jnp.sum(lrow)


def _full(a):
    return (a, a.shape[1], 0)


def _s5_prep_f(a_re, a_im, log_dt, bt_re, bt_im, e01):
    dt = jnp.exp(_dot_exact01(log_dt, e01))
    mag = jnp.exp(a_re * dt)
    ph = a_im * dt
    lb_re = mag * jnp.cos(ph)
    lb_im = mag * jnp.sin(ph)
    den = a_re * a_re + a_im * a_im
    nr = lb_re - 1.0
    ni = lb_im
    f_re = (nr * a_re + ni * a_im) / den
    f_im = (ni * a_re - nr * a_im) / den
    bb_re = f_re * bt_re - f_im * bt_im
    bb_im = f_re * bt_im + f_im * bt_re
    return lb_re, lb_im, bb_re, bb_im


def _s5_prep_outs():
    return [jax.ShapeDtypeStruct((1, S5_NSTATE), F32)] * 2 + [jax.ShapeDtypeStruct((S5_GROUP, S5_NSTATE), F32)] * 2


def _s5_prep(a_re, a_im, log_dt, bt_re, bt_im, e01):
    def body(*refs):
        res = _s5_prep_f(*[r[...] for r in refs[:6]])
        for o, v in zip(refs[6:], res):
            o[...] = v
    return pl.pallas_call(body, out_shape=_s5_prep_outs(), name="s5_prep",
                          compiler_params=pltpu.CompilerParams(vmem_limit_bytes=VMEM_LIMIT))(a_re, a_im, log_dt, bt_re, bt_im, e01)


def _s5_prep_bwd(a_re, a_im, log_dt, bt_re, bt_im, e01, d_lb_re, d_lb_im, d_bb_re, d_bb_im):
    def f(a_re, a_im, log_dt, bt_re, bt_im, e01):
        @jax.custom_vjp
        def expand(v):
            return _dot_exact01(v, e01)
        expand.defvjp(lambda v: (_dot_exact01(v, e01), None), lambda _, ct: (_dot_exact01(ct, e01, _NT),))
        dt = jnp.exp(expand(log_dt))
        mag = jnp.exp(a_re * dt)
        ph = a_im * dt
        lb_re = mag * jnp.cos(ph)
        lb_im = mag * jnp.sin(ph)
        den = a_re * a_re + a_im * a_im
        nr = lb_re - 1.0
        f_re = (nr * a_re + lb_im * a_im) / den
        f_im = (lb_im * a_re - nr * a_im) / den
        return lb_re, lb_im, f_re * bt_re - f_im * bt_im, f_re * bt_im + f_im * bt_re

    def body(*refs):
        ins = [r[...] for r in refs[:5]]
        e = refs[5][...]
        cots = tuple(r[...] for r in refs[6:10])
        _, vjp = jax.vjp(lambda *p: f(*p, e), *ins)
        for o, v in zip(refs[10:], vjp(cots)):
            o[...] = v
    outs = [jax.ShapeDtypeStruct(v.shape, F32) for v in (a_re, a_im, log_dt, bt_re, bt_im)]
    return pl.pallas_call(body, out_shape=outs, name="s5_prep_bwd",
                          compiler_params=pltpu.CompilerParams(vmem_limit_bytes=VMEM_LIMIT))(
        a_re, a_im, log_dt, bt_re, bt_im, e01, d_lb_re, d_lb_im, d_bb_re, d_bb_im)


def _s5_scan(x_re, x_im, a_r, a_i, c_r, c_i, n_tiles, reverse):
    sgn = -1.0 if reverse else 1.0

    def tile(k, carry):
        cr, ci = carry
        i = (n_tiles - 1 - k) if reverse else k
        order = range(7, -1, -1) if reverse else range(8)
        for j in order:
            br = x_re[i, pl.ds(j, 1), :]
            bi = x_im[i, pl.ds(j, 1), :]
            nr = a_r * cr - (sgn * a_i) * ci + br
            ni = a_r * ci + (sgn * a_i) * cr + bi
            x_re[i, pl.ds(j, 1), :] = nr
            x_im[i, pl.ds(j, 1), :] = ni
            cr, ci = nr, ni
        return cr, ci

    return lax.fori_loop(0, n_tiles, tile, (c_r, c_i))


def _s5_fwd(u, lam_re, lam_im, bbd_re, bbd_im, cbd_re, cbd_im, d_skip, comm, *, tc):
    length = u.shape[0]
    tc = min(tc, length)
    nt = length // tc
    nci, nco = len(comm.ins), len(comm.out_shapes)

    def body(*refs):
        u_ref, lr_ref, li_ref, br_ref, bi_ref, cr_ref, ci_ref, d_ref = refs[:8]
        z_ref, sr_ref, si_ref = refs[8 + nci:11 + nci]
        xr, xi, car_r, car_i = refs[11 + nci + nco:15 + nci + nco]
        comm.run(pl.program_id(0) * nt + pl.program_id(1), S5_NB * nt, refs[8:8 + nci], refs[11 + nci:11 + nci + nco],
                 refs[15 + nci + nco:])

        @pl.when(pl.program_id(1) == 0)
        def _():
            car_r[...] = jnp.zeros_like(car_r)
            car_i[...] = jnp.zeros_like(car_i)
        sr_ref[0] = car_r[...]
        si_ref[0] = car_i[...]
        uv = u_ref[...]
        xr[...] = _dot(uv, br_ref[0]).reshape(tc // 8, 8, S5_BS)
        xi[...] = _dot(uv, bi_ref[0]).reshape(tc // 8, 8, S5_BS)
        cr, ci = _s5_scan(xr, xi, lr_ref[...], li_ref[...], car_r[...], car_i[...], tc // 8, False)
        car_r[...] = cr
        car_i[...] = ci
        y = (_dot(xr[...].reshape(tc, S5_BS), cr_ref[0]) - _dot(xi[...].reshape(tc, S5_BS), ci_ref[0]) + d_ref[...] * uv)
        z_ref[...] = _gelu(y).astype(z_ref.dtype)

    blk_u = pl.BlockSpec((tc, S5_BC), lambda g, t: (t, g))
    blk_l = pl.BlockSpec((1, S5_BS), lambda g, t: (0, g))
    blk_b = pl.BlockSpec((1, S5_BC, S5_BS), lambda g, t: (g, 0, 0))
    blk_c = pl.BlockSpec((1, S5_BS, S5_BC), lambda g, t: (g, 0, 0))
    blk_d = pl.BlockSpec((1, S5_BC), lambda g, t: (0, g))
    blk_s = pl.BlockSpec((1, 1, S5_BS), lambda g, t: (t, 0, g))
    return pl.pallas_call(
        body, grid=(S5_NB, nt),
        in_specs=[blk_u, blk_l, blk_l, blk_b, blk_b, blk_c, blk_c, blk_d] + [_ANY_SPEC] * nci,
        out_specs=[blk_u, blk_s, blk_s] + [_ANY_SPEC] * nco,
        out_shape=[jax.ShapeDtypeStruct((length, D_MODEL), MXU_DT),
                   jax.ShapeDtypeStruct((nt, 1, S5_NSTATE), F32), jax.ShapeDtypeStruct((nt, 1, S5_NSTATE), F32)] + comm.out_shapes,
        scratch_shapes=[pltpu.VMEM((tc // 8, 8, S5_BS), F32), pltpu.VMEM((tc // 8, 8, S5_BS), F32),
                        pltpu.VMEM((1, S5_BS), F32), pltpu.VMEM((1, S5_BS), F32)] + comm.scratch,
        name="s5_fwd", compiler_params=_params(("arbitrary", "arbitrary")),
    )(u, lam_re, lam_im, bbd_re, bbd_im, cbd_re, cbd_im, d_skip, *comm.ins)


def _s5_bwd(u, dz, st_re, st_im, lam_re, lam_im, bbd_re, bbd_im, cbd_re, cbd_im, d_skip, comm, *, tc):
    length = u.shape[0]
    tc = min(tc, length)
    nt = length // tc
    nci, nco = len(comm.ins), len(comm.out_shapes)

    def body(*refs):
        u_ref, dz_ref, sr_ref, si_ref, lr_ref, li_ref, br_ref, bi_ref, cr_ref, ci_ref, d_ref = refs[:11]
        du_ref, dbr_ref, dbi_ref, dcr_ref, dci_ref, dlr_ref, dli_ref, dd_ref = refs[11 + nci:19 + nci]
        xr, xi, gr, gi, car_r, car_i = refs[19 + nci + nco:25 + nci + nco]
        comm.run(pl.program_id(0) * nt + pl.program_id(1), S5_NB * nt, refs[11:11 + nci], refs[19 + nci:19 + nci + nco],
                 refs[25 + nci + nco:])

        @pl.when(pl.program_id(1) == 0)
        def _():
            car_r[...] = jnp.zeros_like(car_r)
            car_i[...] = jnp.zeros_like(car_i)
            for r in (dbr_ref, dbi_ref, dcr_ref, dci_ref, dlr_ref, dli_ref, dd_ref):
                r[...] = jnp.zeros(r.shape, F32)
        a_r, a_i = lr_ref[...], li_ref[...]
        uv = u_ref[...]
        xr[...] = _dot(uv, br_ref[0]).reshape(tc // 8, 8, S5_BS)
        xi[...] = _dot(uv, bi_ref[0]).reshape(tc // 8, 8, S5_BS)
        _s5_scan(xr, xi, a_r, a_i, sr_ref[0], si_ref[0], tc // 8, False)
        xrv = xr[...].reshape(tc, S5_BS)
        xiv = xi[...].reshape(tc, S5_BS)
        y = _dot(xrv, cr_ref[0]) - _dot(xiv, ci_ref[0]) + d_ref[...] * uv
        _, gelu_vjp = jax.vjp(_gelu, y)
        dy = gelu_vjp(dz_ref[...].astype(F32))[0]
        dd_ref[...] += jnp.sum(dy * uv, axis=0, keepdims=True)
        dcr_ref[0] += _dot(xrv, dy, _TN)
        dci_ref[0] -= _dot(xiv, dy, _TN)
        gr[...] = _dot(dy, cr_ref[0], _NT).reshape(tc // 8, 8, S5_BS)
        gi[...] = (-_dot(dy, ci_ref[0], _NT)).reshape(tc // 8, 8, S5_BS)
        cr, ci = _s5_scan(gr, gi, a_r, a_i, car_r[...], car_i[...], tc // 8, True)
        car_r[...] = cr
        car_i[...] = ci
        grv = gr[...].reshape(tc, S5_BS)
        giv = gi[...].reshape(tc, S5_BS)
        first = lax.broadcasted_iota(jnp.int32, (tc, 1), 0) == 0
        xpr = jnp.where(first, sr_ref[0], pltpu.roll(xrv, 1, 0))
        xpi = jnp.where(first, si_ref[0], pltpu.roll(xiv, 1, 0))
        dlr_ref[...] += jnp.sum(grv * xpr + giv * xpi, axis=0, keepdims=True)
        dli_ref[...] += jnp.sum(giv * xpr - grv * xpi, axis=0, keepdims=True)
        dbr_ref[0] += _dot(uv, grv, _TN)
        dbi_ref[0] += _dot(uv, giv, _TN)
        du_ref[...] = _dot(grv, br_ref[0], _NT) + _dot(giv, bi_ref[0], _NT) + d_ref[...] * dy

    rev = lambda t: nt - 1 - t
    blk_u = pl.BlockSpec((tc, S5_BC), lambda g, t: (rev(t), g))
    blk_l = pl.BlockSpec((1, S5_BS), lambda g, t: (0, g))
    blk_b = pl.BlockSpec((1, S5_BC, S5_BS), lambda g, t: (g, 0, 0))
    blk_c = pl.BlockSpec((1, S5_BS, S5_BC), lambda g, t: (g, 0, 0))
    blk_d = pl.BlockSpec((1, S5_BC), lambda g, t: (0, g))
    blk_s = pl.BlockSpec((1, 1, S5_BS), lambda g, t: (rev(t), 0, g))
    return pl.pallas_call(
        body, grid=(S5_NB, nt),
        in_specs=[blk_u, blk_u, blk_s, blk_s, blk_l, blk_l, blk_b, blk_b, blk_c, blk_c, blk_d] + [_ANY_SPEC] * nci,
        out_specs=[blk_u, blk_b, blk_b, blk_c, blk_c, blk_l, blk_l, blk_d] + [_ANY_SPEC] * nco,
        out_shape=[jax.ShapeDtypeStruct((length, D_MODEL), F32),
                   jax.ShapeDtypeStruct((S5_NB, S5_BC, S5_BS), F32), jax.ShapeDtypeStruct((S5_NB, S5_BC, S5_BS), F32),
                   jax.ShapeDtypeStruct((S5_NB, S5_BS, S5_BC), F32), jax.ShapeDtypeStruct((S5_NB, S5_BS, S5_BC), F32),
                   jax.ShapeDtypeStruct((1, S5_NSTATE), F32), jax.ShapeDtypeStruct((1, S5_NSTATE), F32),
                   jax.ShapeDtypeStruct((1, D_MODEL), F32)] + comm.out_shapes,
        scratch_shapes=[pltpu.VMEM((tc // 8, 8, S5_BS), F32), pltpu.VMEM((tc // 8, 8, S5_BS), F32),
                        pltpu.VMEM((tc // 8, 8, S5_BS), F32), pltpu.VMEM((tc // 8, 8, S5_BS), F32),
                        pltpu.VMEM((1, S5_BS), F32), pltpu.VMEM((1, S5_BS), F32)] + comm.scratch,
        name="s5_bwd", compiler_params=_params(("arbitrary", "arbitrary")),
    )(u, dz, st_re, st_im, lam_re, lam_im, bbd_re, bbd_im, cbd_re, cbd_im, d_skip, *comm.ins)


def _blockdiag_b(bt):
    eye = jnp.eye(S5_GB, dtype=bt.dtype)
    t = bt.reshape(S5_GROUP, S5_NB, S5_GB, S5_STATE)
    return jnp.einsum("ab,hnbp->nahbp", eye, t).reshape(S5_NB, S5_BC, S5_BS)


def _unblockdiag_b(m):
    eye = jnp.eye(S5_GB, dtype=m.dtype)
    t = m.reshape(S5_NB, S5_GB, S5_GROUP, S5_GB, S5_STATE)
    return jnp.einsum("ab,nahbp->hnbp", eye, t).reshape(S5_GROUP, S5_NSTATE)


def _blockdiag_c(c):
    eye = jnp.eye(S5_GB, dtype=c.dtype)
    t = c.reshape(S5_NB, S5_GB, S5_GROUP, S5_STATE)
    return jnp.einsum("ab,nbhp->napbh", eye, t).reshape(S5_NB, S5_BS, S5_BC)


def _unblockdiag_c(m):
    eye = jnp.eye(S5_GB, dtype=m.dtype)
    t = m.reshape(S5_NB, S5_GB, S5_STATE, S5_GB, S5_GROUP)
    return jnp.einsum("ab,napbh->nbhp", eye, t).reshape(S5_GROUPS, S5_GROUP, S5_STATE)


def _gla_gates(glr, wg2, bg, tri):
    pre = _dot(glr, wg2) + bg
    la = _logsig(pre) * (1.0 / GLA_TAU)
    gc = _dot_f32(tri, la)
    gend = gc[GLA_CHUNK - 1:GLA_CHUNK, :]
    e = jnp.exp(gend - gc)
    return pre, e, jnp.exp(gend)


def _gla_specs(nc, rev):
    ix = (lambda n: nc - 1 - n) if rev else (lambda n: n)
    c = GLA_CHUNK
    return dict(
        q=pl.BlockSpec((c, GLA_QK), lambda n: (ix(n), 0)),
        k=pl.BlockSpec((c, GLA_QK), lambda n: (ix(n), 1)),
        v=pl.BlockSpec((c, D_MODEL), lambda n: (ix(n), 1)),
        r=pl.BlockSpec((c, D_MODEL), lambda n: (ix(n), 2)),
        glr=pl.BlockSpec((c, GLA_RANK_PAD), lambda n: (ix(n), (2 * GLA_QK + 2 * D_MODEL) // GLA_RANK_PAD)),
        wg2=pl.BlockSpec((GLA_RANK_PAD, GLA_QK), lambda n: (0, 0)),
        bg=pl.BlockSpec((1, GLA_QK), lambda n: (0, 0)),
        gn=pl.BlockSpec((1, D_MODEL), lambda n: (0, 0)),
        tri=pl.BlockSpec((c, c), lambda n: (0, 0)),
        row=pl.BlockSpec((c, D_MODEL), lambda n: (ix(n), 0)),
        rowp=pl.BlockSpec((c, GLA_INP), lambda n: (ix(n), 0)),
        st=pl.BlockSpec((1, GLA_HEADS, GLA_DV, GLA_DK), lambda n: (ix(n), 0, 0, 0)),
    )


def _gla_fwd(proj, wg2, bg, gn, tri, comm):
    length = proj.shape[0]
    nc = length // GLA_CHUNK
    scale = GLA_DK ** -0.5
    nci, nco = len(comm.ins), len(comm.out_shapes)

    def body(*refs):
        q_ref, k_ref, v_ref, r_ref, glr_ref, wg2_ref, bg_ref, gn_ref, tri_ref = refs[:9]
        og_ref, sp_ref = refs[9 + nci:11 + nci]
        st = refs[11 + nci + nco]
        comm.run(pl.program_id(0), nc, refs[9:9 + nci], refs[11 + nci:11 + nci + nco], refs[12 + nci + nco:])

        @pl.when(pl.program_id(0) == 0)
        def _():
            st[...] = jnp.zeros_like(st)
        _, e, dec = _gla_gates(glr_ref[...], wg2_ref[...], bg_ref[...], tri_ref[...])
        kd = k_ref[...].astype(F32) * e
        q = q_ref[...].astype(F32) * scale
        for h in range(GLA_HEADS):
            sk = slice(h * GLA_DK, (h + 1) * GLA_DK)
            sv = slice(h * GLA_DV, (h + 1) * GLA_DV)
            sp_ref[0, h] = st[h]
            stn = dec[:, sk] * st[h] + _dot(v_ref[:, sv], kd[:, sk], _TN)
            st[h] = stn
            o = _dot(q[:, sk], stn, _NT)
            on = o * _rms(o)
            og_ref[:, sv] = (on * gn_ref[:, sv] * _silu(r_ref[:, sv].astype(F32))).astype(og_ref.dtype)

    s = _gla_specs(nc, False)
    return pl.pallas_call(
        body, grid=(nc,),
        in_specs=[s["q"], s["k"], s["v"], s["r"], s["glr"], s["wg2"], s["bg"], s["gn"], s["tri"]] + [_ANY_SPEC] * nci,
        out_specs=[s["row"], s["st"]] + [_ANY_SPEC] * nco,
        out_shape=[jax.ShapeDtypeStruct((length, D_MODEL), MXU_DT),
                   jax.ShapeDtypeStruct((nc, GLA_HEADS, GLA_DV, GLA_DK), F32)] + comm.out_shapes,
        scratch_shapes=[pltpu.VMEM((GLA_HEADS, GLA_DV, GLA_DK), F32)] + comm.scratch,
        name="gla_fwd", compiler_params=_params(("arbitrary",)),
    )(proj, proj, proj, proj, proj, wg2, bg, gn, tri, *comm.ins)


def _gla_bwd(proj, d_og, s_prev, wg2, bg, gn, tri):
    length = proj.shape[0]
    nc = length // GLA_CHUNK
    scale = GLA_DK ** -0.5

    def body(q_ref, k_ref, v_ref, r_ref, glr_ref, dog_ref, sp_ref, wg2_ref, bg_ref, gn_ref, tri_ref,
             dp_ref, dwg2_ref, dbg_ref, dgn_ref, dst):
        dq_ref = dp_ref.at[:, pl.ds(0, GLA_QK)]
        dk_ref = dp_ref.at[:, pl.ds(GLA_QK, GLA_QK)]
        dv_ref = dp_ref.at[:, pl.ds(2 * GLA_QK, D_MODEL)]
        dr_ref = dp_ref.at[:, pl.ds(2 * GLA_QK + D_MODEL, D_MODEL)]
        dglr_ref = dp_ref.at[:, pl.ds(2 * GLA_QK + 2 * D_MODEL, GLA_RANK_PAD)]

        @pl.when(pl.program_id(0) == 0)
        def _():
            dst[...] = jnp.zeros_like(dst)
            dwg2_ref[...] = jnp.zeros_like(dwg2_ref)
            dbg_ref[...] = jnp.zeros_like(dbg_ref)
            dgn_ref[...] = jnp.zeros_like(dgn_ref)
        glr = glr_ref[...]
        pre, e, dec = _gla_gates(glr, wg2_ref[...], bg_ref[...], tri_ref[...])
        k = k_ref[...].astype(F32)
        kd = k * e
        q = q_ref[...].astype(F32) * scale
        dkd_parts, ddec_parts = [], []
        for h in range(GLA_HEADS):
            sk = slice(h * GLA_DK, (h + 1) * GLA_DK)
            sv = slice(h * GLA_DV, (h + 1) * GLA_DV)
            stp = sp_ref[0, h]
            vh = v_ref[:, sv]
            stn = dec[:, sk] * stp + _dot(vh, kd[:, sk], _TN)
            o = _dot(q[:, sk], stn, _NT)
            rinv = _rms(o)
            on = o * rinv
            rv = r_ref[:, sv].astype(F32)
            sg = _sigmoid(rv)
            sr = rv * sg
            dog = dog_ref[:, sv].astype(F32)
            gnh = gn_ref[:, sv]
            d_ong = dog * sr
            dr_ref[:, sv] = (dog * (on * gnh) * (sg * (1.0 + rv * (1.0 - sg)))).astype(dr_ref.dtype)
            dgn_ref[:, sv] += jnp.sum(d_ong * on, axis=0, keepdims=True)
            d_on = d_ong * gnh
            do = rinv * (d_on - on * jnp.mean(d_on * on, axis=-1, keepdims=True))
            dq_ref[:, sk] = (_dot(do, stn) * scale).astype(dq_ref.dtype)
            dstn = dst[h] + _dot(do, q[:, sk], _TN)
            dst[h] = dec[:, sk] * dstn
            ddec_parts.append(jnp.sum(dstn * stp, axis=0, keepdims=True))
            dv_ref[:, sv] = _dot(kd[:, sk], dstn, _NT).astype(dv_ref.dtype)
            dkd_parts.append(_dot(vh, dstn))
        dkd = jnp.concatenate(dkd_parts, axis=1)
        ddec = jnp.concatenate(ddec_parts, axis=1)
        dk_ref[...] = (dkd * e).astype(dk_ref.dtype)
        w = dkd * kd
        dgend = jnp.sum(w, axis=0, keepdims=True) + ddec * dec
        dla = dgend - _dot_f32(tri_ref[...], w, _TN)
        dpre = dla * (1.0 - _sigmoid(pre)) * (1.0 / GLA_TAU)
        dwg2_ref[...] += _dot(glr, dpre, _TN)
        dbg_ref[...] += jnp.sum(dpre, axis=0, keepdims=True)
        dglr_ref[...] = _dot(dpre, wg2_ref[...], _NT).astype(dglr_ref.dtype)

    s = _gla_specs(nc, True)
    return pl.pallas_call(
        body, grid=(nc,),
        in_specs=[s["q"], s["k"], s["v"], s["r"], s["glr"], s["row"], s["st"], s["wg2"], s["bg"], s["gn"], s["tri"]],
        out_specs=[s["rowp"], s["wg2"], s["bg"], s["gn"]],
        out_shape=[jax.ShapeDtypeStruct((length, GLA_INP), MXU_DT),
                   jax.ShapeDtypeStruct((GLA_RANK_PAD, GLA_QK), F32), jax.ShapeDtypeStruct((1, GLA_QK), F32),
                   jax.ShapeDtypeStruct((1, D_MODEL), F32)],
        scratch_shapes=[pltpu.VMEM((GLA_HEADS, GLA_DV, GLA_DK), F32)],
        name="gla_bwd", compiler_params=_params(("arbitrary",)),
    )(proj, proj, proj, proj, proj, d_og, s_prev, wg2, bg, gn, tri)


def _local_step(x, tgt, mods, nrm, s5p, w, shards, core, *, row_tile=256, s5_tc=256):
    length = x.shape[0]
    tmm = 512
    glu_sh, gin_sh, gout_sh, ff1a_sh, ff1b_sh, ff2a_sh, ff2b_sh = shards
    w = dict(w)
    rc = functools.partial(_rowcall, tile=row_tile)
    (sh1a, sc1a, gt1a, sh2a, sc2a, gt2a), (sh1b, sc1b, gt1b, sh2b, sc2b, gt2b) = mods
    vec = (1, D_MODEL)
    row32, row16 = (D_MODEL, F32), (D_MODEL, MXU_DT)

    (h0,) = rc(lambda *a: _f_pn(*a)[1:], [_full(x)], [nrm["mix"][0], sc1a, sh1a], [row32], [], name="pn0")
    lam_re, lam_im, bb_re, bb_im = _s5_prep(s5p["a_re"], s5p["a_im"], s5p["log_dt"], s5p["bt_re"], s5p["bt_im"], s5p["e01"])
    bbd_re = _blockdiag_b(bb_re.reshape(S5_GROUP, S5_GROUPS, S5_STATE)).astype(MXU_DT)
    bbd_im = _blockdiag_b(bb_im.reshape(S5_GROUP, S5_GROUPS, S5_STATE)).astype(MXU_DT)
    cbd_re = _blockdiag_c(s5p["c_re"]).astype(MXU_DT)
    cbd_im = _blockdiag_c(s5p["c_im"]).astype(MXU_DT)
    z0, st_re, st_im, glu_s, ff1a_s, ff2a_s = _s5_fwd(
        h0, lam_re, lam_im, bbd_re, bbd_im, cbd_re, cbd_im, s5p["d"], _ag_comm([glu_sh, ff1a_sh, ff2a_sh]), tc=s5_tc)
    w["glu"] = glu_s.transpose(1, 0, 2).reshape(D_MODEL, 2 * D_MODEL)
    w["ff1"] = [ff1a_s.transpose(1, 0, 2).reshape(D_MODEL, D_FF), None]
    w["ff2"] = [ff2a_s.reshape(D_FF, D_MODEL), None]
    vg = _mm(z0, w["glu"], "nn", F32, tm=tmm, tn=2048, name="glu_mm")
    x1, h1 = rc(_f_glu_res_pn, [_full(x), (vg, D_MODEL, 0), (vg, D_MODEL, 1)], [gt1a, nrm["mlp"][0], sc2a, sh2a],
                [row32, row16], [], name="node1")
    relu = lambda acc: jnp.maximum(acc, 0.0)
    sq = lambda a: a * a
    a0, gin_s = _mm(h1, w["ff1"][0], "nn", MXU_DT, tm=tmm, tn=2048, name="ff1a", out_fn=relu, comm=_ag_comm([gin_sh]))
    f0, ff1b_s = _mm(a0, w["ff2"][0], "nn", F32, tm=tmm, tn=1024, name="ff2a", a_fn=sq, comm=_ag_comm([ff1b_sh]))
    gin_full = gin_s.transpose(1, 0, 2).reshape(D_MODEL, GLA_IN)
    q_, k_, v_, glr_, r_ = jnp.split(gin_full, [GLA_QK, 2 * GLA_QK, 2 * GLA_QK + D_MODEL, 2 * GLA_QK + D_MODEL + GLA_RANK], axis=1)
    w["gin"] = jnp.concatenate([q_, k_, v_, r_, glr_, jnp.zeros((D_MODEL, GLA_RANK_PAD - GLA_RANK), MXU_DT)], axis=1)
    w["ff1"][1] = ff1b_s.transpose(1, 0, 2).reshape(D_MODEL, D_FF)
    x2, h2 = rc(_f_res_pn, [_full(x1), _full(f0)], [gt2a, nrm["mix"][1], sc1b, sh1b], [row32, row16], [], name="node2")
    proj = _mm(h2, w["gin"], "nn", MXU_DT, tm=tmm, tn=GLA_INP, name="gla_in")
    og, s_prev, gout_s, ff2b_s = _gla_fwd(proj, w["wg2"], w["bg"], w["gn"], w["tri"], _ag_comm([gout_sh, ff2b_sh]))
    w["gout"] = gout_s.reshape(D_MODEL, D_MODEL)
    w["ff2"][1] = ff2b_s.reshape(D_FF, D_MODEL)
    y1 = _mm(og, w["gout"], "nn", F32, tm=tmm, tn=1024, name="gla_out")
    x3, h3 = rc(_f_res_pn, [_full(x2), _full(y1)], [gt1b, nrm["mlp"][1], sc2b, sh2b], [row32, row16], [], name="node3")
    a1 = _mm(h3, w["ff1"][1], "nn", MXU_DT, tm=tmm, tn=2048, name="ff1b", out_fn=relu)
    f1 = _mm(a1, w["ff2"][1], "nn", F32, tm=tmm, tn=1024, name="ff2b", a_fn=sq)

    g = {}
    dx, df, g["gt2b"], g["nf"], loss = rc(_g_final, [_full(x3), _full(f1), _full(tgt)], [gt2b, nrm["final"]],
                                          [row32, row16], [vec, vec, (8, 128)], name="final")

    def mlp_bwd(df, a, h, w1, w2, tag):
        dw2 = _mm(a, df, "tn", MXU_DT, tm=1024, tn=1024, name="dff2" + tag, a_fn=sq)
        du = _mm(df, w2, "nt", MXU_DT, tm=tmm, tn=2048, name="dact" + tag, extra=a, out_fn=lambda acc, e: acc * (2.0 * e))
        dw1 = _mm(h, du, "tn", MXU_DT, tm=1024, tn=D_FF // N_DEV, name="dff1" + tag, by_owner=True)
        dh = _mm(du, w1, "nt", F32, tm=tmm, tn=1024, name="dh" + tag)
        return dw1, dw2, dh

    def node_bwd(f, prim_rows, cots, pars, row_want, outs, name):
        nrow = len(prim_rows)
        return rc(_vjp_of(f, nrow, len(cots), row_want), prim_rows + cots, pars, outs, [vec] * len(pars), name=name)

    g["ff1b"], g["ff2b"], dh3 = mlp_bwd(df, a1, h3, w["ff1"][1], w["ff2"][1], "b")
    dx, dy1, g["gt1b"], g["mlp1"], g["sc2b"], g["sh2b"] = node_bwd(
        _f_res_pn, [_full(x2), _full(y1)], [_full(dx), _full(dh3)], [gt1b, nrm["mlp"][1], sc2b, sh2b], (0, 1),
        [row32, row16], "node3_bwd")
    g["gout"] = _mm(og, dy1, "tn", MXU_DT, tm=512, tn=1024, name="dgout")
    d_og = _mm(dy1, w["gout"], "nt", MXU_DT, tm=tmm, tn=1024, name="dog")
    dproj, g["wg2"], g["bg"], g["gn"] = _gla_bwd(proj, d_og, s_prev, w["wg2"], w["bg"], w["gn"], w["tri"])
    g["gin"] = _mm(h2, dproj, "tn", MXU_DT, tm=512, tn=640, name="dgin")
    dh2 = _mm(dproj, w["gin"], "nt", F32, tm=tmm, tn=1024, name="dh2")
    dx, df0, g["gt2a"], g["mix1"], g["sc1b"], g["sh1b"] = node_bwd(
        _f_res_pn, [_full(x1), _full(f0)], [_full(dx), _full(dh2)], [gt2a, nrm["mix"][1], sc1b, sh1b], (0, 1),
        [row32, row16], "node2_bwd")
    g["ff1a"], g["ff2a"], dh1 = mlp_bwd(df0, a0, h1, w["ff1"][0], w["ff2"][0], "a")
    glu_vjp = _vjp_of(_f_glu_res_pn, 3, 2, (0, 1, 2))

    def glu_bwd(*a):
        r = glu_vjp(*a)
        return (r[0], jnp.concatenate([r[1], r[2]], axis=1)) + r[3:]

    dx, dvg, g["gt1a"], g["mlp0"], g["sc2a"], g["sh2a"] = rc(
        glu_bwd, [_full(x), (vg, D_MODEL, 0), (vg, D_MODEL, 1), _full(dx), _full(dh1)], [gt1a, nrm["mlp"][0], sc2a, sh2a],
        [row32, (2 * D_MODEL, MXU_DT)], [vec] * 4, name="node1_bwd")
    g["glu"] = _mm(z0, dvg, "tn", MXU_DT, tm=1024, tn=2 * D_MODEL // N_DEV, name="dglu", by_owner=True)
    dz0 = _mm(dvg, w["glu"], "nt", MXU_DT, tm=tmm, tn=1024, name="dz0")
    gin_g = g.pop("gin")
    gin_g = jnp.concatenate([gin_g[:, :2 * GLA_QK + D_MODEL], gin_g[:, GLA_INP - GLA_RANK_PAD:GLA_INP - GLA_RANK_PAD + GLA_RANK],
                             gin_g[:, 2 * GLA_QK + D_MODEL:2 * GLA_QK + 2 * D_MODEL]], axis=1)
    per_owner = [g.pop("glu"), gin_g.reshape(D_MODEL, N_DEV, GLA_IN // N_DEV).transpose(1, 0, 2),
                 g.pop("gout").reshape(N_DEV, D_MODEL // N_DEV, D_MODEL), g.pop("ff1a"), g.pop("ff1b"),
                 g.pop("ff2a").reshape(N_DEV, D_FF // N_DEV, D_MODEL), g.pop("ff2b").reshape(N_DEV, D_FF // N_DEV, D_MODEL)]
    from_sibling = _swap_sibling(per_owner)
    chip_sum = [_pair_add(a, b, core, name="rs_add%d" % i) for i, (a, b) in enumerate(zip(per_owner, from_sibling))]
    res = _s5_bwd(h0, dz0, st_re, st_im, lam_re, lam_im, bbd_re, bbd_im, cbd_re, cbd_im, s5p["d"], _chips_comm(chip_sum), tc=s5_tc)
    du0, dbbd_re, dbbd_im, dcbd_re, dcbd_im, dlam_re, dlam_im, g["s5_d"] = res[:8]
    from_chips = res[8:]
    g["s5_c_re"] = _unblockdiag_c(dcbd_re)
    g["s5_c_im"] = _unblockdiag_c(dcbd_im)
    g["s5_a_re"], g["s5_a_im"], g["s5_log_dt"], g["s5_bt_re"], g["s5_bt_im"] = _s5_prep_bwd(
        s5p["a_re"], s5p["a_im"], s5p["log_dt"], s5p["bt_re"], s5p["bt_im"], s5p["e01"],
        dlam_re, dlam_im, _unblockdiag_b(dbbd_re), _unblockdiag_b(dbbd_im))
    grad_x, g["mix0"], g["sc1a"], g["sh1a"] = node_bwd(
        _f_pn, [_full(x)], [_full(dx), _full(du0)], [nrm["mix"][0], sc1a, sh1a], (0,), [row32], "node0_bwd")
    return loss[0, 0], grad_x, g, chip_sum, from_chips


_MESH = pl.DeviceIdType.MESH
_VMEM_SPEC = pl.BlockSpec(memory_space=pltpu.VMEM)
_ANY_SPEC = pl.BlockSpec(memory_space=pl.ANY)


def _my_place():
    ix, iy, ic = lax.axis_index("x"), lax.axis_index("y"), lax.axis_index("c")
    return ix, iy, ic


def _exchange(x, *, gather, name):
    r = x.shape[-2]

    def body(x_ref, o_ref, ssem, rsem):
        ix, iy, ic = _my_place()
        me = 4 * ix + 2 * iy + ic
        if gather:
            o_ref[me] = x_ref[...]
        else:
            o_ref[me] = x_ref[me]
        copies = []
        for k in range(1, N_DEV):
            tx, ty, tc = ix ^ (k >> 2), iy ^ ((k >> 1) & 1), ic ^ (k & 1)
            src = x_ref if gather else x_ref.at[4 * tx + 2 * ty + tc]
            cp = pltpu.make_async_remote_copy(src_ref=src, dst_ref=o_ref.at[me], send_sem=ssem.at[k - 1],
                                              recv_sem=rsem.at[k - 1], device_id=(tx, ty, tc), device_id_type=_MESH)
            cp.start()
            copies.append(cp)
        for cp in copies:
            cp.wait()

    return pl.pallas_call(
        body, out_shape=jax.ShapeDtypeStruct((N_DEV, r, 128), x.dtype), in_specs=[_VMEM_SPEC], out_specs=_VMEM_SPEC,
        scratch_shapes=[pltpu.SemaphoreType.DMA((N_DEV - 1,)), pltpu.SemaphoreType.DMA((N_DEV - 1,))], name=name,
    )(x)


class _Comm:
    def __init__(self, ins, out_shapes, scratch, phases):
        self.ins, self.out_shapes, self.scratch, self.phases = list(ins), list(out_shapes), list(scratch), phases

    def run(self, step, n_steps, in_refs, out_refs, scratch_refs):
        when = {"first": 0, "mid": (3 * n_steps) // 8, "last": n_steps - 1}
        for phase, fn in self.phases:
            pl.when(step == when[phase])(functools.partial(fn, in_refs, out_refs, scratch_refs))


def _ag_comm(xs):
    n = len(xs)

    def parts():
        ix, iy, ic = _my_place()
        return ic, (ix, iy, ic), (ix, iy, 1 - ic), [(1 - ix, iy), (ix, 1 - iy), (1 - ix, 1 - iy)]

    def copy(ins, outs, sc, a, k, block, to, from_x=False):
        px, py, pc = block
        slot = outs[a].at[4 * px + 2 * py + pc]
        return pltpu.make_async_remote_copy(
            src_ref=ins[a] if from_x else slot, dst_ref=slot, send_sem=sc[0].at[7 * a + k], recv_sem=sc[1].at[7 * a + k],
            device_id=to, device_id_type=_MESH)

    def local(ins, outs, sc, a, me):
        return pltpu.make_async_copy(ins[a], outs[a].at[4 * me[0] + 2 * me[1] + me[2]], sc[2].at[a])

    def start(ins, outs, sc):
        ic, me, sibling, chips = parts()
        for a in range(n):
            local(ins, outs, sc, a, me).start()
            copy(ins, outs, sc, a, 0, me, sibling, True).start()
            for j, chip in enumerate(chips):
                copy(ins, outs, sc, a, 1 + j, me, (*chip, ic), True).start()

    def forward(ins, outs, sc):
        ic, me, sibling, chips = parts()
        for a in range(n):
            for j, chip in enumerate(chips):
                copy(ins, outs, sc, a, 1 + j, (*chip, ic), me).wait_recv()
                copy(ins, outs, sc, a, 4 + j, (*chip, ic), sibling).start()

    def finish(ins, outs, sc):
        ic, me, sibling, chips = parts()
        for a in range(n):
            copy(ins, outs, sc, a, 0, sibling, me).wait_recv()
            for j, chip in enumerate(chips):
                copy(ins, outs, sc, a, 4 + j, (*chip, 1 - ic), me).wait_recv()
        for a in range(n):
            copy(ins, outs, sc, a, 0, me, sibling, True).wait_send()
            for j, chip in enumerate(chips):
                copy(ins, outs, sc, a, 1 + j, me, (*chip, ic), True).wait_send()
                copy(ins, outs, sc, a, 4 + j, (*chip, ic), sibling).wait_send()
            local(ins, outs, sc, a, me).wait()

    return _Comm(xs, [jax.ShapeDtypeStruct((N_DEV,) + x.shape, x.dtype) for x in xs],
                 [pltpu.SemaphoreType.DMA((7 * n,)), pltpu.SemaphoreType.DMA((7 * n,)), pltpu.SemaphoreType.DMA((n,))],
                 [("first", start), ("mid", forward), ("last", finish)])


def _chips_comm(ps):
    n = len(ps)

    def copies(ins, outs, sc):
        ix, iy, ic = _my_place()
        out = []
        for a in range(n):
            for k in range(1, 4):
                tx, ty = ix ^ (k >> 1), iy ^ (k & 1)
                out.append(pltpu.make_async_remote_copy(
                    src_ref=ins[a].at[2 * tx + ty], dst_ref=outs[a].at[k - 1], send_sem=sc[0].at[3 * a + k - 1],
                    recv_sem=sc[1].at[3 * a + k - 1], device_id=(tx, ty, ic), device_id_type=_MESH))
        return out

    def start(ins, outs, sc):
        for cp in copies(ins, outs, sc):
            cp.start()

    def finish(ins, outs, sc):
        for cp in copies(ins, outs, sc):
            cp.wait()

    return _Comm(ps, [jax.ShapeDtypeStruct((3,) + p.shape[1:], p.dtype) for p in ps],
                 [pltpu.SemaphoreType.DMA((3 * n,)), pltpu.SemaphoreType.DMA((3 * n,))], [("first", start), ("last", finish)])


def _swap_sibling(gs):
    n = len(gs)

    def body(*refs):
        g_refs, o_refs = refs[:n], refs[n:2 * n]
        ssem, rsem = refs[2 * n:]
        ix, iy, ic = _my_place()
        copies = []
        for a in range(n):
            for q in range(4):
                cp = pltpu.make_async_remote_copy(src_ref=g_refs[a].at[2 * q + 1 - ic], dst_ref=o_refs[a].at[q],
                                                  send_sem=ssem.at[4 * a + q], recv_sem=rsem.at[4 * a + q],
                                                  device_id=(ix, iy, 1 - ic), device_id_type=_MESH)
                cp.start()
                copies.append(cp)
        for cp in copies:
            cp.wait()

    return pl.pallas_call(
        body, out_shape=[jax.ShapeDtypeStruct((4,) + g.shape[1:], g.dtype) for g in gs], in_specs=[_ANY_SPEC] * n,
        out_specs=[_ANY_SPEC] * n,
        scratch_shapes=[pltpu.SemaphoreType.DMA((4 * n,)), pltpu.SemaphoreType.DMA((4 * n,))], name="rs_sibling",
    )(*gs)


def _ada_fwd(c_all, w_ada, b_cols):
    def body(c_ref, w_ref, b_ref, o_ref):
        cs = _silu(c_ref[...])
        for i in range(2):
            o_ref[i] = _dot(cs, w_ref[i]) + b_ref[pl.ds(i, 1), :]
    return pl.pallas_call(body, out_shape=jax.ShapeDtypeStruct((2, N_DEV, w_ada.shape[2]), F32), name="ada_fwd",
                          compiler_params=pltpu.CompilerParams(vmem_limit_bytes=VMEM_LIMIT))(c_all, w_ada, b_cols)


def _ada_bwd(c_all, dm):
    def body(c_ref, d_ref, o_ref):
        cs = _silu(c_ref[...])
        for i in range(2):
            o_ref[i] = _dot(cs, d_ref[i], _TN)
    return pl.pallas_call(body, out_shape=jax.ShapeDtypeStruct((2, D_MODEL, dm.shape[2]), F32), name="ada_bwd",
                          compiler_params=pltpu.CompilerParams(vmem_limit_bytes=VMEM_LIMIT))(c_all, dm)


def _pair_add(g, recv, core, *, name):
    _, r, cdim = g.shape

    def body(core_ref, g_ref, r_ref, o_ref):
        o_ref[...] = (g_ref[...].astype(F32) + r_ref[...].astype(F32)).astype(o_ref.dtype)

    grid_spec = pltpu.PrefetchScalarGridSpec(
        num_scalar_prefetch=1, grid=(4,),
        in_specs=[pl.BlockSpec((1, r, cdim), lambda q, core_ref: (2 * q + core_ref[0], 0, 0)),
                  pl.BlockSpec((1, r, cdim), lambda q, core_ref: (q, 0, 0))],
        out_specs=pl.BlockSpec((1, r, cdim), lambda q, core_ref: (q, 0, 0)))
    return pl.pallas_call(body, grid_spec=grid_spec, out_shape=jax.ShapeDtypeStruct((4, r, cdim), g.dtype), name=name,
                          compiler_params=_params(("parallel",)))(core, g, recv)


def _sum_slots(x, *, name):
    def body(x_ref, o_ref):
        acc = x_ref[0]
        for s in range(1, N_DEV):
            acc = acc + x_ref[s]
        o_ref[...] = acc
    return pl.pallas_call(body, out_shape=jax.ShapeDtypeStruct(x.shape[1:], F32), name=name)(x)


def _adamw_math(w, m, v, g):
    mn = ADAM_B1 * m + (1.0 - ADAM_B1) * g
    vn = ADAM_B2 * v + (1.0 - ADAM_B2) * (g * g)
    m_hat = mn / (1.0 - ADAM_B1 ** ADAM_STEP)
    v_hat = vn / (1.0 - ADAM_B2 ** ADAM_STEP)
    return -ADAM_LR * (m_hat / (jnp.sqrt(v_hat) + ADAM_EPS) + ADAM_WD * w), mn, vn


def _adamw_multi(ws, ms, vs, gs, *, name):
    n = len(ws)

    def body(*refs):
        for i in range(n):
            g = refs[3 * n + i][...]
            o = refs[4 * n + 4 * i:4 * n + 4 * i + 4]
            o[0][...] = g
            o[1][...], o[2][...], o[3][...] = _adamw_math(refs[i][...], refs[n + i][...], refs[2 * n + i][...], g)

    out_shape = [jax.ShapeDtypeStruct(w.shape, F32) for w in ws for _ in range(4)]
    res = pl.pallas_call(body, out_shape=out_shape, name=name,
                         compiler_params=pltpu.CompilerParams(vmem_limit_bytes=VMEM_LIMIT))(*ws, *ms, *vs, *gs)
    return [res[4 * i:4 * i + 4] for i in range(n)]


def _adamw(w, m, v, gparts, *, tile, name, sel=None):
    r, cdim = w.shape
    ng = len(gparts)
    sel = jnp.zeros((1,), jnp.int32) if sel is None else sel

    def body(*refs):
        w_ref, m_ref, v_ref = refs[1:4]
        g = None
        for p, part in zip(refs[4:4 + ng], gparts):
            pv = (p[0] if isinstance(part, tuple) else p[...]).astype(F32)
            g = pv if g is None else g + pv
        g_ref, d_ref, nm_ref, nv_ref = refs[4 + ng:]
        g_ref[...] = g
        d_ref[...], nm_ref[...], nv_ref[...] = _adamw_math(w_ref[...], m_ref[...], v_ref[...], g)

    spec = pl.BlockSpec((tile, cdim), lambda i, s: (i, 0))

    def part_spec(part):
        if not isinstance(part, tuple):
            return spec
        slab = part[1]
        if slab is None:
            return pl.BlockSpec((1, tile, cdim), lambda i, s: (s[0], i, 0))
        return pl.BlockSpec((1, tile, cdim), lambda i, s: (slab, i, 0))

    grid_spec = pltpu.PrefetchScalarGridSpec(
        num_scalar_prefetch=1, grid=(r // tile,), in_specs=[spec] * 3 + [part_spec(p) for p in gparts], out_specs=[spec] * 4)
    return pl.pallas_call(
        body, grid_spec=grid_spec, out_shape=[jax.ShapeDtypeStruct(w.shape, F32)] * 4, name=name,
        compiler_params=_params(("parallel",)),
    )(sel, w, m, v, *[p[0] if isinstance(p, tuple) else p for p in gparts])


_REP_ROWS = 272
_REP_SIZE = 2 * 1024 * 2 + 4096 * 2 + 64 + 65536 * 4 + 1024 + 1024


def _pad_rows(v, rows):
    return jnp.pad(v.reshape(-1), (0, rows * 128 - v.size)).reshape(rows, 128)


def kernel(x, c, w_ada, b_ada, norm_mix, norm_mlp, s5_a_re, s5_a_im, s5_log_dt, s5_b_re, s5_b_im, s5_c_re, s5_c_im, s5_d, s5_w_glu, gla_w_in, gla_w_gate2, gla_b_gate, gla_g_norm, gla_w_out, w_ff1, w_ff2, norm_final, loss_target, m_w_ada, m_b_ada, m_norm_mix, m_norm_mlp, m_s5_a_re, m_s5_a_im, m_s5_log_dt, m_s5_b_re, m_s5_b_im, m_s5_c_re, m_s5_c_im, m_s5_d, m_s5_w_glu, m_gla_w_in, m_gla_w_gate2, m_gla_b_gate, m_gla_g_norm, m_gla_w_out, m_w_ff1, m_w_ff2, m_norm_final, v_w_ada, v_b_ada, v_norm_mix, v_norm_mlp, v_s5_a_re, v_s5_a_im, v_s5_log_dt, v_s5_b_re, v_s5_b_im, v_s5_c_re, v_s5_c_im, v_s5_d, v_s5_w_glu, v_gla_w_in, v_gla_w_gate2, v_gla_b_gate, v_gla_g_norm, v_gla_w_out, v_w_ff1, v_w_ff2, v_norm_final):
    ix, iy, ic = _my_place()
    me = 4 * ix + 2 * iy + ic
    ada_w = w_ada.shape[2]

    msg = jnp.concatenate([c.reshape(8, 128), gla_w_gate2[0].reshape(8, 128), _pad_rows(gla_b_gate, 1),
                           gla_g_norm.reshape(1, 128), jnp.zeros((6, 128), F32)])
    got = _exchange(msg, gather=True, name="gather_small")
    c_all = got[:, 0:8].reshape(N_DEV, D_MODEL)
    wg2 = got[:, 8:16].reshape(N_DEV, GLA_RANK, 64).transpose(1, 0, 2).reshape(GLA_RANK, GLA_QK)
    bg = got[:, 16, :64].reshape(1, GLA_QK)
    gn = got[:, 17, :].reshape(1, D_MODEL)

    b_cols = lax.dynamic_slice_in_dim(b_ada, me * ada_w, ada_w, axis=1)
    mod_cols = _ada_fwd(c_all, w_ada, b_cols)
    pay = jnp.pad(mod_cols.transpose(1, 0, 2).reshape(N_DEV, 12, 128), ((0, 0), (0, 4), (0, 0)))
    mod = _exchange(pay, gather=False, name="a2a_mod")[:, :12].reshape(N_DEV, 2, ada_w).transpose(1, 0, 2).reshape(2, 6 * D_MODEL)
    mods = [[mod[i:i + 1, j * D_MODEL:(j + 1) * D_MODEL] for j in range(6)] for i in range(2)]

    big_w = [s5_w_glu[0], gla_w_in[0], gla_w_out[0], w_ff1[0], w_ff1[1], w_ff2[0], w_ff2[1]]
    big_m = [m_s5_w_glu[0], m_gla_w_in[0], m_gla_w_out[0], m_w_ff1[0], m_w_ff1[1], m_w_ff2[0], m_w_ff2[1]]
    big_v = [v_s5_w_glu[0], v_gla_w_in[0], v_gla_w_out[0], v_w_ff1[0], v_w_ff1[1], v_w_ff2[0], v_w_ff2[1]]
    w = dict(wg2=jnp.pad(wg2, ((0, GLA_RANK_PAD - GLA_RANK), (0, 0))), bg=bg, gn=gn, tri=jnp.tril(jnp.ones((GLA_CHUNK, GLA_CHUNK), F32)))
    core = ic.reshape(1).astype(jnp.int32)
    chip = (2 * ix + iy).reshape(1).astype(jnp.int32)
    nrm = dict(mix=[norm_mix[i:i + 1] for i in range(2)], mlp=[norm_mlp[i:i + 1] for i in range(2)], final=norm_final.reshape(1, D_MODEL))
    e01 = (lax.broadcasted_iota(jnp.int32, (S5_GROUPS, S5_NSTATE), 1) // S5_STATE
           == lax.broadcasted_iota(jnp.int32, (S5_GROUPS, S5_NSTATE), 0)).astype(F32)
    s5p = dict(a_re=s5_a_re.reshape(1, S5_NSTATE), a_im=s5_a_im.reshape(1, S5_NSTATE), log_dt=s5_log_dt,
               bt_re=s5_b_re[0].transpose(2, 0, 1).reshape(S5_GROUP, S5_NSTATE),
               bt_im=s5_b_im[0].transpose(2, 0, 1).reshape(S5_GROUP, S5_NSTATE),
               c_re=s5_c_re[0], c_im=s5_c_im[0], d=s5_d, e01=e01)

    loss_local, grad_x, g, chip_sum, from_chips = _local_step(
        x[0], loss_target[0], mods, nrm, s5p, w, [a.astype(MXU_DT) for a in big_w], core)
    loss = lax.psum(loss_local, ("x", "y", "c"))
    big = []
    for i in range(len(big_w)):
        parts = [(chip_sum[i], None), (from_chips[i], 0), (from_chips[i], 1), (from_chips[i], 2)]
        big.append(_adamw(big_w[i], big_m[i], big_v[i], parts, tile=min(512, big_w[i].shape[0]), name="adamw_big%d" % i, sel=chip))

    rep = [jnp.concatenate([g["mix0"], g["mix1"]]), jnp.concatenate([g["mlp0"], g["mlp1"]]), g["s5_a_re"], g["s5_a_im"], g["s5_log_dt"],
           g["s5_bt_re"].reshape(S5_GROUP, S5_GROUPS, S5_STATE).transpose(1, 2, 0), g["s5_bt_im"].reshape(S5_GROUP, S5_GROUPS, S5_STATE).transpose(1, 2, 0),
           g["s5_c_re"], g["s5_c_im"], g["s5_d"], g["nf"]]
    rep_shapes = [(2, D_MODEL), (2, D_MODEL), (1, 64, 64), (1, 64, 64), (1, 64), (1, 64, 64, 16), (1, 64, 64, 16), (1, 64, 16, 64), (1, 64, 16, 64), (1, D_MODEL), (D_MODEL,)]
    rep_flat = jnp.concatenate([a.reshape(-1) for a in rep])
    rep_blk = jnp.pad(rep_flat, (0, N_DEV * _REP_ROWS * 128 - _REP_SIZE)).reshape(N_DEV, _REP_ROWS, 128)
    dmod = jnp.stack([jnp.concatenate([g["sh1" + t], g["sc1" + t], g["gt1" + t], g["sh2" + t], g["sc2" + t], g["gt2" + t]], axis=1)[0] for t in "ab"])
    msg = jnp.concatenate([
        rep_blk,
        g["wg2"][:GLA_RANK].reshape(GLA_RANK, N_DEV, 64).transpose(1, 0, 2).reshape(N_DEV, 8, 128),
        jnp.pad(g["bg"].reshape(N_DEV, 1, 64), ((0, 0), (0, 0), (0, 64))),
        g["gn"].reshape(N_DEV, 1, 128),
        dmod.reshape(2, N_DEV, ada_w).transpose(1, 0, 2).reshape(N_DEV, 12, 128),
        jnp.zeros((N_DEV, 2, 128), F32),
    ], axis=1)
    got = _exchange(msg, gather=False, name="a2a_small_grads")
    tot = _sum_slots(got, name="sum_small_grads")
    dm = got[:, 282:294].reshape(N_DEV, 2, ada_w).transpose(1, 0, 2)
    g_w_ada = _ada_bwd(c_all, dm)
    back = _exchange(jnp.concatenate([tot[0:_REP_ROWS], tot[282:294], jnp.zeros((4, 128), F32)]), gather=True, name="gather_small_grads")
    rep_sum = back[:, :_REP_ROWS].reshape(-1)[:_REP_SIZE]
    g_b_ada = back[:, _REP_ROWS:_REP_ROWS + 12].reshape(N_DEV, 2, ada_w).transpose(1, 0, 2).reshape(2, 6 * D_MODEL)
    g_rep, off = [], 0
    for s in rep_shapes:
        n = math.prod(s)
        g_rep.append(rep_sum[off:off + n].reshape(s))
        off += n
    g_small = g_rep + [g_b_ada, tot[272:280].reshape(GLA_RANK, 64)[None], tot[280, :64][None], tot[281][None]]
    p_small = [norm_mix, norm_mlp, s5_a_re, s5_a_im, s5_log_dt, s5_b_re, s5_b_im, s5_c_re, s5_c_im, s5_d, norm_final, b_ada, gla_w_gate2, gla_b_gate, gla_g_norm]
    m_small = [m_norm_mix, m_norm_mlp, m_s5_a_re, m_s5_a_im, m_s5_log_dt, m_s5_b_re, m_s5_b_im, m_s5_c_re, m_s5_c_im, m_s5_d, m_norm_final, m_b_ada, m_gla_w_gate2, m_gla_b_gate, m_gla_g_norm]
    v_small = [v_norm_mix, v_norm_mlp, v_s5_a_re, v_s5_a_im, v_s5_log_dt, v_s5_b_re, v_s5_b_im, v_s5_c_re, v_s5_c_im, v_s5_d, v_norm_final, v_b_ada, v_gla_w_gate2, v_gla_b_gate, v_gla_g_norm]
    as2d = lambda a: a.reshape(1, -1) if a.ndim == 1 else a
    small = _adamw_multi([as2d(a) for a in p_small], [as2d(a) for a in m_small], [as2d(a) for a in v_small],
                         [as2d(a) for a in g_small], name="adamw_small")
    small = [[o.reshape(p.shape) for o in outs] for outs, p in zip(small, p_small)]
    ada = _adamw(w_ada.reshape(2 * D_MODEL, ada_w), m_w_ada.reshape(2 * D_MODEL, ada_w), v_w_ada.reshape(2 * D_MODEL, ada_w),
                 [g_w_ada.reshape(2 * D_MODEL, ada_w)], tile=512, name="adamw_ada")
    ada = [a.reshape(w_ada.shape) for a in ada]

    def leaves(k):
        nm, nl, a_re, a_im, ldt, b_re, b_im, c_re, c_im, dsk, nf, bada, wg2_, bg_, gn_ = [s[k] for s in small]
        glu_, gin_, gout_, ff1a_, ff1b_, ff2a_, ff2b_ = [b[k] for b in big]
        return [ada[k], bada, nm, nl, a_re, a_im, ldt, b_re, b_im, c_re, c_im, dsk, glu_[None], gin_[None], wg2_, bg_, gn_, gout_[None],
                jnp.stack([ff1a_, ff1b_]), jnp.stack([ff2a_, ff2b_]), nf]

    return (loss, grad_x[None], *leaves(0), *leaves(1), *leaves(2), *leaves(3))
```

```python
import functools
import math

import jax
import jax.numpy as jnp
from jax import lax
from jax.experimental import pallas as pl
from jax.experimental.pallas import tpu as pltpu

F32 = jnp.float32
BF16 = jnp.bfloat16
MXU_DT = BF16
EPS = 1e-6
N_DEV = 8
VMEM_LIMIT = 56 * 1024 * 1024

D_MODEL = 1024
S5_GROUP = 16
S5_GROUPS = 64
S5_STATE = 64
S5_NSTATE = S5_GROUPS * S5_STATE
S5_GB = 16
S5_NB = S5_GROUPS // S5_GB
S5_BC = S5_GB * S5_GROUP
S5_BS = S5_GB * S5_STATE
GLA_HEADS = 4
GLA_QK = 512
GLA_DK = 128
GLA_DV = 256
GLA_RANK = 16
GLA_RANK_PAD = 128
GLA_TAU = 16.0
GLA_CHUNK = 64
GLA_IN = 3088
GLA_INP = 2 * GLA_QK + 2 * D_MODEL + GLA_RANK_PAD
D_FF = 4096

ADAM_LR = 0.001
ADAM_B1 = 0.9
ADAM_B2 = 0.999
ADAM_EPS = 1e-08
ADAM_WD = 0.01
ADAM_STEP = 10

_NN = (((1,), (0,)), ((), ()))
_NT = (((1,), (1,)), ((), ()))
_TN = (((0,), (0,)), ((), ()))


def _dot(a, b, dn=_NN):
    return lax.dot_general(a.astype(MXU_DT), b.astype(MXU_DT), dn, preferred_element_type=F32)


def _dot_exact01(x, m01, dn=_NN):
    x1 = x.astype(BF16)
    r1 = x - x1.astype(F32)
    x2 = r1.astype(BF16)
    x3 = (r1 - x2.astype(F32)).astype(BF16)
    m = m01.astype(BF16)
    d = lambda u: lax.dot_general(u, m, dn, preferred_element_type=F32)
    return d(x1) + d(x2) + d(x3)


def _dot_f32(a, b, dn=_NN):
    def split(x):
        x1 = x.astype(BF16)
        r1 = x - x1.astype(F32)
        x2 = r1.astype(BF16)
        x3 = (r1 - x2.astype(F32)).astype(BF16)
        return x1, x2, x3
    a1, a2, a3 = split(a)
    b1, b2, b3 = split(b)
    d = lambda u, v: lax.dot_general(u, v, dn, preferred_element_type=F32)
    return (d(a1, b1) + (d(a1, b2) + d(a2, b1)) + (d(a2, b2) + d(a1, b3) + d(a3, b1)))


def _sigmoid(x):
    return 1.0 / (1.0 + jnp.exp(-x))


def _silu(x):
    return x * _sigmoid(x)


def _gelu(x):
    return 0.5 * x * (1.0 + jnp.tanh(math.sqrt(2.0 / math.pi) * (x + 0.044715 * (x * x * x))))


def _logsig(x):
    return jnp.minimum(x, 0.0) - jnp.log(1.0 + jnp.exp(-jnp.abs(x)))


def _rms(x):
    return lax.rsqrt(jnp.mean(x * x, axis=-1, keepdims=True) + EPS)


def _params(sem):
    return pltpu.CompilerParams(dimension_semantics=sem, vmem_limit_bytes=VMEM_LIMIT)


def _mm(a, b, dims, out_dtype, *, tm, tn, name, a_fn=None, out_fn=None, extra=None, by_owner=False, comm=None):
    if dims == "tn":
        k, m = a.shape
        n = b.shape[1]
    else:
        m, k = a.shape
        n = b.shape[0] if dims == "nt" else b.shape[1]
    tm, tn = min(tm, m), min(tn, n)
    assert m % tm == 0 and n % tn == 0, (name, m, n, tm, tn)
    dn = {"nn": _NN, "nt": _NT, "tn": _TN}[dims]
    n_in = 2 if extra is None else 3
    nci, nco = (len(comm.ins), len(comm.out_shapes)) if comm is not None else (0, 0)

    def body(*refs):
        a_ref, b_ref = refs[0], refs[1]
        o_ref = refs[n_in + nci]
        if comm is not None:
            run_comm = functools.partial(
                comm.run, pl.program_id(0) * (n // tn) + pl.program_id(1), (m // tm) * (n // tn), refs[n_in:n_in + nci],
                refs[n_in + nci + 1:n_in + nci + 1 + nco], refs[n_in + nci + 1 + nco:])
            run_comm(("first", "mid"))
        av = a_ref[...]
        if a_fn is not None:
            av = a_fn(av.astype(F32))
        acc = _dot(av, b_ref[...], dn)
        if extra is not None:
            acc = out_fn(acc, refs[2][...].astype(F32))
        elif out_fn is not None:
            acc = out_fn(acc)
        if by_owner:
            o_ref[0] = acc.astype(o_ref.dtype)
        else:
            o_ref[...] = acc.astype(o_ref.dtype)
        if comm is not None:
            run_comm(("last",))

    a_spec = pl.BlockSpec((k, tm), lambda i, j: (0, i)) if dims == "tn" else pl.BlockSpec((tm, k), lambda i, j: (i, 0))
    b_spec = pl.BlockSpec((tn, k), lambda i, j: (j, 0)) if dims == "nt" else pl.BlockSpec((k, tn), lambda i, j: (0, j))
    if by_owner:
        o_spec = pl.BlockSpec((1, tm, tn), lambda i, j: (j, i, 0))
        out_shape = jax.ShapeDtypeStruct((n // tn, m, tn), out_dtype)
    else:
        o_spec = pl.BlockSpec((tm, tn), lambda i, j: (i, j))
        out_shape = jax.ShapeDtypeStruct((m, n), out_dtype)
    in_specs, args = [a_spec, b_spec], [a, b]
    if extra is not None:
        in_specs.append(o_spec)
        args.append(extra)
    if comm is None:
        return pl.pallas_call(
            body, grid=(m // tm, n // tn), in_specs=in_specs, out_specs=o_spec, out_shape=out_shape, name=name,
            compiler_params=_params(("parallel", "parallel")),
        )(*args)
    return pl.pallas_call(
        body, grid=(m // tm, n // tn), in_specs=in_specs + [_ANY_SPEC] * nci, out_specs=[o_spec] + [_ANY_SPEC] * nco,
        out_shape=[out_shape] + comm.out_shapes, scratch_shapes=comm.scratch, name=name,
        compiler_params=_params(("arbitrary", "arbitrary")),
    )(*args, *comm.ins)


def _rowcall(f, rows, pars, outs, accs, *, tile, name):
    length = rows[0][0].shape[0]
    tile = min(tile, length)
    nr, npar, no = len(rows), len(pars), len(outs)

    def body(*refs):
        vals = [r[...].astype(F32) for r in refs[:nr + npar]]
        res = f(*vals)
        o_refs = refs[nr + npar:nr + npar + no]
        a_refs = refs[nr + npar + no:]
        for o, v in zip(o_refs, res[:no]):
            o[...] = v.astype(o.dtype)
        if a_refs:
            @pl.when(pl.program_id(0) == 0)
            def _():
                for a in a_refs:
                    a[...] = jnp.zeros(a.shape, F32)
            for a, v in zip(a_refs, res[no:]):
                a[...] += jnp.broadcast_to(v, a.shape)

    in_specs = [pl.BlockSpec((tile, w), lambda i, cb=cb: (i, cb)) for (_, w, cb) in rows]
    in_specs += [pl.BlockSpec(p.shape, lambda i: (0, 0)) for p in pars]
    out_specs = [pl.BlockSpec((tile, w), lambda i: (i, 0)) for (w, _) in outs]
    out_specs += [pl.BlockSpec(s, lambda i: (0, 0)) for s in accs]
    out_shape = [jax.ShapeDtypeStruct((length, w), dt) for (w, dt) in outs]
    out_shape += [jax.ShapeDtypeStruct(s, F32) for s in accs]
    return pl.pallas_call(
        body, grid=(length // tile,), in_specs=in_specs, out_specs=out_specs, out_shape=out_shape, name=name,
        compiler_params=_params(("arbitrary",)),
    )(*[r[0] for r in rows], *pars)


def _vjp_of(f, n_row, n_cot, row_want):
    def g(*a):
        prow, cots, par = a[:n_row], a[n_row:n_row + n_cot], a[n_row + n_cot:]
        _, vjp = jax.vjp(f, *prow, *par)
        grads = vjp(tuple(cots))
        return tuple(grads[i] for i in row_want) + tuple(grads[n_row:])
    return g


def _f_pn(x, g, sc, sh):
    return (x, x * _rms(x) * g * (1.0 + sc) + sh)


def _f_res_pn(x, y, gt, g, sc, sh):
    xn = x + gt * y
    return (xn, xn * _rms(xn) * g * (1.0 + sc) + sh)


def _f_glu_res_pn(x, val, gate, gt, g, sc, sh):
    xn = x + gt * (val * _sigmoid(gate))
    return (xn, xn * _rms(xn) * g * (1.0 + sc) + sh)


def _f_final(x, y, tgt, gt, g):
    xn = x + gt * y
    err = xn * _rms(xn) * g - tgt
    return 0.5 * jnp.mean(err * err, axis=-1, keepdims=True)


def _g_final(x, y, tgt, gt, g):
    lrow, vjp = jax.vjp(_f_final, x, y, tgt, gt, g)
    dx, dy, _, dgt, dg = vjp(jnp.ones_like(lrow))
    return dx, dy, dgt, dg, jnp.sum(lrow)


def _full(a):
    return (a, a.shape[1], 0)


def _s5_prep_f(a_re, a_im, log_dt, bt_re, bt_im, e01):
    dt = jnp.exp(_dot_exact01(log_dt, e01))
    mag = jnp.exp(a_re * dt)
    ph = a_im * dt
    lb_re = mag * jnp.cos(ph)
    lb_im = mag * jnp.sin(ph)
    den = a_re * a_re + a_im * a_im
    nr = lb_re - 1.0
    ni = lb_im
    f_re = (nr * a_re + ni * a_im) / den
    f_im = (ni * a_re - nr * a_im) / den
    bb_re = f_re * bt_re - f_im * bt_im
    bb_im = f_re * bt_im + f_im * bt_re
    return lb_re, lb_im, bb_re, bb_im


def _s5_prep_outs():
    return [jax.ShapeDtypeStruct((1, S5_NSTATE), F32)] * 2 + [jax.ShapeDtypeStruct((S5_GROUP, S5_NSTATE), F32)] * 2


def _s5_prep(a_re, a_im, log_dt, bt_re, bt_im, e01):
    def body(*refs):
        res = _s5_prep_f(*[r[...] for r in refs[:6]])
        for o, v in zip(refs[6:], res):
            o[...] = v
    return pl.pallas_call(body, out_shape=_s5_prep_outs(), name="s5_prep",
                          compiler_params=pltpu.CompilerParams(vmem_limit_bytes=VMEM_LIMIT))(a_re, a_im, log_dt, bt_re, bt_im, e01)


def _s5_prep_bwd(a_re, a_im, log_dt, bt_re, bt_im, e01, d_lb_re, d_lb_im, d_bb_re, d_bb_im):
    def f(a_re, a_im, log_dt, bt_re, bt_im, e01):
        @jax.custom_vjp
        def expand(v):
            return _dot_exact01(v, e01)
        expand.defvjp(lambda v: (_dot_exact01(v, e01), None), lambda _, ct: (_dot_exact01(ct, e01, _NT),))
        dt = jnp.exp(expand(log_dt))
        mag = jnp.exp(a_re * dt)
        ph = a_im * dt
        lb_re = mag * jnp.cos(ph)
        lb_im = mag * jnp.sin(ph)
        den = a_re * a_re + a_im * a_im
        nr = lb_re - 1.0
        f_re = (nr * a_re + lb_im * a_im) / den
        f_im = (lb_im * a_re - nr * a_im) / den
        return lb_re, lb_im, f_re * bt_re - f_im * bt_im, f_re * bt_im + f_im * bt_re

    def body(*refs):
        ins = [r[...] for r in refs[:5]]
        e = refs[5][...]
        cots = tuple(r[...] for r in refs[6:10])
        _, vjp = jax.vjp(lambda *p: f(*p, e), *ins)
        for o, v in zip(refs[10:], vjp(cots)):
            o[...] = v
    outs = [jax.ShapeDtypeStruct(v.shape, F32) for v in (a_re, a_im, log_dt, bt_re, bt_im)]
    return pl.pallas_call(body, out_shape=outs, name="s5_prep_bwd",
                          compiler_params=pltpu.CompilerParams(vmem_limit_bytes=VMEM_LIMIT))(
        a_re, a_im, log_dt, bt_re, bt_im, e01, d_lb_re, d_lb_im, d_bb_re, d_bb_im)


def _s5_scan(x_re, x_im, a_r, a_i, c_r, c_i, n_tiles, reverse):
    sgn = -1.0 if reverse else 1.0

    def tile(k, carry):
        cr, ci = carry
        i = (n_tiles - 1 - k) if reverse else k
        order = range(7, -1, -1) if reverse else range(8)
        for j in order:
            br = x_re[i, pl.ds(j, 1), :]
            bi = x_im[i, pl.ds(j, 1), :]
            nr = a_r * cr - (sgn * a_i) * ci + br
            ni = a_r * ci + (sgn * a_i) * cr + bi
            x_re[i, pl.ds(j, 1), :] = nr
            x_im[i, pl.ds(j, 1), :] = ni
            cr, ci = nr, ni
        return cr, ci

    return lax.fori_loop(0, n_tiles, tile, (c_r, c_i))


def _s5_fwd(u, lam_re, lam_im, bbd_re, bbd_im, cbd_re, cbd_im, d_skip, comm, *, tc):
    length = u.shape[0]
    tc = min(tc, length)
    nt = length // tc
    nci, nco = len(comm.ins), len(comm.out_shapes)

    def body(*refs):
        u_ref, lr_ref, li_ref, br_ref, bi_ref, cr_ref, ci_ref, d_ref = refs[:8]
        z_ref, sr_ref, si_ref = refs[8 + nci:11 + nci]
        xr, xi, car_r, car_i = refs[11 + nci + nco:15 + nci + nco]
        run_comm = functools.partial(comm.run, pl.program_id(0) * nt + pl.program_id(1), S5_NB * nt, refs[8:8 + nci],
                                     refs[11 + nci:11 + nci + nco], refs[15 + nci + nco:])
        run_comm(("first", "mid"))

        @pl.when(pl.program_id(1) == 0)
        def _():
            car_r[...] = jnp.zeros_like(car_r)
            car_i[...] = jnp.zeros_like(car_i)
        sr_ref[0] = car_r[...]
        si_ref[0] = car_i[...]
        uv = u_ref[...]
        xr[...] = _dot(uv, br_ref[0]).reshape(tc // 8, 8, S5_BS)
        xi[...] = _dot(uv, bi_ref[0]).reshape(tc // 8, 8, S5_BS)
        cr, ci = _s5_scan(xr, xi, lr_ref[...], li_ref[...], car_r[...], car_i[...], tc // 8, False)
        car_r[...] = cr
        car_i[...] = ci
        y = (_dot(xr[...].reshape(tc, S5_BS), cr_ref[0]) - _dot(xi[...].reshape(tc, S5_BS), ci_ref[0]) + d_ref[...] * uv)
        z_ref[...] = _gelu(y).astype(z_ref.dtype)
        run_comm(("last",))

    blk_u = pl.BlockSpec((tc, S5_BC), lambda g, t: (t, g))
    blk_l = pl.BlockSpec((1, S5_BS), lambda g, t: (0, g))
    blk_b = pl.BlockSpec((1, S5_BC, S5_BS), lambda g, t: (g, 0, 0))
    blk_c = pl.BlockSpec((1, S5_BS, S5_BC), lambda g, t: (g, 0, 0))
    blk_d = pl.BlockSpec((1, S5_BC), lambda g, t: (0, g))
    blk_s = pl.BlockSpec((1, 1, S5_BS), lambda g, t: (t, 0, g))
    return pl.pallas_call(
        body, grid=(S5_NB, nt),
        in_specs=[blk_u, blk_l, blk_l, blk_b, blk_b, blk_c, blk_c, blk_d] + [_ANY_SPEC] * nci,
        out_specs=[blk_u, blk_s, blk_s] + [_ANY_SPEC] * nco,
        out_shape=[jax.ShapeDtypeStruct((length, D_MODEL), MXU_DT),
                   jax.ShapeDtypeStruct((nt, 1, S5_NSTATE), F32), jax.ShapeDtypeStruct((nt, 1, S5_NSTATE), F32)] + comm.out_shapes,
        scratch_shapes=[pltpu.VMEM((tc // 8, 8, S5_BS), F32), pltpu.VMEM((tc // 8, 8, S5_BS), F32),
                        pltpu.VMEM((1, S5_BS), F32), pltpu.VMEM((1, S5_BS), F32)] + comm.scratch,
        name="s5_fwd", compiler_params=_params(("arbitrary", "arbitrary")),
    )(u, lam_re, lam_im, bbd_re, bbd_im, cbd_re, cbd_im, d_skip, *comm.ins)


def _s5_bwd(u, dz, st_re, st_im, lam_re, lam_im, bbd_re, bbd_im, cbd_re, cbd_im, d_skip, comm, *, tc):
    length = u.shape[0]
    tc = min(tc, length)
    nt = length // tc
    nci, nco = len(comm.ins), len(comm.out_shapes)

    def body(*refs):
        u_ref, dz_ref, sr_ref, si_ref, lr_ref, li_ref, br_ref, bi_ref, cr_ref, ci_ref, d_ref = refs[:11]
        du_ref, dbr_ref, dbi_ref, dcr_ref, dci_ref, dlr_ref, dli_ref, dd_ref = refs[11 + nci:19 + nci]
        xr, xi, gr, gi, car_r, car_i = refs[19 + nci + nco:25 + nci + nco]
        run_comm = functools.partial(comm.run, pl.program_id(0) * nt + pl.program_id(1), S5_NB * nt, refs[11:11 + nci],
                                     refs[19 + nci:19 + nci + nco], refs[25 + nci + nco:])
        run_comm(("first", "mid"))

        @pl.when(pl.program_id(1) == 0)
        def _():
            car_r[...] = jnp.zeros_like(car_r)
            car_i[...] = jnp.zeros_like(car_i)
            for r in (dbr_ref, dbi_ref, dcr_ref, dci_ref, dlr_ref, dli_ref, dd_ref):
                r[...] = jnp.zeros(r.shape, F32)
        a_r, a_i = lr_ref[...], li_ref[...]
        uv = u_ref[...]
        xr[...] = _dot(uv, br_ref[0]).reshape(tc // 8, 8, S5_BS)
        xi[...] = _dot(uv, bi_ref[0]).reshape(tc // 8, 8, S5_BS)
        _s5_scan(xr, xi, a_r, a_i, sr_ref[0], si_ref[0], tc // 8, False)
        xrv = xr[...].reshape(tc, S5_BS)
        xiv = xi[...].reshape(tc, S5_BS)
        y = _dot(xrv, cr_ref[0]) - _dot(xiv, ci_ref[0]) + d_ref[...] * uv
        _, gelu_vjp = jax.vjp(_gelu, y)
        dy = gelu_vjp(dz_ref[...].astype(F32))[0]
        dd_ref[...] += jnp.sum(dy * uv, axis=0, keepdims=True)
        dcr_ref[0] += _dot(xrv, dy, _TN)
        dci_ref[0] -= _dot(xiv, dy, _TN)
        gr[...] = _dot(dy, cr_ref[0], _NT).reshape(tc // 8, 8, S5_BS)
        gi[...] = (-_dot(dy, ci_ref[0], _NT)).reshape(tc // 8, 8, S5_BS)
        cr, ci = _s5_scan(gr, gi, a_r, a_i, car_r[...], car_i[...], tc // 8, True)
        car_r[...] = cr
        car_i[...] = ci
        grv = gr[...].reshape(tc, S5_BS)
        giv = gi[...].reshape(tc, S5_BS)
        first = lax.broadcasted_iota(jnp.int32, (tc, 1), 0) == 0
        xpr = jnp.where(first, sr_ref[0], pltpu.roll(xrv, 1, 0))
        xpi = jnp.where(first, si_ref[0], pltpu.roll(xiv, 1, 0))
        dlr_ref[...] += jnp.sum(grv * xpr + giv * xpi, axis=0, keepdims=True)
        dli_ref[...] += jnp.sum(giv * xpr - grv * xpi, axis=0, keepdims=True)
        dbr_ref[0] += _dot(uv, grv, _TN)
        dbi_ref[0] += _dot(uv, giv, _TN)
        du_ref[...] = _dot(grv, br_ref[0], _NT) + _dot(giv, bi_ref[0], _NT) + d_ref[...] * dy
        run_comm(("last",))

    rev = lambda t: nt - 1 - t
    blk_u = pl.BlockSpec((tc, S5_BC), lambda g, t: (rev(t), g))
    blk_l = pl.BlockSpec((1, S5_BS), lambda g, t: (0, g))
    blk_b = pl.BlockSpec((1, S5_BC, S5_BS), lambda g, t: (g, 0, 0))
    blk_c = pl.BlockSpec((1, S5_BS, S5_BC), lambda g, t: (g, 0, 0))
    blk_d = pl.BlockSpec((1, S5_BC), lambda g, t: (0, g))
    blk_s = pl.BlockSpec((1, 1, S5_BS), lambda g, t: (rev(t), 0, g))
    return pl.pallas_call(
        body, grid=(S5_NB, nt),
        in_specs=[blk_u, blk_u, blk_s, blk_s, blk_l, blk_l, blk_b, blk_b, blk_c, blk_c, blk_d] + [_ANY_SPEC] * nci,
        out_specs=[blk_u, blk_b, blk_b, blk_c, blk_c, blk_l, blk_l, blk_d] + [_ANY_SPEC] * nco,
        out_shape=[jax.ShapeDtypeStruct((length, D_MODEL), F32),
                   jax.ShapeDtypeStruct((S5_NB, S5_BC, S5_BS), F32), jax.ShapeDtypeStruct((S5_NB, S5_BC, S5_BS), F32),
                   jax.ShapeDtypeStruct((S5_NB, S5_BS, S5_BC), F32), jax.ShapeDtypeStruct((S5_NB, S5_BS, S5_BC), F32),
                   jax.ShapeDtypeStruct((1, S5_NSTATE), F32), jax.ShapeDtypeStruct((1, S5_NSTATE), F32),
                   jax.ShapeDtypeStruct((1, D_MODEL), F32)] + comm.out_shapes,
        scratch_shapes=[pltpu.VMEM((tc // 8, 8, S5_BS), F32), pltpu.VMEM((tc // 8, 8, S5_BS), F32),
                        pltpu.VMEM((tc // 8, 8, S5_BS), F32), pltpu.VMEM((tc // 8, 8, S5_BS), F32),
                        pltpu.VMEM((1, S5_BS), F32), pltpu.VMEM((1, S5_BS), F32)] + comm.scratch,
        name="s5_bwd", compiler_params=_params(("arbitrary", "arbitrary")),
    )(u, dz, st_re, st_im, lam_re, lam_im, bbd_re, bbd_im, cbd_re, cbd_im, d_skip, *comm.ins)


def _blockdiag_b(bt):
    eye = jnp.eye(S5_GB, dtype=bt.dtype)
    t = bt.reshape(S5_GROUP, S5_NB, S5_GB, S5_STATE)
    return jnp.einsum("ab,hnbp->nahbp", eye, t).reshape(S5_NB, S5_BC, S5_BS)


def _unblockdiag_b(m):
    eye = jnp.eye(S5_GB, dtype=m.dtype)
    t = m.reshape(S5_NB, S5_GB, S5_GROUP, S5_GB, S5_STATE)
    return jnp.einsum("ab,nahbp->hnbp", eye, t).reshape(S5_GROUP, S5_NSTATE)


def _blockdiag_c(c):
    eye = jnp.eye(S5_GB, dtype=c.dtype)
    t = c.reshape(S5_NB, S5_GB, S5_GROUP, S5_STATE)
    return jnp.einsum("ab,nbhp->napbh", eye, t).reshape(S5_NB, S5_BS, S5_BC)


def _unblockdiag_c(m):
    eye = jnp.eye(S5_GB, dtype=m.dtype)
    t = m.reshape(S5_NB, S5_GB, S5_STATE, S5_GB, S5_GROUP)
    return jnp.einsum("ab,napbh->nbhp", eye, t).reshape(S5_GROUPS, S5_GROUP, S5_STATE)


def _gla_gates(glr, wg2, bg, tri):
    pre = _dot(glr, wg2) + bg
    la = _logsig(pre) * (1.0 / GLA_TAU)
    gc = _dot_f32(tri, la)
    gend = gc[GLA_CHUNK - 1:GLA_CHUNK, :]
    e = jnp.exp(gend - gc)
    return pre, e, jnp.exp(gend)


def _gla_specs(nc, rev):
    ix = (lambda n: nc - 1 - n) if rev else (lambda n: n)
    c = GLA_CHUNK
    return dict(
        q=pl.BlockSpec((c, GLA_QK), lambda n: (ix(n), 0)),
        k=pl.BlockSpec((c, GLA_QK), lambda n: (ix(n), 1)),
        v=pl.BlockSpec((c, D_MODEL), lambda n: (ix(n), 1)),
        r=pl.BlockSpec((c, D_MODEL), lambda n: (ix(n), 2)),
        glr=pl.BlockSpec((c, GLA_RANK_PAD), lambda n: (ix(n), (2 * GLA_QK + 2 * D_MODEL) // GLA_RANK_PAD)),
        wg2=pl.BlockSpec((GLA_RANK_PAD, GLA_QK), lambda n: (0, 0)),
        bg=pl.BlockSpec((1, GLA_QK), lambda n: (0, 0)),
        gn=pl.BlockSpec((1, D_MODEL), lambda n: (0, 0)),
        tri=pl.BlockSpec((c, c), lambda n: (0, 0)),
        row=pl.BlockSpec((c, D_MODEL), lambda n: (ix(n), 0)),
        rowp=pl.BlockSpec((c, GLA_INP), lambda n: (ix(n), 0)),
        st=pl.BlockSpec((1, GLA_HEADS, GLA_DV, GLA_DK), lambda n: (ix(n), 0, 0, 0)),
    )


def _gla_fwd(proj, wg2, bg, gn, tri, comm):
    length = proj.shape[0]
    nc = length // GLA_CHUNK
    scale = GLA_DK ** -0.5
    nci, nco = len(comm.ins), len(comm.out_shapes)

    def body(*refs):
        q_ref, k_ref, v_ref, r_ref, glr_ref, wg2_ref, bg_ref, gn_ref, tri_ref = refs[:9]
        og_ref, sp_ref = refs[9 + nci:11 + nci]
        st = refs[11 + nci + nco]
        run_comm = functools.partial(comm.run, pl.program_id(0), nc, refs[9:9 + nci], refs[11 + nci:11 + nci + nco],
                                     refs[12 + nci + nco:])
        run_comm(("first", "mid"))

        @pl.when(pl.program_id(0) == 0)
        def _():
            st[...] = jnp.zeros_like(st)
        _, e, dec = _gla_gates(glr_ref[...], wg2_ref[...], bg_ref[...], tri_ref[...])
        kd = k_ref[...].astype(F32) * e
        q = q_ref[...].astype(F32) * scale
        for h in range(GLA_HEADS):
            sk = slice(h * GLA_DK, (h + 1) * GLA_DK)
            sv = slice(h * GLA_DV, (h + 1) * GLA_DV)
            sp_ref[0, h] = st[h]
            stn = dec[:, sk] * st[h] + _dot(v_ref[:, sv], kd[:, sk], _TN)
            st[h] = stn
            o = _dot(q[:, sk], stn, _NT)
            on = o * _rms(o)
            og_ref[:, sv] = (on * gn_ref[:, sv] * _silu(r_ref[:, sv].astype(F32))).astype(og_ref.dtype)
        run_comm(("last",))

    s = _gla_specs(nc, False)
    return pl.pallas_call(
        body, grid=(nc,),
        in_specs=[s["q"], s["k"], s["v"], s["r"], s["glr"], s["wg2"], s["bg"], s["gn"], s["tri"]] + [_ANY_SPEC] * nci,
        out_specs=[s["row"], s["st"]] + [_ANY_SPEC] * nco,
        out_shape=[jax.ShapeDtypeStruct((length, D_MODEL), MXU_DT),
                   jax.ShapeDtypeStruct((nc, GLA_HEADS, GLA_DV, GLA_DK), F32)] + comm.out_shapes,
        scratch_shapes=[pltpu.VMEM((GLA_HEADS, GLA_DV, GLA_DK), F32)] + comm.scratch,
        name="gla_fwd", compiler_params=_params(("arbitrary",)),
    )(proj, proj, proj, proj, proj, wg2, bg, gn, tri, *comm.ins)


def _gla_bwd(proj, d_og, s_prev, wg2, bg, gn, tri):
    length = proj.shape[0]
    nc = length // GLA_CHUNK
    scale = GLA_DK ** -0.5

    def body(q_ref, k_ref, v_ref, r_ref, glr_ref, dog_ref, sp_ref, wg2_ref, bg_ref, gn_ref, tri_ref,
             dp_ref, dwg2_ref, dbg_ref, dgn_ref, dst):
        dq_ref = dp_ref.at[:, pl.ds(0, GLA_QK)]
        dk_ref = dp_ref.at[:, pl.ds(GLA_QK, GLA_QK)]
        dv_ref = dp_ref.at[:, pl.ds(2 * GLA_QK, D_MODEL)]
        dr_ref = dp_ref.at[:, pl.ds(2 * GLA_QK + D_MODEL, D_MODEL)]
        dglr_ref = dp_ref.at[:, pl.ds(2 * GLA_QK + 2 * D_MODEL, GLA_RANK_PAD)]

        @pl.when(pl.program_id(0) == 0)
        def _():
            dst[...] = jnp.zeros_like(dst)
            dwg2_ref[...] = jnp.zeros_like(dwg2_ref)
            dbg_ref[...] = jnp.zeros_like(dbg_ref)
            dgn_ref[...] = jnp.zeros_like(dgn_ref)
        glr = glr_ref[...]
        pre, e, dec = _gla_gates(glr, wg2_ref[...], bg_ref[...], tri_ref[...])
        k = k_ref[...].astype(F32)
        kd = k * e
        q = q_ref[...].astype(F32) * scale
        dkd_parts, ddec_parts = [], []
        for h in range(GLA_HEADS):
            sk = slice(h * GLA_DK, (h + 1) * GLA_DK)
            sv = slice(h * GLA_DV, (h + 1) * GLA_DV)
            stp = sp_ref[0, h]
            vh = v_ref[:, sv]
            stn = dec[:, sk] * stp + _dot(vh, kd[:, sk], _TN)
            o = _dot(q[:, sk], stn, _NT)
            rinv = _rms(o)
            on = o * rinv
            rv = r_ref[:, sv].astype(F32)
            sg = _sigmoid(rv)
            sr = rv * sg
            dog = dog_ref[:, sv].astype(F32)
            gnh = gn_ref[:, sv]
            d_ong = dog * sr
            dr_ref[:, sv] = (dog * (on * gnh) * (sg * (1.0 + rv * (1.0 - sg)))).astype(dr_ref.dtype)
            dgn_ref[:, sv] += jnp.sum(d_ong * on, axis=0, keepdims=True)
            d_on = d_ong * gnh
            do = rinv * (d_on - on * jnp.mean(d_on * on, axis=-1, keepdims=True))
            dq_ref[:, sk] = (_dot(do, stn) * scale).astype(dq_ref.dtype)
            dstn = dst[h] + _dot(do, q[:, sk], _TN)
            dst[h] = dec[:, sk] * dstn
            ddec_parts.append(jnp.sum(dstn * stp, axis=0, keepdims=True))
            dv_ref[:, sv] = _dot(kd[:, sk], dstn, _NT).astype(dv_ref.dtype)
            dkd_parts.append(_dot(vh, dstn))
        dkd = jnp.concatenate(dkd_parts, axis=1)
        ddec = jnp.concatenate(ddec_parts, axis=1)
        dk_ref[...] = (dkd * e).astype(dk_ref.dtype)
        w = dkd * kd
        dgend = jnp.sum(w, axis=0, keepdims=True) + ddec * dec
        dla = dgend - _dot_f32(tri_ref[...], w, _TN)
        dpre = dla * (1.0 - _sigmoid(pre)) * (1.0 / GLA_TAU)
        dwg2_ref[...] += _dot(glr, dpre, _TN)
        dbg_ref[...] += jnp.sum(dpre, axis=0, keepdims=True)
        dglr_ref[...] = _dot(dpre, wg2_ref[...], _NT).astype(dglr_ref.dtype)

    s = _gla_specs(nc, True)
    return pl.pallas_call(
        body, grid=(nc,),
        in_specs=[s["q"], s["k"], s["v"], s["r"], s["glr"], s["row"], s["st"], s["wg2"], s["bg"], s["gn"], s["tri"]],
        out_specs=[s["rowp"], s["wg2"], s["bg"], s["gn"]],
        out_shape=[jax.ShapeDtypeStruct((length, GLA_INP), MXU_DT),
                   jax.ShapeDtypeStruct((GLA_RANK_PAD, GLA_QK), F32), jax.ShapeDtypeStruct((1, GLA_QK), F32),
                   jax.ShapeDtypeStruct((1, D_MODEL), F32)],
        scratch_shapes=[pltpu.VMEM((GLA_HEADS, GLA_DV, GLA_DK), F32)],
        name="gla_bwd", compiler_params=_params(("arbitrary",)),
    )(proj, proj, proj, proj, proj, d_og, s_prev, wg2, bg, gn, tri)


def _local_step(x, tgt, mods, nrm, s5p, w, shards, core, *, row_tile=256, s5_tc=256):
    length = x.shape[0]
    tmm = 512
    glu_sh, gin_sh, gout_sh, ff1a_sh, ff1b_sh, ff2a_sh, ff2b_sh = shards
    w = dict(w)
    rc = functools.partial(_rowcall, tile=row_tile)
    (sh1a, sc1a, gt1a, sh2a, sc2a, gt2a), (sh1b, sc1b, gt1b, sh2b, sc2b, gt2b) = mods
    vec = (1, D_MODEL)
    row32, row16 = (D_MODEL, F32), (D_MODEL, MXU_DT)

    (h0,) = rc(lambda *a: _f_pn(*a)[1:], [_full(x)], [nrm["mix"][0], sc1a, sh1a], [row32], [], name="pn0")
    lam_re, lam_im, bb_re, bb_im = _s5_prep(s5p["a_re"], s5p["a_im"], s5p["log_dt"], s5p["bt_re"], s5p["bt_im"], s5p["e01"])
    bbd_re = _blockdiag_b(bb_re.reshape(S5_GROUP, S5_GROUPS, S5_STATE)).astype(MXU_DT)
    bbd_im = _blockdiag_b(bb_im.reshape(S5_GROUP, S5_GROUPS, S5_STATE)).astype(MXU_DT)
    cbd_re = _blockdiag_c(s5p["c_re"]).astype(MXU_DT)
    cbd_im = _blockdiag_c(s5p["c_im"]).astype(MXU_DT)
    z0, st_re, st_im, glu_s, ff1a_s, ff2a_s = _s5_fwd(
        h0, lam_re, lam_im, bbd_re, bbd_im, cbd_re, cbd_im, s5p["d"], _ag_comm([glu_sh, ff1a_sh, ff2a_sh]), tc=s5_tc)
    w["glu"] = glu_s.transpose(1, 0, 2).reshape(D_MODEL, 2 * D_MODEL)
    w["ff1"] = [ff1a_s.transpose(1, 0, 2).reshape(D_MODEL, D_FF), None]
    w["ff2"] = [ff2a_s.reshape(D_FF, D_MODEL), None]
    vg = _mm(z0, w["glu"], "nn", F32, tm=tmm, tn=2048, name="glu_mm")
    x1, h1 = rc(_f_glu_res_pn, [_full(x), (vg, D_MODEL, 0), (vg, D_MODEL, 1)], [gt1a, nrm["mlp"][0], sc2a, sh2a],
                [row32, row16], [], name="node1")
    relu = lambda acc: jnp.maximum(acc, 0.0)
    sq = lambda a: a * a
    a0, gin_s = _mm(h1, w["ff1"][0], "nn", MXU_DT, tm=tmm, tn=2048, name="ff1a", out_fn=relu, comm=_ag_comm([gin_sh]))
    f0, ff1b_s = _mm(a0, w["ff2"][0], "nn", F32, tm=tmm, tn=1024, name="ff2a", a_fn=sq, comm=_ag_comm([ff1b_sh]))
    gin_full = gin_s.transpose(1, 0, 2).reshape(D_MODEL, GLA_IN)
    q_, k_, v_, glr_, r_ = jnp.split(gin_full, [GLA_QK, 2 * GLA_QK, 2 * GLA_QK + D_MODEL, 2 * GLA_QK + D_MODEL + GLA_RANK], axis=1)
    w["gin"] = jnp.concatenate([q_, k_, v_, r_, glr_, jnp.zeros((D_MODEL, GLA_RANK_PAD - GLA_RANK), MXU_DT)], axis=1)
    w["ff1"][1] = ff1b_s.transpose(1, 0, 2).reshape(D_MODEL, D_FF)
    x2, h2 = rc(_f_res_pn, [_full(x1), _full(f0)], [gt2a, nrm["mix"][1], sc1b, sh1b], [row32, row16], [], name="node2")
    proj = _mm(h2, w["gin"], "nn", MXU_DT, tm=tmm, tn=GLA_INP, name="gla_in")
    og, s_prev, gout_s, ff2b_s = _gla_fwd(proj, w["wg2"], w["bg"], w["gn"], w["tri"], _ag_comm([gout_sh, ff2b_sh]))
    w["gout"] = gout_s.reshape(D_MODEL, D_MODEL)
    w["ff2"][1] = ff2b_s.reshape(D_FF, D_MODEL)
    y1 = _mm(og, w["gout"], "nn", F32, tm=tmm, tn=1024, name="gla_out")
    x3, h3 = rc(_f_res_pn, [_full(x2), _full(y1)], [gt1b, nrm["mlp"][1], sc2b, sh2b], [row32, row16], [], name="node3")
    a1 = _mm(h3, w["ff1"][1], "nn", MXU_DT, tm=tmm, tn=2048, name="ff1b", out_fn=relu)
    f1 = _mm(a1, w["ff2"][1], "nn", F32, tm=tmm, tn=1024, name="ff2b", a_fn=sq)

    g = {}
    dx, df, g["gt2b"], g["nf"], loss = rc(_g_final, [_full(x3), _full(f1), _full(tgt)], [gt2b, nrm["final"]],
                                          [row32, row16], [vec, vec, (8, 128)], name="final")

    def mlp_bwd(df, a, h, w1, w2, tag):
        dw2 = _mm(a, df, "tn", MXU_DT, tm=1024, tn=1024, name="dff2" + tag, a_fn=sq)
        du = _mm(df, w2, "nt", MXU_DT, tm=tmm, tn=2048, name="dact" + tag, extra=a, out_fn=lambda acc, e: acc * (2.0 * e))
        dw1 = _mm(h, du, "tn", MXU_DT, tm=1024, tn=D_FF // N_DEV, name="dff1" + tag, by_owner=True)
        dh = _mm(du, w1, "nt", F32, tm=tmm, tn=1024, name="dh" + tag)
        return dw1, dw2, dh

    def node_bwd(f, prim_rows, cots, pars, row_want, outs, name):
        nrow = len(prim_rows)
        return rc(_vjp_of(f, nrow, len(cots), row_want), prim_rows + cots, pars, outs, [vec] * len(pars), name=name)

    g["ff1b"], g["ff2b"], dh3 = mlp_bwd(df, a1, h3, w["ff1"][1], w["ff2"][1], "b")
    dx, dy1, g["gt1b"], g["mlp1"], g["sc2b"], g["sh2b"] = node_bwd(
        _f_res_pn, [_full(x2), _full(y1)], [_full(dx), _full(dh3)], [gt1b, nrm["mlp"][1], sc2b, sh2b], (0, 1),
        [row32, row16], "node3_bwd")
    g["gout"] = _mm(og, dy1, "tn", MXU_DT, tm=512, tn=1024, name="dgout")
    d_og = _mm(dy1, w["gout"], "nt", MXU_DT, tm=tmm, tn=1024, name="dog")
    dproj, g["wg2"], g["bg"], g["gn"] = _gla_bwd(proj, d_og, s_prev, w["wg2"], w["bg"], w["gn"], w["tri"])
    g["gin"] = _mm(h2, dproj, "tn", MXU_DT, tm=512, tn=640, name="dgin")
    dh2 = _mm(dproj, w["gin"], "nt", F32, tm=tmm, tn=1024, name="dh2")
    dx, df0, g["gt2a"], g["mix1"], g["sc1b"], g["sh1b"] = node_bwd(
        _f_res_pn, [_full(x1), _full(f0)], [_full(dx), _full(dh2)], [gt2a, nrm["mix"][1], sc1b, sh1b], (0, 1),
        [row32, row16], "node2_bwd")
    g["ff1a"], g["ff2a"], dh1 = mlp_bwd(df0, a0, h1, w["ff1"][0], w["ff2"][0], "a")
    glu_vjp = _vjp_of(_f_glu_res_pn, 3, 2, (0, 1, 2))

    def glu_bwd(*a):
        r = glu_vjp(*a)
        return (r[0], jnp.concatenate([r[1], r[2]], axis=1)) + r[3:]

    dx, dvg, g["gt1a"], g["mlp0"], g["sc2a"], g["sh2a"] = rc(
        glu_bwd, [_full(x), (vg, D_MODEL, 0), (vg, D_MODEL, 1), _full(dx), _full(dh1)], [gt1a, nrm["mlp"][0], sc2a, sh2a],
        [row32, (2 * D_MODEL, MXU_DT)], [vec] * 4, name="node1_bwd")
    gin_g = g.pop("gin")
    gin_g = jnp.concatenate([gin_g[:, :2 * GLA_QK + D_MODEL], gin_g[:, GLA_INP - GLA_RANK_PAD:GLA_INP - GLA_RANK_PAD + GLA_RANK],
                             gin_g[:, 2 * GLA_QK + D_MODEL:2 * GLA_QK + 2 * D_MODEL]], axis=1)
    per_owner = [gin_g.reshape(D_MODEL, N_DEV, GLA_IN // N_DEV).transpose(1, 0, 2),
                 g.pop("gout").reshape(N_DEV, D_MODEL // N_DEV, D_MODEL), g.pop("ff1a"), g.pop("ff1b"),
                 g.pop("ff2a").reshape(N_DEV, D_FF // N_DEV, D_MODEL), g.pop("ff2b").reshape(N_DEV, D_FF // N_DEV, D_MODEL)]
    res = _mm(z0, dvg, "tn", MXU_DT, tm=1024, tn=2 * D_MODEL // N_DEV, name="dglu", by_owner=True, comm=_sibling_comm(per_owner))
    glu_g, from_sibling = res[0], list(res[1:])
    dz0, glu_sib = _mm(dvg, w["glu"], "nt", MXU_DT, tm=tmm, tn=1024, name="dz0", comm=_sibling_comm([glu_g]))
    per_owner = [glu_g] + per_owner
    chip_sum = _pair_add(per_owner, [glu_sib] + from_sibling, core, name="rs_add")
    res = _s5_bwd(h0, dz0, st_re, st_im, lam_re, lam_im, bbd_re, bbd_im, cbd_re, cbd_im, s5p["d"], _chips_comm(chip_sum), tc=s5_tc)
    du0, dbbd_re, dbbd_im, dcbd_re, dcbd_im, dlam_re, dlam_im, g["s5_d"] = res[:8]
    from_chips = res[8:]
    g["s5_c_re"] = _unblockdiag_c(dcbd_re)
    g["s5_c_im"] = _unblockdiag_c(dcbd_im)
    g["s5_a_re"], g["s5_a_im"], g["s5_log_dt"], g["s5_bt_re"], g["s5_bt_im"] = _s5_prep_bwd(
        s5p["a_re"], s5p["a_im"], s5p["log_dt"], s5p["bt_re"], s5p["bt_im"], s5p["e01"],
        dlam_re, dlam_im, _unblockdiag_b(dbbd_re), _unblockdiag_b(dbbd_im))
    grad_x, g["mix0"], g["sc1a"], g["sh1a"] = node_bwd(
        _f_pn, [_full(x)], [_full(dx), _full(du0)], [nrm["mix"][0], sc1a, sh1a], (0,), [row32], "node0_bwd")
    return loss[0, 0], grad_x, g, chip_sum, from_chips


_MESH = pl.DeviceIdType.MESH
_VMEM_SPEC = pl.BlockSpec(memory_space=pltpu.VMEM)
_ANY_SPEC = pl.BlockSpec(memory_space=pl.ANY)


def _my_place():
    ix, iy, ic = lax.axis_index("x"), lax.axis_index("y"), lax.axis_index("c")
    return ix, iy, ic


def _exchange(x, *, gather, name):
    r = x.shape[-2]

    def body(x_ref, o_ref, ssem, rsem):
        ix, iy, ic = _my_place()
        me = 4 * ix + 2 * iy + ic
        if gather:
            o_ref[me] = x_ref[...]
        else:
            o_ref[me] = x_ref[me]
        copies = []
        for k in range(1, N_DEV):
            tx, ty, tc = ix ^ (k >> 2), iy ^ ((k >> 1) & 1), ic ^ (k & 1)
            src = x_ref if gather else x_ref.at[4 * tx + 2 * ty + tc]
            cp = pltpu.make_async_remote_copy(src_ref=src, dst_ref=o_ref.at[me], send_sem=ssem.at[k - 1],
                                              recv_sem=rsem.at[k - 1], device_id=(tx, ty, tc), device_id_type=_MESH)
            cp.start()
            copies.append(cp)
        for cp in copies:
            cp.wait()

    return pl.pallas_call(
        body, out_shape=jax.ShapeDtypeStruct((N_DEV, r, 128), x.dtype), in_specs=[_VMEM_SPEC], out_specs=_VMEM_SPEC,
        scratch_shapes=[pltpu.SemaphoreType.DMA((N_DEV - 1,)), pltpu.SemaphoreType.DMA((N_DEV - 1,))], name=name,
    )(x)


class _Comm:
    def __init__(self, ins, out_shapes, scratch, phases):
        self.ins, self.out_shapes, self.scratch, self.phases = list(ins), list(out_shapes), list(scratch), phases

    def run(self, step, n_steps, in_refs, out_refs, scratch_refs, only):
        when = {"first": 0, "mid": (7 * n_steps) // 8, "last": n_steps - 1}
        for phase, fn in self.phases:
            if phase in only:
                pl.when(step == when[phase])(functools.partial(fn, in_refs, out_refs, scratch_refs))


def _ag_comm(xs):
    n = len(xs)

    def parts():
        ix, iy, ic = _my_place()
        return ic, (ix, iy, ic), (ix, iy, 1 - ic), [(1 - ix, iy), (ix, 1 - iy), (1 - ix, 1 - iy)]

    def copy(ins, outs, sc, a, k, block, to, from_x=False):
        px, py, pc = block
        slot = outs[a].at[4 * px + 2 * py + pc]
        return pltpu.make_async_remote_copy(
            src_ref=ins[a] if from_x else slot, dst_ref=slot, send_sem=sc[0].at[7 * a + k], recv_sem=sc[1].at[7 * a + k],
            device_id=to, device_id_type=_MESH)

    def local(ins, outs, sc, a, me):
        return pltpu.make_async_copy(ins[a], outs[a].at[4 * me[0] + 2 * me[1] + me[2]], sc[2].at[a])

    def start(ins, outs, sc):
        ic, me, sibling, chips = parts()
        for a in range(n):
            local(ins, outs, sc, a, me).start()
            copy(ins, outs, sc, a, 0, me, sibling, True).start()
            for j, chip in enumerate(chips):
                copy(ins, outs, sc, a, 1 + j, me, (*chip, ic), True).start()

    def forward(ins, outs, sc):
        ic, me, sibling, chips = parts()
        for a in range(n):
            for j, chip in enumerate(chips):
                copy(ins, outs, sc, a, 1 + j, (*chip, ic), me).wait_recv()
                copy(ins, outs, sc, a, 4 + j, (*chip, ic), sibling).start()

    def finish(ins, outs, sc):
        ic, me, sibling, chips = parts()
        for a in range(n):
            copy(ins, outs, sc, a, 0, sibling, me).wait_recv()
            for j, chip in enumerate(chips):
                copy(ins, outs, sc, a, 4 + j, (*chip, 1 - ic), me).wait_recv()
        for a in range(n):
            copy(ins, outs, sc, a, 0, me, sibling, True).wait_send()
            for j, chip in enumerate(chips):
                copy(ins, outs, sc, a, 1 + j, me, (*chip, ic), True).wait_send()
                copy(ins, outs, sc, a, 4 + j, (*chip, ic), sibling).wait_send()
            local(ins, outs, sc, a, me).wait()

    return _Comm(xs, [jax.ShapeDtypeStruct((N_DEV,) + x.shape, x.dtype) for x in xs],
                 [pltpu.SemaphoreType.DMA((7 * n,)), pltpu.SemaphoreType.DMA((7 * n,)), pltpu.SemaphoreType.DMA((n,))],
                 [("first", start), ("mid", forward), ("last", finish)])


def _chips_comm(ps):
    n = len(ps)

    def copies(ins, outs, sc):
        ix, iy, ic = _my_place()
        out = []
        for a in range(n):
            for k in range(1, 4):
                tx, ty = ix ^ (k >> 1), iy ^ (k & 1)
                out.append(pltpu.make_async_remote_copy(
                    src_ref=ins[a].at[2 * tx + ty], dst_ref=outs[a].at[k - 1], send_sem=sc[0].at[3 * a + k - 1],
                    recv_sem=sc[1].at[3 * a + k - 1], device_id=(tx, ty, ic), device_id_type=_MESH))
        return out

    def start(ins, outs, sc):
        for cp in copies(ins, outs, sc):
            cp.start()

    def finish(ins, outs, sc):
        for cp in copies(ins, outs, sc):
            cp.wait()

    return _Comm(ps, [jax.ShapeDtypeStruct((3,) + p.shape[1:], p.dtype) for p in ps],
                 [pltpu.SemaphoreType.DMA((3 * n,)), pltpu.SemaphoreType.DMA((3 * n,))], [("first", start), ("last", finish)])


def _sibling_comm(gs):
    n = len(gs)

    def copies(ins, outs, sc):
        ix, iy, ic = _my_place()
        return [pltpu.make_async_remote_copy(src_ref=ins[a].at[2 * q + 1 - ic], dst_ref=outs[a].at[q], send_sem=sc[0].at[4 * a + q],
                                             recv_sem=sc[1].at[4 * a + q], device_id=(ix, iy, 1 - ic), device_id_type=_MESH)
                for a in range(n) for q in range(4)]

    def start(ins, outs, sc):
        for cp in copies(ins, outs, sc):
            cp.start()

    def finish(ins, outs, sc):
        for cp in copies(ins, outs, sc):
            cp.wait()

    return _Comm(gs, [jax.ShapeDtypeStruct((4,) + g.shape[1:], g.dtype) for g in gs],
                 [pltpu.SemaphoreType.DMA((4 * n,)), pltpu.SemaphoreType.DMA((4 * n,))], [("first", start), ("last", finish)])


def _ada_fwd(c_all, w_ada, b_cols):
    def body(c_ref, w_ref, b_ref, o_ref):
        cs = _silu(c_ref[...])
        for i in range(2):
            o_ref[i] = _dot(cs, w_ref[i]) + b_ref[pl.ds(i, 1), :]
    return pl.pallas_call(body, out_shape=jax.ShapeDtypeStruct((2, N_DEV, w_ada.shape[2]), F32), name="ada_fwd",
                          compiler_params=pltpu.CompilerParams(vmem_limit_bytes=VMEM_LIMIT))(c_all, w_ada, b_cols)


def _ada_bwd(c_all, dm):
    def body(c_ref, d_ref, o_ref):
        cs = _silu(c_ref[...])
        for i in range(2):
            o_ref[i] = _dot(cs, d_ref[i], _TN)
    return pl.pallas_call(body, out_shape=jax.ShapeDtypeStruct((2, D_MODEL, dm.shape[2]), F32), name="ada_bwd",
                          compiler_params=pltpu.CompilerParams(vmem_limit_bytes=VMEM_LIMIT))(c_all, dm)


def _pair_add(gs, recvs, core, *, name):
    n = len(gs)

    def body(core_ref, *refs):
        for a in range(n):
            refs[2 * n + a][...] = (refs[a][...].astype(F32) + refs[n + a][...].astype(F32)).astype(refs[2 * n + a].dtype)

    own = [pl.BlockSpec((1,) + g.shape[1:], lambda q, core_ref: (2 * q + core_ref[0], 0, 0)) for g in gs]
    slab = [pl.BlockSpec((1,) + g.shape[1:], lambda q, core_ref: (q, 0, 0)) for g in gs]
    grid_spec = pltpu.PrefetchScalarGridSpec(num_scalar_prefetch=1, grid=(4,), in_specs=own + slab, out_specs=slab)
    return pl.pallas_call(body, grid_spec=grid_spec, out_shape=[jax.ShapeDtypeStruct((4,) + g.shape[1:], g.dtype) for g in gs],
                          name=name, compiler_params=_params(("parallel",)))(core, *gs, *recvs)


def _sum_slots(x, *, name):
    def body(x_ref, o_ref):
        acc = x_ref[0]
        for s in range(1, N_DEV):
            acc = acc + x_ref[s]
        o_ref[...] = acc
    return pl.pallas_call(body, out_shape=jax.ShapeDtypeStruct(x.shape[1:], F32), name=name)(x)


def _adamw_math(w, m, v, g):
    mn = ADAM_B1 * m + (1.0 - ADAM_B1) * g
    vn = ADAM_B2 * v + (1.0 - ADAM_B2) * (g * g)
    m_hat = mn / (1.0 - ADAM_B1 ** ADAM_STEP)
    v_hat = vn / (1.0 - ADAM_B2 ** ADAM_STEP)
    return -ADAM_LR * (m_hat / (jnp.sqrt(v_hat) + ADAM_EPS) + ADAM_WD * w), mn, vn


def _adamw_multi(ws, ms, vs, gs, *, name):
    n = len(ws)

    def body(*refs):
        for i in range(n):
            g = refs[3 * n + i][...]
            o = refs[4 * n + 4 * i:4 * n + 4 * i + 4]
            o[0][...] = g
            o[1][...], o[2][...], o[3][...] = _adamw_math(refs[i][...], refs[n + i][...], refs[2 * n + i][...], g)

    out_shape = [jax.ShapeDtypeStruct(w.shape, F32) for w in ws for _ in range(4)]
    res = pl.pallas_call(body, out_shape=out_shape, name=name,
                         compiler_params=pltpu.CompilerParams(vmem_limit_bytes=VMEM_LIMIT))(*ws, *ms, *vs, *gs)
    return [res[4 * i:4 * i + 4] for i in range(n)]


def _adamw(w, m, v, gparts, *, tile, name, sel=None):
    r, cdim = w.shape
    ng = len(gparts)
    sel = jnp.zeros((1,), jnp.int32) if sel is None else sel

    def body(*refs):
        w_ref, m_ref, v_ref = refs[1:4]
        g = None
        for p, part in zip(refs[4:4 + ng], gparts):
            pv = (p[0] if isinstance(part, tuple) else p[...]).astype(F32)
            g = pv if g is None else g + pv
        g_ref, d_ref, nm_ref, nv_ref = refs[4 + ng:]
        g_ref[...] = g
        d_ref[...], nm_ref[...], nv_ref[...] = _adamw_math(w_ref[...], m_ref[...], v_ref[...], g)

    spec = pl.BlockSpec((tile, cdim), lambda i, s: (i, 0))

    def part_spec(part):
        if not isinstance(part, tuple):
            return spec
        slab = part[1]
        if slab is None:
            return pl.BlockSpec((1, tile, cdim), lambda i, s: (s[0], i, 0))
        return pl.BlockSpec((1, tile, cdim), lambda i, s: (slab, i, 0))

    grid_spec = pltpu.PrefetchScalarGridSpec(
        num_scalar_prefetch=1, grid=(r // tile,), in_specs=[spec] * 3 + [part_spec(p) for p in gparts], out_specs=[spec] * 4)
    return pl.pallas_call(
        body, grid_spec=grid_spec, out_shape=[jax.ShapeDtypeStruct(w.shape, F32)] * 4, name=name,
        compiler_params=_params(("parallel",)),
    )(sel, w, m, v, *[p[0] if isinstance(p, tuple) else p for p in gparts])


_REP_ROWS = 272
_REP_SIZE = 2 * 1024 * 2 + 4096 * 2 + 64 + 65536 * 4 + 1024 + 1024


def _pad_rows(v, rows):
    return jnp.pad(v.reshape(-1), (0, rows * 128 - v.size)).reshape(rows, 128)


def kernel(x, c, w_ada, b_ada, norm_mix, norm_mlp, s5_a_re, s5_a_im, s5_log_dt, s5_b_re, s5_b_im, s5_c_re, s5_c_im, s5_d, s5_w_glu, gla_w_in, gla_w_gate2, gla_b_gate, gla_g_norm, gla_w_out, w_ff1, w_ff2, norm_final, loss_target, m_w_ada, m_b_ada, m_norm_mix, m_norm_mlp, m_s5_a_re, m_s5_a_im, m_s5_log_dt, m_s5_b_re, m_s5_b_im, m_s5_c_re, m_s5_c_im, m_s5_d, m_s5_w_glu, m_gla_w_in, m_gla_w_gate2, m_gla_b_gate, m_gla_g_norm, m_gla_w_out, m_w_ff1, m_w_ff2, m_norm_final, v_w_ada, v_b_ada, v_norm_mix, v_norm_mlp, v_s5_a_re, v_s5_a_im, v_s5_log_dt, v_s5_b_re, v_s5_b_im, v_s5_c_re, v_s5_c_im, v_s5_d, v_s5_w_glu, v_gla_w_in, v_gla_w_gate2, v_gla_b_gate, v_gla_g_norm, v_gla_w_out, v_w_ff1, v_w_ff2, v_norm_final):
    ix, iy, ic = _my_place()
    me = 4 * ix + 2 * iy + ic
    ada_w = w_ada.shape[2]

    msg = jnp.concatenate([c.reshape(8, 128), gla_w_gate2[0].reshape(8, 128), _pad_rows(gla_b_gate, 1),
                           gla_g_norm.reshape(1, 128), jnp.zeros((6, 128), F32)])
    got = _exchange(msg, gather=True, name="gather_small")
    c_all = got[:, 0:8].reshape(N_DEV, D_MODEL)
    wg2 = got[:, 8:16].reshape(N_DEV, GLA_RANK, 64).transpose(1, 0, 2).reshape(GLA_RANK, GLA_QK)
    bg = got[:, 16, :64].reshape(1, GLA_QK)
    gn = got[:, 17, :].reshape(1, D_MODEL)

    b_cols = lax.dynamic_slice_in_dim(b_ada, me * ada_w, ada_w, axis=1)
    mod_cols = _ada_fwd(c_all, w_ada, b_cols)
    pay = jnp.pad(mod_cols.transpose(1, 0, 2).reshape(N_DEV, 12, 128), ((0, 0), (0, 4), (0, 0)))
    mod = _exchange(pay, gather=False, name="a2a_mod")[:, :12].reshape(N_DEV, 2, ada_w).transpose(1, 0, 2).reshape(2, 6 * D_MODEL)
    mods = [[mod[i:i + 1, j * D_MODEL:(j + 1) * D_MODEL] for j in range(6)] for i in range(2)]

    big_w = [s5_w_glu[0], gla_w_in[0], gla_w_out[0], w_ff1[0], w_ff1[1], w_ff2[0], w_ff2[1]]
    big_m = [m_s5_w_glu[0], m_gla_w_in[0], m_gla_w_out[0], m_w_ff1[0], m_w_ff1[1], m_w_ff2[0], m_w_ff2[1]]
    big_v = [v_s5_w_glu[0], v_gla_w_in[0], v_gla_w_out[0], v_w_ff1[0], v_w_ff1[1], v_w_ff2[0], v_w_ff2[1]]
    w = dict(wg2=jnp.pad(wg2, ((0, GLA_RANK_PAD - GLA_RANK), (0, 0))), bg=bg, gn=gn, tri=jnp.tril(jnp.ones((GLA_CHUNK, GLA_CHUNK), F32)))
    core = ic.reshape(1).astype(jnp.int32)
    chip = (2 * ix + iy).reshape(1).astype(jnp.int32)
    nrm = dict(mix=[norm_mix[i:i + 1] for i in range(2)], mlp=[norm_mlp[i:i + 1] for i in range(2)], final=norm_final.reshape(1, D_MODEL))
    e01 = (lax.broadcasted_iota(jnp.int32, (S5_GROUPS, S5_NSTATE), 1) // S5_STATE
           == lax.broadcasted_iota(jnp.int32, (S5_GROUPS, S5_NSTATE), 0)).astype(F32)
    s5p = dict(a_re=s5_a_re.reshape(1, S5_NSTATE), a_im=s5_a_im.reshape(1, S5_NSTATE), log_dt=s5_log_dt,
               bt_re=s5_b_re[0].transpose(2, 0, 1).reshape(S5_GROUP, S5_NSTATE),
               bt_im=s5_b_im[0].transpose(2, 0, 1).reshape(S5_GROUP, S5_NSTATE),
               c_re=s5_c_re[0], c_im=s5_c_im[0], d=s5_d, e01=e01)

    loss_local, grad_x, g, chip_sum, from_chips = _local_step(
        x[0], loss_target[0], mods, nrm, s5p, w, [a.astype(MXU_DT) for a in big_w], core)
    loss = lax.psum(loss_local, ("x", "y", "c"))
    big = []
    for i in range(len(big_w)):
        parts = [(chip_sum[i], None), (from_chips[i], 0), (from_chips[i], 1), (from_chips[i], 2)]
        big.append(_adamw(big_w[i], big_m[i], big_v[i], parts, tile=min(512, big_w[i].shape[0]), name="adamw_big%d" % i, sel=chip))

    rep = [jnp.concatenate([g["mix0"], g["mix1"]]), jnp.concatenate([g["mlp0"], g["mlp1"]]), g["s5_a_re"], g["s5_a_im"], g["s5_log_dt"],
           g["s5_bt_re"].reshape(S5_GROUP, S5_GROUPS, S5_STATE).transpose(1, 2, 0), g["s5_bt_im"].reshape(S5_GROUP, S5_GROUPS, S5_STATE).transpose(1, 2, 0),
           g["s5_c_re"], g["s5_c_im"], g["s5_d"], g["nf"]]
    rep_shapes = [(2, D_MODEL), (2, D_MODEL), (1, 64, 64), (1, 64, 64), (1, 64), (1, 64, 64, 16), (1, 64, 64, 16), (1, 64, 16, 64), (1, 64, 16, 64), (1, D_MODEL), (D_MODEL,)]
    rep_flat = jnp.concatenate([a.reshape(-1) for a in rep])
    rep_blk = jnp.pad(rep_flat, (0, N_DEV * _REP_ROWS * 128 - _REP_SIZE)).reshape(N_DEV, _REP_ROWS, 128)
    dmod = jnp.stack([jnp.concatenate([g["sh1" + t], g["sc1" + t], g["gt1" + t], g["sh2" + t], g["sc2" + t], g["gt2" + t]], axis=1)[0] for t in "ab"])
    msg = jnp.concatenate([
        rep_blk,
        g["wg2"][:GLA_RANK].reshape(GLA_RANK, N_DEV, 64).transpose(1, 0, 2).reshape(N_DEV, 8, 128),
        jnp.pad(g["bg"].reshape(N_DEV, 1, 64), ((0, 0), (0, 0), (0, 64))),
        g["gn"].reshape(N_DEV, 1, 128),
        dmod.reshape(2, N_DEV, ada_w).transpose(1, 0, 2).reshape(N_DEV, 12, 128),
        jnp.zeros((N_DEV, 2, 128), F32),
    ], axis=1)
    got = _exchange(msg, gather=False, name="a2a_small_grads")
    tot = _sum_slots(got, name="sum_small_grads")
    dm = got[:, 282:294].reshape(N_DEV, 2, ada_w).transpose(1, 0, 2)
    g_w_ada = _ada_bwd(c_all, dm)
    back = _exchange(jnp.concatenate([tot[0:_REP_ROWS], tot[282:294], jnp.zeros((4, 128), F32)]), gather=True, name="gather_small_grads")
    rep_sum = back[:, :_REP_ROWS].reshape(-1)[:_REP_SIZE]
    g_b_ada = back[:, _REP_ROWS:_REP_ROWS + 12].reshape(N_DEV, 2, ada_w).transpose(1, 0, 2).reshape(2, 6 * D_MODEL)
    g_rep, off = [], 0
    for s in rep_shapes:
        n = math.prod(s)
        g_rep.append(rep_sum[off:off + n].reshape(s))
        off += n
    g_small = g_rep + [g_b_ada, tot[272:280].reshape(GLA_RANK, 64)[None], tot[280, :64][None], tot[281][None]]
    p_small = [norm_mix, norm_mlp, s5_a_re, s5_a_im, s5_log_dt, s5_b_re, s5_b_im, s5_c_re, s5_c_im, s5_d, norm_final, b_ada, gla_w_gate2, gla_b_gate, gla_g_norm]
    m_small = [m_norm_mix, m_norm_mlp, m_s5_a_re, m_s5_a_im, m_s5_log_dt, m_s5_b_re, m_s5_b_im, m_s5_c_re, m_s5_c_im, m_s5_d, m_norm_final, m_b_ada, m_gla_w_gate2, m_gla_b_gate, m_gla_g_norm]
    v_small = [v_norm_mix, v_norm_mlp, v_s5_a_re, v_s5_a_im, v_s5_log_dt, v_s5_b_re, v_s5_b_im, v_s5_c_re, v_s5_c_im, v_s5_d, v_norm_final, v_b_ada, v_gla_w_gate2, v_gla_b_gate, v_gla_g_norm]
    as2d = lambda a: a.reshape(1, -1) if a.ndim == 1 else a
    small = _adamw_multi([as2d(a) for a in p_small], [as2d(a) for a in m_small], [as2d(a) for a in v_small],
                         [as2d(a) for a in g_small], name="adamw_small")
    small = [[o.reshape(p.shape) for o in outs] for outs, p in zip(small, p_small)]
    ada = _adamw(w_ada.reshape(2 * D_MODEL, ada_w), m_w_ada.reshape(2 * D_MODEL, ada_w), v_w_ada.reshape(2 * D_MODEL, ada_w),
                 [g_w_ada.reshape(2 * D_MODEL, ada_w)], tile=512, name="adamw_ada")
    ada = [a.reshape(w_ada.shape) for a in ada]

    def leaves(k):
        nm, nl, a_re, a_im, ldt, b_re, b_im, c_re, c_im, dsk, nf, bada, wg2_, bg_, gn_ = [s[k] for s in small]
        glu_, gin_, gout_, ff1a_, ff1b_, ff2a_, ff2b_ = [b[k] for b in big]
        return [ada[k], bada, nm, nl, a_re, a_im, ldt, b_re, b_im, c_re, c_im, dsk, glu_[None], gin_[None], wg2_, bg_, gn_, gout_[None],
                jnp.stack([ff1a_, ff1b_]), jnp.stack([ff2a_, ff2b_]), nf]

    return (loss, grad_x[None], *leaves(0), *leaves(1), *leaves(2), *leaves(3))
```

```python
import functools
import math

import jax
import jax.numpy as jnp
from jax import lax
from jax.experimental import pallas as pl
from jax.experimental.pallas import tpu as pltpu

F32 = jnp.float32
BF16 = jnp.bfloat16
MXU_DT = BF16
EPS = 1e-6
N_DEV = 8
VMEM_LIMIT = 56 * 1024 * 1024

D_MODEL = 1024
S5_GROUP = 16
S5_GROUPS = 64
S5_STATE = 64
S5_NSTATE = S5_GROUPS * S5_STATE
S5_GB = 16
S5_NB = S5_GROUPS // S5_GB
S5_BC = S5_GB * S5_GROUP
S5_BS = S5_GB * S5_STATE
GLA_HEADS = 4
GLA_QK = 512
GLA_DK = 128
GLA_DV = 256
GLA_RANK = 16
GLA_RANK_PAD = 128
GLA_TAU = 16.0
GLA_CHUNK = 64
GLA_NB = 4
GLA_IN = 3088
GLA_INP = 2 * GLA_QK + 2 * D_MODEL + GLA_RANK_PAD
D_FF = 4096

ADAM_LR = 0.001
ADAM_B1 = 0.9
ADAM_B2 = 0.999
ADAM_EPS = 1e-08
ADAM_WD = 0.01
ADAM_STEP = 10

_NN = (((1,), (0,)), ((), ()))
_NT = (((1,), (1,)), ((), ()))
_TN = (((0,), (0,)), ((), ()))


def _dot(a, b, dn=_NN):
    return lax.dot_general(a.astype(MXU_DT), b.astype(MXU_DT), dn, preferred_element_type=F32)


def _dot_exact01(x, m01, dn=_NN):
    x1 = x.astype(BF16)
    r1 = x - x1.astype(F32)
    x2 = r1.astype(BF16)
    x3 = (r1 - x2.astype(F32)).astype(BF16)
    m = m01.astype(BF16)
    d = lambda u: lax.dot_general(u, m, dn, preferred_element_type=F32)
    return d(x1) + d(x2) + d(x3)


def _dot_01_left(m01, x, dn=_NN):
    x1 = x.astype(BF16)
    r1 = x - x1.astype(F32)
    x2 = r1.astype(BF16)
    x3 = (r1 - x2.astype(F32)).astype(BF16)
    m = m01.astype(BF16)
    d = lambda u: lax.dot_general(m, u, dn, preferred_element_type=F32)
    return d(x1) + d(x2) + d(x3)


def _sigmoid(x):
    return 1.0 / (1.0 + jnp.exp(-x))


def _silu(x):
    return x * _sigmoid(x)


def _gelu(x):
    return 0.5 * x * (1.0 + jnp.tanh(math.sqrt(2.0 / math.pi) * (x + 0.044715 * (x * x * x))))


def _logsig(x):
    return jnp.minimum(x, 0.0) - jnp.log(1.0 + jnp.exp(-jnp.abs(x)))


def _rms(x):
    return lax.rsqrt(jnp.mean(x * x, axis=-1, keepdims=True) + EPS)


def _params(sem):
    return pltpu.CompilerParams(dimension_semantics=sem, vmem_limit_bytes=VMEM_LIMIT)


def _mm(a, b, dims, out_dtype, *, tm, tn, name, a_fn=None, out_fn=None, extra=None, by_owner=False, comm=None):
    if dims == "tn":
        k, m = a.shape
        n = b.shape[1]
    else:
        m, k = a.shape
        n = b.shape[0] if dims == "nt" else b.shape[1]
    tm, tn = min(tm, m), min(tn, n)
    assert m % tm == 0 and n % tn == 0, (name, m, n, tm, tn)
    dn = {"nn": _NN, "nt": _NT, "tn": _TN}[dims]
    n_in = 2 if extra is None else 3
    nci, nco = (len(comm.ins), len(comm.out_shapes)) if comm is not None else (0, 0)

    def body(*refs):
        a_ref, b_ref = refs[0], refs[1]
        o_ref = refs[n_in + nci]
        if comm is not None:
            run_comm = functools.partial(
                comm.run, pl.program_id(0) * (n // tn) + pl.program_id(1), (m // tm) * (n // tn), refs[n_in:n_in + nci],
                refs[n_in + nci + 1:n_in + nci + 1 + nco], refs[n_in + nci + 1 + nco:])
            run_comm(("first", "mid"))
        av = a_ref[...]
        if a_fn is not None:
            av = a_fn(av.astype(F32))
        acc = _dot(av, b_ref[...], dn)
        if extra is not None:
            acc = out_fn(acc, refs[2][...].astype(F32))
        elif out_fn is not None:
            acc = out_fn(acc)
        if by_owner:
            o_ref[0] = acc.astype(o_ref.dtype)
        else:
            o_ref[...] = acc.astype(o_ref.dtype)
        if comm is not None:
            run_comm(("last",))

    a_spec = pl.BlockSpec((k, tm), lambda i, j: (0, i)) if dims == "tn" else pl.BlockSpec((tm, k), lambda i, j: (i, 0))
    b_spec = pl.BlockSpec((tn, k), lambda i, j: (j, 0)) if dims == "nt" else pl.BlockSpec((k, tn), lambda i, j: (0, j))
    if by_owner:
        o_spec = pl.BlockSpec((1, tm, tn), lambda i, j: (j, i, 0))
        out_shape = jax.ShapeDtypeStruct((n // tn, m, tn), out_dtype)
    else:
        o_spec = pl.BlockSpec((tm, tn), lambda i, j: (i, j))
        out_shape = jax.ShapeDtypeStruct((m, n), out_dtype)
    in_specs, args = [a_spec, b_spec], [a, b]
    if extra is not None:
        in_specs.append(o_spec)
        args.append(extra)
    if comm is None:
        return pl.pallas_call(
            body, grid=(m // tm, n // tn), in_specs=in_specs, out_specs=o_spec, out_shape=out_shape, name=name,
            compiler_params=_params(("parallel", "parallel")),
        )(*args)
    return pl.pallas_call(
        body, grid=(m // tm, n // tn), in_specs=in_specs + [_ANY_SPEC] * nci, out_specs=[o_spec] + [_ANY_SPEC] * nco,
        out_shape=[out_shape] + comm.out_shapes, scratch_shapes=comm.scratch, name=name,
        compiler_params=_params(("arbitrary", "arbitrary")),
    )(*args, *comm.ins)


def _rowcall(f, rows, pars, outs, accs, *, tile, name):
    length = rows[0][0].shape[0]
    tile = min(tile, length)
    nr, npar, no = len(rows), len(pars), len(outs)

    def body(*refs):
        vals = [r[...].astype(F32) for r in refs[:nr + npar]]
        res = f(*vals)
        o_refs = refs[nr + npar:nr + npar + no]
        a_refs = refs[nr + npar + no:]
        for o, v in zip(o_refs, res[:no]):
            o[...] = v.astype(o.dtype)
        if a_refs:
            @pl.when(pl.program_id(0) == 0)
            def _():
                for a in a_refs:
                    a[...] = jnp.zeros(a.shape, F32)
            for a, v in zip(a_refs, res[no:]):
                a[...] += jnp.broadcast_to(v, a.shape)

    in_specs = [pl.BlockSpec((tile, w), lambda i, cb=cb: (i, cb)) for (_, w, cb) in rows]
    in_specs += [pl.BlockSpec(p.shape, lambda i: (0, 0)) for p in pars]
    out_specs = [pl.BlockSpec((tile, w), lambda i: (i, 0)) for (w, _) in outs]
    out_specs += [pl.BlockSpec(s, lambda i: (0, 0)) for s in accs]
    out_shape = [jax.ShapeDtypeStruct((length, w), dt) for (w, dt) in outs]
    out_shape += [jax.ShapeDtypeStruct(s, F32) for s in accs]
    return pl.pallas_call(
        body, grid=(length // tile,), in_specs=in_specs, out_specs=out_specs, out_shape=out_shape, name=name,
        compiler_params=_params(("arbitrary",)),
    )(*[r[0] for r in rows], *pars)


def _vjp_of(f, n_row, n_cot, row_want):
    def g(*a):
        prow, cots, par = a[:n_row], a[n_row:n_row + n_cot], a[n_row + n_cot:]
        _, vjp = jax.vjp(f, *prow, *par)
        grads = vjp(tuple(cots))
        return tuple(grads[i] for i in row_want) + tuple(grads[n_row:])
    return g


def _f_pn(x, g, sc, sh):
    return (x, x * _rms(x) * g * (1.0 + sc) + sh)


def _f_res_pn(x, y, gt, g, sc, sh):
    xn = x + gt * y
    return (xn, xn * _rms(xn) * g * (1.0 + sc) + sh)


def _f_glu_res_pn(x, val, gate, gt, g, sc, sh):
    xn = x + gt * (val * _sigmoid(gate))
    return (xn, xn * _rms(xn) * g * (1.0 + sc) + sh)


def _f_final(x, y, tgt, gt, g):
    xn = x + gt * y
    err = xn * _rms(xn) * g - tgt
    return 0.5 * jnp.mean(err * err, axis=-1, keepdims=True)


def _g_final(x, y, tgt, gt, g):
    lrow, vjp = jax.vjp(_f_final, x, y, tgt, gt, g)
    dx, dy, _, dgt, dg = vjp(jnp.ones_like(lrow))
    return dx, dy, dgt, dg, jnp.sum(lrow)


def _full(a):
    return (a, a.shape[1], 0)


def _s5_prep_f(a_re, a_im, log_dt, bt_re, bt_im, e01):
    dt = jnp.exp(_dot_exact01(log_dt, e01))
    mag = jnp.exp(a_re * dt)
    ph = a_im * dt
    lb_re = mag * jnp.cos(ph)
    lb_im = mag * jnp.sin(ph)
    den = a_re * a_re + a_im * a_im
    nr = lb_re - 1.0
    ni = lb_im
    f_re = (nr * a_re + ni * a_im) / den
    f_im = (ni * a_re - nr * a_im) / den
    bb_re = f_re * bt_re - f_im * bt_im
    bb_im = f_re * bt_im + f_im * bt_re
    return lb_re, lb_im, bb_re, bb_im


def _s5_prep_outs():
    return [jax.ShapeDtypeStruct((1, S5_NSTATE), F32)] * 2 + [jax.ShapeDtypeStruct((S5_GROUP, S5_NSTATE), F32)] * 2


def _s5_prep(a_re, a_im, log_dt, bt_re, bt_im, e01):
    def body(*refs):
        res = _s5_prep_f(*[r[...] for r in refs[:6]])
        for o, v in zip(refs[6:], res):
            o[...] = v
    return pl.pallas_call(body, out_shape=_s5_prep_outs(), name="s5_prep",
                          compiler_params=pltpu.CompilerParams(vmem_limit_bytes=VMEM_LIMIT))(a_re, a_im, log_dt, bt_re, bt_im, e01)


def _s5_prep_bwd(a_re, a_im, log_dt, bt_re, bt_im, e01, d_lb_re, d_lb_im, d_bb_re, d_bb_im):
    def f(a_re, a_im, log_dt, bt_re, bt_im, e01):
        @jax.custom_vjp
        def expand(v):
            return _dot_exact01(v, e01)
        expand.defvjp(lambda v: (_dot_exact01(v, e01), None), lambda _, ct: (_dot_exact01(ct, e01, _NT),))
        dt = jnp.exp(expand(log_dt))
        mag = jnp.exp(a_re * dt)
        ph = a_im * dt
        lb_re = mag * jnp.cos(ph)
        lb_im = mag * jnp.sin(ph)
        den = a_re * a_re + a_im * a_im
        nr = lb_re - 1.0
        f_re = (nr * a_re + lb_im * a_im) / den
        f_im = (lb_im * a_re - nr * a_im) / den
        return lb_re, lb_im, f_re * bt_re - f_im * bt_im, f_re * bt_im + f_im * bt_re

    def body(*refs):
        ins = [r[...] for r in refs[:5]]
        e = refs[5][...]
        cots = tuple(r[...] for r in refs[6:10])
        _, vjp = jax.vjp(lambda *p: f(*p, e), *ins)
        for o, v in zip(refs[10:], vjp(cots)):
            o[...] = v
    outs = [jax.ShapeDtypeStruct(v.shape, F32) for v in (a_re, a_im, log_dt, bt_re, bt_im)]
    return pl.pallas_call(body, out_shape=outs, name="s5_prep_bwd",
                          compiler_params=pltpu.CompilerParams(vmem_limit_bytes=VMEM_LIMIT))(
        a_re, a_im, log_dt, bt_re, bt_im, e01, d_lb_re, d_lb_im, d_bb_re, d_bb_im)


def _s5_scan(x_re, x_im, a_r, a_i, c_r, c_i, n_tiles, reverse):
    sgn = -1.0 if reverse else 1.0

    def tile(k, carry):
        cr, ci = carry
        i = (n_tiles - 1 - k) if reverse else k
        order = range(7, -1, -1) if reverse else range(8)
        for j in order:
            br = x_re[i, pl.ds(j, 1), :]
            bi = x_im[i, pl.ds(j, 1), :]
            nr = a_r * cr - (sgn * a_i) * ci + br
            ni = a_r * ci + (sgn * a_i) * cr + bi
            x_re[i, pl.ds(j, 1), :] = nr
            x_im[i, pl.ds(j, 1), :] = ni
            cr, ci = nr, ni
        return cr, ci

    return lax.fori_loop(0, n_tiles, tile, (c_r, c_i))


def _s5_fwd(u, lam_re, lam_im, bbd_re, bbd_im, cbd_re, cbd_im, d_skip, comm, *, tc):
    length = u.shape[0]
    tc = min(tc, length)
    nt = length // tc
    nci, nco = len(comm.ins), len(comm.out_shapes)

    def body(*refs):
        u_ref, lr_ref, li_ref, br_ref, bi_ref, cr_ref, ci_ref, d_ref = refs[:8]
        z_ref, sr_ref, si_ref = refs[8 + nci:11 + nci]
        xr, xi, car_r, car_i = refs[11 + nci + nco:15 + nci + nco]
        run_comm = functools.partial(comm.run, pl.program_id(0) * nt + pl.program_id(1), S5_NB * nt, refs[8:8 + nci],
                                     refs[11 + nci:11 + nci + nco], refs[15 + nci + nco:])
        run_comm(("first", "mid"))

        @pl.when(pl.program_id(1) == 0)
        def _():
            car_r[...] = jnp.zeros_like(car_r)
            car_i[...] = jnp.zeros_like(car_i)
        sr_ref[0] = car_r[...]
        si_ref[0] = car_i[...]
        uv = u_ref[...]
        xr[...] = _dot(uv, br_ref[0]).reshape(tc // 8, 8, S5_BS)
        xi[...] = _dot(uv, bi_ref[0]).reshape(tc // 8, 8, S5_BS)
        cr, ci = _s5_scan(xr, xi, lr_ref[...], li_ref[...], car_r[...], car_i[...], tc // 8, False)
        car_r[...] = cr
        car_i[...] = ci
        y = (_dot(xr[...].reshape(tc, S5_BS), cr_ref[0]) - _dot(xi[...].reshape(tc, S5_BS), ci_ref[0]) + d_ref[...] * uv)
        z_ref[...] = _gelu(y).astype(z_ref.dtype)
        run_comm(("last",))

    blk_u = pl.BlockSpec((tc, S5_BC), lambda g, t: (t, g))
    blk_l = pl.BlockSpec((1, S5_BS), lambda g, t: (0, g))
    blk_b = pl.BlockSpec((1, S5_BC, S5_BS), lambda g, t: (g, 0, 0))
    blk_c = pl.BlockSpec((1, S5_BS, S5_BC), lambda g, t: (g, 0, 0))
    blk_d = pl.BlockSpec((1, S5_BC), lambda g, t: (0, g))
    blk_s = pl.BlockSpec((1, 1, S5_BS), lambda g, t: (t, 0, g))
    return pl.pallas_call(
        body, grid=(S5_NB, nt),
        in_specs=[blk_u, blk_l, blk_l, blk_b, blk_b, blk_c, blk_c, blk_d] + [_ANY_SPEC] * nci,
        out_specs=[blk_u, blk_s, blk_s] + [_ANY_SPEC] * nco,
        out_shape=[jax.ShapeDtypeStruct((length, D_MODEL), MXU_DT),
                   jax.ShapeDtypeStruct((nt, 1, S5_NSTATE), F32), jax.ShapeDtypeStruct((nt, 1, S5_NSTATE), F32)] + comm.out_shapes,
        scratch_shapes=[pltpu.VMEM((tc // 8, 8, S5_BS), F32), pltpu.VMEM((tc // 8, 8, S5_BS), F32),
                        pltpu.VMEM((1, S5_BS), F32), pltpu.VMEM((1, S5_BS), F32)] + comm.scratch,
        name="s5_fwd", compiler_params=_params(("arbitrary", "arbitrary")),
    )(u, lam_re, lam_im, bbd_re, bbd_im, cbd_re, cbd_im, d_skip, *comm.ins)


def _s5_bwd(u, dz, st_re, st_im, lam_re, lam_im, bbd_re, bbd_im, cbd_re, cbd_im, d_skip, comm, *, tc):
    length = u.shape[0]
    tc = min(tc, length)
    nt = length // tc
    nci, nco = len(comm.ins), len(comm.out_shapes)

    def body(*refs):
        u_ref, dz_ref, sr_ref, si_ref, lr_ref, li_ref, br_ref, bi_ref, cr_ref, ci_ref, d_ref = refs[:11]
        du_ref, dbr_ref, dbi_ref, dcr_ref, dci_ref, dlr_ref, dli_ref, dd_ref = refs[11 + nci:19 + nci]
        xr, xi, gr, gi, car_r, car_i = refs[19 + nci + nco:25 + nci + nco]
        run_comm = functools.partial(comm.run, pl.program_id(0) * nt + pl.program_id(1), S5_NB * nt, refs[11:11 + nci],
                                     refs[19 + nci:19 + nci + nco], refs[25 + nci + nco:])
        run_comm(("first", "mid"))

        @pl.when(pl.program_id(1) == 0)
        def _():
            car_r[...] = jnp.zeros_like(car_r)
            car_i[...] = jnp.zeros_like(car_i)
            for r in (dbr_ref, dbi_ref, dcr_ref, dci_ref, dlr_ref, dli_ref, dd_ref):
                r[...] = jnp.zeros(r.shape, F32)
        a_r, a_i = lr_ref[...], li_ref[...]
        uv = u_ref[...]
        xr[...] = _dot(uv, br_ref[0]).reshape(tc // 8, 8, S5_BS)
        xi[...] = _dot(uv, bi_ref[0]).reshape(tc // 8, 8, S5_BS)
        _s5_scan(xr, xi, a_r, a_i, sr_ref[0], si_ref[0], tc // 8, False)
        xrv = xr[...].reshape(tc, S5_BS)
        xiv = xi[...].reshape(tc, S5_BS)
        y = _dot(xrv, cr_ref[0]) - _dot(xiv, ci_ref[0]) + d_ref[...] * uv
        _, gelu_vjp = jax.vjp(_gelu, y)
        dy = gelu_vjp(dz_ref[...].astype(F32))[0]
        dd_ref[...] += jnp.sum(dy * uv, axis=0, keepdims=True)
        dcr_ref[0] += _dot(xrv, dy, _TN)
        dci_ref[0] -= _dot(xiv, dy, _TN)
        gr[...] = _dot(dy, cr_ref[0], _NT).reshape(tc // 8, 8, S5_BS)
        gi[...] = (-_dot(dy, ci_ref[0], _NT)).reshape(tc // 8, 8, S5_BS)
        cr, ci = _s5_scan(gr, gi, a_r, a_i, car_r[...], car_i[...], tc // 8, True)
        car_r[...] = cr
        car_i[...] = ci
        grv = gr[...].reshape(tc, S5_BS)
        giv = gi[...].reshape(tc, S5_BS)
        first = lax.broadcasted_iota(jnp.int32, (tc, 1), 0) == 0
        xpr = jnp.where(first, sr_ref[0], pltpu.roll(xrv, 1, 0))
        xpi = jnp.where(first, si_ref[0], pltpu.roll(xiv, 1, 0))
        dlr_ref[...] += jnp.sum(grv * xpr + giv * xpi, axis=0, keepdims=True)
        dli_ref[...] += jnp.sum(giv * xpr - grv * xpi, axis=0, keepdims=True)
        dbr_ref[0] += _dot(uv, grv, _TN)
        dbi_ref[0] += _dot(uv, giv, _TN)
        du_ref[...] = _dot(grv, br_ref[0], _NT) + _dot(giv, bi_ref[0], _NT) + d_ref[...] * dy
        run_comm(("last",))

    rev = lambda t: nt - 1 - t
    blk_u = pl.BlockSpec((tc, S5_BC), lambda g, t: (rev(t), g))
    blk_l = pl.BlockSpec((1, S5_BS), lambda g, t: (0, g))
    blk_b = pl.BlockSpec((1, S5_BC, S5_BS), lambda g, t: (g, 0, 0))
    blk_c = pl.BlockSpec((1, S5_BS, S5_BC), lambda g, t: (g, 0, 0))
    blk_d = pl.BlockSpec((1, S5_BC), lambda g, t: (0, g))
    blk_s = pl.BlockSpec((1, 1, S5_BS), lambda g, t: (rev(t), 0, g))
    return pl.pallas_call(
        body, grid=(S5_NB, nt),
        in_specs=[blk_u, blk_u, blk_s, blk_s, blk_l, blk_l, blk_b, blk_b, blk_c, blk_c, blk_d] + [_ANY_SPEC] * nci,
        out_specs=[blk_u, blk_b, blk_b, blk_c, blk_c, blk_l, blk_l, blk_d] + [_ANY_SPEC] * nco,
        out_shape=[jax.ShapeDtypeStruct((length, D_MODEL), F32),
                   jax.ShapeDtypeStruct((S5_NB, S5_BC, S5_BS), F32), jax.ShapeDtypeStruct((S5_NB, S5_BC, S5_BS), F32),
                   jax.ShapeDtypeStruct((S5_NB, S5_BS, S5_BC), F32), jax.ShapeDtypeStruct((S5_NB, S5_BS, S5_BC), F32),
                   jax.ShapeDtypeStruct((1, S5_NSTATE), F32), jax.ShapeDtypeStruct((1, S5_NSTATE), F32),
                   jax.ShapeDtypeStruct((1, D_MODEL), F32)] + comm.out_shapes,
        scratch_shapes=[pltpu.VMEM((tc // 8, 8, S5_BS), F32), pltpu.VMEM((tc // 8, 8, S5_BS), F32),
                        pltpu.VMEM((tc // 8, 8, S5_BS), F32), pltpu.VMEM((tc // 8, 8, S5_BS), F32),
                        pltpu.VMEM((1, S5_BS), F32), pltpu.VMEM((1, S5_BS), F32)] + comm.scratch,
        name="s5_bwd", compiler_params=_params(("arbitrary", "arbitrary")),
    )(u, dz, st_re, st_im, lam_re, lam_im, bbd_re, bbd_im, cbd_re, cbd_im, d_skip, *comm.ins)


def _blockdiag_b(bt):
    eye = jnp.eye(S5_GB, dtype=bt.dtype)
    t = bt.reshape(S5_GROUP, S5_NB, S5_GB, S5_STATE)
    return jnp.einsum("ab,hnbp->nahbp", eye, t).reshape(S5_NB, S5_BC, S5_BS)


def _unblockdiag_b(m):
    eye = jnp.eye(S5_GB, dtype=m.dtype)
    t = m.reshape(S5_NB, S5_GB, S5_GROUP, S5_GB, S5_STATE)
    return jnp.einsum("ab,nahbp->hnbp", eye, t).reshape(S5_GROUP, S5_NSTATE)


def _blockdiag_c(c):
    eye = jnp.eye(S5_GB, dtype=c.dtype)
    t = c.reshape(S5_NB, S5_GB, S5_GROUP, S5_STATE)
    return jnp.einsum("ab,nbhp->napbh", eye, t).reshape(S5_NB, S5_BS, S5_BC)


def _unblockdiag_c(m):
    eye = jnp.eye(S5_GB, dtype=m.dtype)
    t = m.reshape(S5_NB, S5_GB, S5_STATE, S5_GB, S5_GROUP)
    return jnp.einsum("ab,napbh->nbhp", eye, t).reshape(S5_GROUPS, S5_GROUP, S5_STATE)


def _chunk_rows(c):
    return slice(c * GLA_CHUNK, (c + 1) * GLA_CHUNK)


def _per_chunk(rows, nb):
    return jnp.concatenate([jnp.broadcast_to(r, (GLA_CHUNK, r.shape[1])) for r in rows], axis=0)


def _gla_gates(glr, wg2, bg, tri, nb):
    pre = _dot(glr, wg2) + bg
    la = _logsig(pre) * (1.0 / GLA_TAU)
    gc = _dot_01_left(tri, la)
    gend = _per_chunk([gc[(c + 1) * GLA_CHUNK - 1:(c + 1) * GLA_CHUNK, :] for c in range(nb)], nb)
    return pre, jnp.exp(gend - gc), jnp.exp(gend)


def _gla_specs(nblk, nb, rev):
    ix = (lambda n: nblk - 1 - n) if rev else (lambda n: n)
    c = GLA_CHUNK * nb
    return dict(
        q=pl.BlockSpec((c, GLA_QK), lambda n: (ix(n), 0)),
        k=pl.BlockSpec((c, GLA_QK), lambda n: (ix(n), 1)),
        v=pl.BlockSpec((c, D_MODEL), lambda n: (ix(n), 1)),
        r=pl.BlockSpec((c, D_MODEL), lambda n: (ix(n), 2)),
        glr=pl.BlockSpec((c, GLA_RANK_PAD), lambda n: (ix(n), (2 * GLA_QK + 2 * D_MODEL) // GLA_RANK_PAD)),
        wg2=pl.BlockSpec((GLA_RANK_PAD, GLA_QK), lambda n: (0, 0)),
        bg=pl.BlockSpec((1, GLA_QK), lambda n: (0, 0)),
        gn=pl.BlockSpec((1, D_MODEL), lambda n: (0, 0)),
        tri=pl.BlockSpec((c, c), lambda n: (0, 0)),
        row=pl.BlockSpec((c, D_MODEL), lambda n: (ix(n), 0)),
        rowp=pl.BlockSpec((c, GLA_INP), lambda n: (ix(n), 0)),
        st=pl.BlockSpec((nb, GLA_HEADS, GLA_DV, GLA_DK), lambda n: (ix(n), 0, 0, 0)),
    )


def _gla_fwd(proj, wg2, bg, gn, tri, comm):
    length = proj.shape[0]
    nc = length // GLA_CHUNK
    nb = tri.shape[0] // GLA_CHUNK
    nblk = nc // nb
    scale = GLA_DK ** -0.5
    nci, nco = len(comm.ins), len(comm.out_shapes)

    def body(*refs):
        q_ref, k_ref, v_ref, r_ref, glr_ref, wg2_ref, bg_ref, gn_ref, tri_ref = refs[:9]
        og_ref, sp_ref = refs[9 + nci:11 + nci]
        st = refs[11 + nci + nco]
        run_comm = functools.partial(comm.run, pl.program_id(0), nblk, refs[9:9 + nci], refs[11 + nci:11 + nci + nco],
                                     refs[12 + nci + nco:])
        run_comm(("first", "mid"))

        @pl.when(pl.program_id(0) == 0)
        def _():
            st[...] = jnp.zeros_like(st)
        _, e, dec = _gla_gates(glr_ref[...], wg2_ref[...], bg_ref[...], tri_ref[...], nb)
        kd = k_ref[...].astype(F32) * e
        q = q_ref[...].astype(F32) * scale
        for h in range(GLA_HEADS):
            sk = slice(h * GLA_DK, (h + 1) * GLA_DK)
            sv = slice(h * GLA_DV, (h + 1) * GLA_DV)
            state = st[h]
            for c in range(nb):
                rows = _chunk_rows(c)
                sp_ref[c, h] = state
                state = dec[c * GLA_CHUNK:c * GLA_CHUNK + 1, sk] * state + _dot(v_ref[rows, sv], kd[rows, sk], _TN)
                o = _dot(q[rows, sk], state, _NT)
                on = o * _rms(o)
                og_ref[rows, sv] = (on * gn_ref[:, sv] * _silu(r_ref[rows, sv].astype(F32))).astype(og_ref.dtype)
            st[h] = state
        run_comm(("last",))

    s = _gla_specs(nblk, nb, False)
    return pl.pallas_call(
        body, grid=(nblk,),
        in_specs=[s["q"], s["k"], s["v"], s["r"], s["glr"], s["wg2"], s["bg"], s["gn"], s["tri"]] + [_ANY_SPEC] * nci,
        out_specs=[s["row"], s["st"]] + [_ANY_SPEC] * nco,
        out_shape=[jax.ShapeDtypeStruct((length, D_MODEL), MXU_DT),
                   jax.ShapeDtypeStruct((nc, GLA_HEADS, GLA_DV, GLA_DK), F32)] + comm.out_shapes,
        scratch_shapes=[pltpu.VMEM((GLA_HEADS, GLA_DV, GLA_DK), F32)] + comm.scratch,
        name="gla_fwd", compiler_params=_params(("arbitrary",)),
    )(proj, proj, proj, proj, proj, wg2, bg, gn, tri, *comm.ins)


def _gla_bwd(proj, d_og, s_prev, wg2, bg, gn, tri):
    length = proj.shape[0]
    nc = length // GLA_CHUNK
    nb = tri.shape[0] // GLA_CHUNK
    nblk = nc // nb
    scale = GLA_DK ** -0.5

    def body(q_ref, k_ref, v_ref, r_ref, glr_ref, dog_ref, sp_ref, wg2_ref, bg_ref, gn_ref, tri_ref,
             dp_ref, dwg2_ref, dbg_ref, dgn_ref, dst):
        dq_ref = dp_ref.at[:, pl.ds(0, GLA_QK)]
        dk_ref = dp_ref.at[:, pl.ds(GLA_QK, GLA_QK)]
        dv_ref = dp_ref.at[:, pl.ds(2 * GLA_QK, D_MODEL)]
        dr_ref = dp_ref.at[:, pl.ds(2 * GLA_QK + D_MODEL, D_MODEL)]
        dglr_ref = dp_ref.at[:, pl.ds(2 * GLA_QK + 2 * D_MODEL, GLA_RANK_PAD)]

        @pl.when(pl.program_id(0) == 0)
        def _():
            dst[...] = jnp.zeros_like(dst)
            dwg2_ref[...] = jnp.zeros_like(dwg2_ref)
            dbg_ref[...] = jnp.zeros_like(dbg_ref)
            dgn_ref[...] = jnp.zeros_like(dgn_ref)
        glr = glr_ref[...]
        pre, e, dec = _gla_gates(glr, wg2_ref[...], bg_ref[...], tri_ref[...], nb)
        k = k_ref[...].astype(F32)
        kd = k * e
        q = q_ref[...].astype(F32) * scale
        dkd_heads, ddec_heads = [], []
        for h in range(GLA_HEADS):
            sk = slice(h * GLA_DK, (h + 1) * GLA_DK)
            sv = slice(h * GLA_DV, (h + 1) * GLA_DV)
            gnh = gn_ref[:, sv]
            stps, grads_in = [], []
            dgn = jnp.zeros((1, GLA_DV), F32)
            for c in range(nb):
                rows = _chunk_rows(c)
                stp = sp_ref[c, h]
                stn = dec[c * GLA_CHUNK:c * GLA_CHUNK + 1, sk] * stp + _dot(v_ref[rows, sv], kd[rows, sk], _TN)
                o = _dot(q[rows, sk], stn, _NT)
                rinv = _rms(o)
                on = o * rinv
                rv = r_ref[rows, sv].astype(F32)
                sg = _sigmoid(rv)
                dog = dog_ref[rows, sv].astype(F32)
                d_ong = dog * (rv * sg)
                dr_ref[rows, sv] = (dog * (on * gnh) * (sg * (1.0 + rv * (1.0 - sg)))).astype(dr_ref.dtype)
                dgn = dgn + jnp.sum(d_ong * on, axis=0, keepdims=True)
                d_on = d_ong * gnh
                do = rinv * (d_on - on * jnp.mean(d_on * on, axis=-1, keepdims=True))
                dq_ref[rows, sk] = (_dot(do, stn) * scale).astype(dq_ref.dtype)
                stps.append(stp)
                grads_in.append(_dot(do, q[rows, sk], _TN))
            dgn_ref[:, sv] += dgn
            carry = dst[h]
            dkd_rows, ddec_rows = [None] * nb, [None] * nb
            for c in reversed(range(nb)):
                rows = _chunk_rows(c)
                dstn = carry + grads_in[c]
                carry = dec[c * GLA_CHUNK:c * GLA_CHUNK + 1, sk] * dstn
                ddec_rows[c] = jnp.sum(dstn * stps[c], axis=0, keepdims=True)
                dv_ref[rows, sv] = _dot(kd[rows, sk], dstn, _NT).astype(dv_ref.dtype)
                dkd_rows[c] = _dot(v_ref[rows, sv], dstn)
            dst[h] = carry
            dkd_heads.append(jnp.concatenate(dkd_rows, axis=0))
            ddec_heads.append(_per_chunk(ddec_rows, nb))
        dkd = jnp.concatenate(dkd_heads, axis=1)
        ddec = jnp.concatenate(ddec_heads, axis=1)
        dk_ref[...] = (dkd * e).astype(dk_ref.dtype)
        w = dkd * kd
        dgend = _per_chunk([jnp.sum(w[_chunk_rows(c)], axis=0, keepdims=True) for c in range(nb)], nb) + ddec * dec
        dla = dgend - _dot_01_left(tri_ref[...], w, _TN)
        dpre = dla * (1.0 - _sigmoid(pre)) * (1.0 / GLA_TAU)
        dwg2_ref[...] += _dot(glr, dpre, _TN)
        dbg_ref[...] += jnp.sum(dpre, axis=0, keepdims=True)
        dglr_ref[...] = _dot(dpre, wg2_ref[...], _NT).astype(dglr_ref.dtype)

    s = _gla_specs(nblk, nb, True)
    return pl.pallas_call(
        body, grid=(nblk,),
        in_specs=[s["q"], s["k"], s["v"], s["r"], s["glr"], s["row"], s["st"], s["wg2"], s["bg"], s["gn"], s["tri"]],
        out_specs=[s["rowp"], s["wg2"], s["bg"], s["gn"]],
        out_shape=[jax.ShapeDtypeStruct((length, GLA_INP), MXU_DT),
                   jax.ShapeDtypeStruct((GLA_RANK_PAD, GLA_QK), F32), jax.ShapeDtypeStruct((1, GLA_QK), F32),
                   jax.ShapeDtypeStruct((1, D_MODEL), F32)],
        scratch_shapes=[pltpu.VMEM((GLA_HEADS, GLA_DV, GLA_DK), F32)],
        name="gla_bwd", compiler_params=_params(("arbitrary",)),
    )(proj, proj, proj, proj, proj, d_og, s_prev, wg2, bg, gn, tri)


def _local_step(x, tgt, mods, nrm, s5p, w, shards, core, *, row_tile=256, s5_tc=256):
    length = x.shape[0]
    tmm = 512
    glu_sh, gin_sh, gout_sh, ff1a_sh, ff1b_sh, ff2a_sh, ff2b_sh = shards
    w = dict(w)
    rc = functools.partial(_rowcall, tile=row_tile)
    (sh1a, sc1a, gt1a, sh2a, sc2a, gt2a), (sh1b, sc1b, gt1b, sh2b, sc2b, gt2b) = mods
    vec = (1, D_MODEL)
    row32, row16 = (D_MODEL, F32), (D_MODEL, MXU_DT)

    (h0,) = rc(lambda *a: _f_pn(*a)[1:], [_full(x)], [nrm["mix"][0], sc1a, sh1a], [row32], [], name="pn0")
    lam_re, lam_im, bb_re, bb_im = _s5_prep(s5p["a_re"], s5p["a_im"], s5p["log_dt"], s5p["bt_re"], s5p["bt_im"], s5p["e01"])
    bbd_re = _blockdiag_b(bb_re.reshape(S5_GROUP, S5_GROUPS, S5_STATE)).astype(MXU_DT)
    bbd_im = _blockdiag_b(bb_im.reshape(S5_GROUP, S5_GROUPS, S5_STATE)).astype(MXU_DT)
    cbd_re = _blockdiag_c(s5p["c_re"]).astype(MXU_DT)
    cbd_im = _blockdiag_c(s5p["c_im"]).astype(MXU_DT)
    z0, st_re, st_im, glu_s, ff1a_s, ff2a_s = _s5_fwd(
        h0, lam_re, lam_im, bbd_re, bbd_im, cbd_re, cbd_im, s5p["d"], _ag_comm([glu_sh, ff1a_sh, ff2a_sh]), tc=s5_tc)
    w["glu"] = glu_s.transpose(1, 0, 2).reshape(D_MODEL, 2 * D_MODEL)
    w["ff1"] = [ff1a_s.transpose(1, 0, 2).reshape(D_MODEL, D_FF), None]
    w["ff2"] = [ff2a_s.reshape(D_FF, D_MODEL), None]
    vg = _mm(z0, w["glu"], "nn", F32, tm=tmm, tn=2048, name="glu_mm")
    x1, h1 = rc(_f_glu_res_pn, [_full(x), (vg, D_MODEL, 0), (vg, D_MODEL, 1)], [gt1a, nrm["mlp"][0], sc2a, sh2a],
                [row32, row16], [], name="node1")
    relu = lambda acc: jnp.maximum(acc, 0.0)
    sq = lambda a: a * a
    a0, gin_s = _mm(h1, w["ff1"][0], "nn", MXU_DT, tm=tmm, tn=2048, name="ff1a", out_fn=relu, comm=_ag_comm([gin_sh]))
    f0, ff1b_s = _mm(a0, w["ff2"][0], "nn", F32, tm=tmm, tn=1024, name="ff2a", a_fn=sq, comm=_ag_comm([ff1b_sh]))
    gin_full = gin_s.transpose(1, 0, 2).reshape(D_MODEL, GLA_IN)
    q_, k_, v_, glr_, r_ = jnp.split(gin_full, [GLA_QK, 2 * GLA_QK, 2 * GLA_QK + D_MODEL, 2 * GLA_QK + D_MODEL + GLA_RANK], axis=1)
    w["gin"] = jnp.concatenate([q_, k_, v_, r_, glr_, jnp.zeros((D_MODEL, GLA_RANK_PAD - GLA_RANK), MXU_DT)], axis=1)
    w["ff1"][1] = ff1b_s.transpose(1, 0, 2).reshape(D_MODEL, D_FF)
    x2, h2 = rc(_f_res_pn, [_full(x1), _full(f0)], [gt2a, nrm["mix"][1], sc1b, sh1b], [row32, row16], [], name="node2")
    proj = _mm(h2, w["gin"], "nn", MXU_DT, tm=tmm, tn=GLA_INP, name="gla_in")
    og, s_prev, gout_s, ff2b_s = _gla_fwd(proj, w["wg2"], w["bg"], w["gn"], w["tri"], _ag_comm([gout_sh, ff2b_sh]))
    w["gout"] = gout_s.reshape(D_MODEL, D_MODEL)
    w["ff2"][1] = ff2b_s.reshape(D_FF, D_MODEL)
    y1 = _mm(og, w["gout"], "nn", F32, tm=tmm, tn=1024, name="gla_out")
    x3, h3 = rc(_f_res_pn, [_full(x2), _full(y1)], [gt1b, nrm["mlp"][1], sc2b, sh2b], [row32, row16], [], name="node3")
    a1 = _mm(h3, w["ff1"][1], "nn", MXU_DT, tm=tmm, tn=2048, name="ff1b", out_fn=relu)
    f1 = _mm(a1, w["ff2"][1], "nn", F32, tm=tmm, tn=1024, name="ff2b", a_fn=sq)

    g = {}
    dx, df, g["gt2b"], g["nf"], loss = rc(_g_final, [_full(x3), _full(f1), _full(tgt)], [gt2b, nrm["final"]],
                                          [row32, row16], [vec, vec, (8, 128)], name="final")

    def mlp_bwd(df, a, h, w1, w2, tag):
        dw2 = _mm(a, df, "tn", MXU_DT, tm=1024, tn=1024, name="dff2" + tag, a_fn=sq)
        du = _mm(df, w2, "nt", MXU_DT, tm=tmm, tn=2048, name="dact" + tag, extra=a, out_fn=lambda acc, e: acc * (2.0 * e))
        dw1 = _mm(h, du, "tn", MXU_DT, tm=1024, tn=D_FF // N_DEV, name="dff1" + tag, by_owner=True)
        dh = _mm(du, w1, "nt", F32, tm=tmm, tn=1024, name="dh" + tag)
        return dw1, dw2, dh

    def node_bwd(f, prim_rows, cots, pars, row_want, outs, name):
        nrow = len(prim_rows)
        return rc(_vjp_of(f, nrow, len(cots), row_want), prim_rows + cots, pars, outs, [vec] * len(pars), name=name)

    g["ff1b"], g["ff2b"], dh3 = mlp_bwd(df, a1, h3, w["ff1"][1], w["ff2"][1], "b")
    dx, dy1, g["gt1b"], g["mlp1"], g["sc2b"], g["sh2b"] = node_bwd(
        _f_res_pn, [_full(x2), _full(y1)], [_full(dx), _full(dh3)], [gt1b, nrm["mlp"][1], sc2b, sh2b], (0, 1),
        [row32, row16], "node3_bwd")
    g["gout"] = _mm(og, dy1, "tn", MXU_DT, tm=512, tn=1024, name="dgout")
    d_og = _mm(dy1, w["gout"], "nt", MXU_DT, tm=tmm, tn=1024, name="dog")
    dproj, g["wg2"], g["bg"], g["gn"] = _gla_bwd(proj, d_og, s_prev, w["wg2"], w["bg"], w["gn"], w["tri"])
    g["gin"] = _mm(h2, dproj, "tn", MXU_DT, tm=512, tn=640, name="dgin")
    dh2 = _mm(dproj, w["gin"], "nt", F32, tm=tmm, tn=1024, name="dh2")
    dx, df0, g["gt2a"], g["mix1"], g["sc1b"], g["sh1b"] = node_bwd(
        _f_res_pn, [_full(x1), _full(f0)], [_full(dx), _full(dh2)], [gt2a, nrm["mix"][1], sc1b, sh1b], (0, 1),
        [row32, row16], "node2_bwd")
    g["ff1a"], g["ff2a"], dh1 = mlp_bwd(df0, a0, h1, w["ff1"][0], w["ff2"][0], "a")
    glu_vjp = _vjp_of(_f_glu_res_pn, 3, 2, (0, 1, 2))

    def glu_bwd(*a):
        r = glu_vjp(*a)
        return (r[0], jnp.concatenate([r[1], r[2]], axis=1)) + r[3:]

    dx, dvg, g["gt1a"], g["mlp0"], g["sc2a"], g["sh2a"] = rc(
        glu_bwd, [_full(x), (vg, D_MODEL, 0), (vg, D_MODEL, 1), _full(dx), _full(dh1)], [gt1a, nrm["mlp"][0], sc2a, sh2a],
        [row32, (2 * D_MODEL, MXU_DT)], [vec] * 4, name="node1_bwd")
    gin_g = g.pop("gin")
    gin_g = jnp.concatenate([gin_g[:, :2 * GLA_QK + D_MODEL], gin_g[:, GLA_INP - GLA_RANK_PAD:GLA_INP - GLA_RANK_PAD + GLA_RANK],
                             gin_g[:, 2 * GLA_QK + D_MODEL:2 * GLA_QK + 2 * D_MODEL]], axis=1)
    per_owner = [gin_g.reshape(D_MODEL, N_DEV, GLA_IN // N_DEV).transpose(1, 0, 2),
                 g.pop("gout").reshape(N_DEV, D_MODEL // N_DEV, D_MODEL), g.pop("ff1a"), g.pop("ff1b"),
                 g.pop("ff2a").reshape(N_DEV, D_FF // N_DEV, D_MODEL), g.pop("ff2b").reshape(N_DEV, D_FF // N_DEV, D_MODEL)]
    res = _mm(z0, dvg, "tn", MXU_DT, tm=1024, tn=2 * D_MODEL // N_DEV, name="dglu", by_owner=True, comm=_sibling_comm(per_owner))
    glu_g, from_sibling = res[0], list(res[1:])
    dz0, glu_sib = _mm(dvg, w["glu"], "nt", MXU_DT, tm=tmm, tn=1024, name="dz0", comm=_sibling_comm([glu_g]))
    per_owner = [glu_g] + per_owner
    chip_sum = _pair_add(per_owner, [glu_sib] + from_sibling, core, name="rs_add")
    res = _s5_bwd(h0, dz0, st_re, st_im, lam_re, lam_im, bbd_re, bbd_im, cbd_re, cbd_im, s5p["d"], _chips_comm(chip_sum), tc=s5_tc)
    du0, dbbd_re, dbbd_im, dcbd_re, dcbd_im, dlam_re, dlam_im, g["s5_d"] = res[:8]
    from_chips = res[8:]
    g["s5_c_re"] = _unblockdiag_c(dcbd_re)
    g["s5_c_im"] = _unblockdiag_c(dcbd_im)
    g["s5_a_re"], g["s5_a_im"], g["s5_log_dt"], g["s5_bt_re"], g["s5_bt_im"] = _s5_prep_bwd(
        s5p["a_re"], s5p["a_im"], s5p["log_dt"], s5p["bt_re"], s5p["bt_im"], s5p["e01"],
        dlam_re, dlam_im, _unblockdiag_b(dbbd_re), _unblockdiag_b(dbbd_im))
    grad_x, g["mix0"], g["sc1a"], g["sh1a"] = node_bwd(
        _f_pn, [_full(x)], [_full(dx), _full(du0)], [nrm["mix"][0], sc1a, sh1a], (0,), [row32], "node0_bwd")
    return loss[0, 0], grad_x, g, chip_sum, from_chips


_MESH = pl.DeviceIdType.MESH
_VMEM_SPEC = pl.BlockSpec(memory_space=pltpu.VMEM)
_ANY_SPEC = pl.BlockSpec(memory_space=pl.ANY)


def _my_place():
    ix, iy, ic = lax.axis_index("x"), lax.axis_index("y"), lax.axis_index("c")
    return ix, iy, ic


def _exchange(x, *, gather, name):
    r = x.shape[-2]

    def body(x_ref, o_ref, ssem, rsem):
        ix, iy, ic = _my_place()
        me = 4 * ix + 2 * iy + ic
        if gather:
            o_ref[me] = x_ref[...]
        else:
            o_ref[me] = x_ref[me]
        copies = []
        for k in range(1, N_DEV):
            tx, ty, tc = ix ^ (k >> 2), iy ^ ((k >> 1) & 1), ic ^ (k & 1)
            src = x_ref if gather else x_ref.at[4 * tx + 2 * ty + tc]
            cp = pltpu.make_async_remote_copy(src_ref=src, dst_ref=o_ref.at[me], send_sem=ssem.at[k - 1],
                                              recv_sem=rsem.at[k - 1], device_id=(tx, ty, tc), device_id_type=_MESH)
            cp.start()
            copies.append(cp)
        for cp in copies:
            cp.wait()

    return pl.pallas_call(
        body, out_shape=jax.ShapeDtypeStruct((N_DEV, r, 128), x.dtype), in_specs=[_VMEM_SPEC], out_specs=_VMEM_SPEC,
        scratch_shapes=[pltpu.SemaphoreType.DMA((N_DEV - 1,)), pltpu.SemaphoreType.DMA((N_DEV - 1,))], name=name,
    )(x)


class _Comm:
    def __init__(self, ins, out_shapes, scratch, phases):
        self.ins, self.out_shapes, self.scratch, self.phases = list(ins), list(out_shapes), list(scratch), phases

    def run(self, step, n_steps, in_refs, out_refs, scratch_refs, only):
        when = {"first": 0, "mid": (7 * n_steps) // 8, "last": n_steps - 1}
        for phase, fn in self.phases:
            if phase in only:
                pl.when(step == when[phase])(functools.partial(fn, in_refs, out_refs, scratch_refs))


def _ag_comm(xs):
    n = len(xs)

    def parts():
        ix, iy, ic = _my_place()
        return ic, (ix, iy, ic), (ix, iy, 1 - ic), [(1 - ix, iy), (ix, 1 - iy), (1 - ix, 1 - iy)]

    def copy(ins, outs, sc, a, k, block, to, from_x=False):
        px, py, pc = block
        slot = outs[a].at[4 * px + 2 * py + pc]
        return pltpu.make_async_remote_copy(
            src_ref=ins[a] if from_x else slot, dst_ref=slot, send_sem=sc[0].at[7 * a + k], recv_sem=sc[1].at[7 * a + k],
            device_id=to, device_id_type=_MESH)

    def local(ins, outs, sc, a, me):
        return pltpu.make_async_copy(ins[a], outs[a].at[4 * me[0] + 2 * me[1] + me[2]], sc[2].at[a])

    def start(ins, outs, sc):
        ic, me, sibling, chips = parts()
        for a in range(n):
            local(ins, outs, sc, a, me).start()
            copy(ins, outs, sc, a, 0, me, sibling, True).start()
            for j, chip in enumerate(chips):
                copy(ins, outs, sc, a, 1 + j, me, (*chip, ic), True).start()

    def forward(ins, outs, sc):
        ic, me, sibling, chips = parts()
        for a in range(n):
            for j, chip in enumerate(chips):
                copy(ins, outs, sc, a, 1 + j, (*chip, ic), me).wait_recv()
                copy(ins, outs, sc, a, 4 + j, (*chip, ic), sibling).start()

    def finish(ins, outs, sc):
        ic, me, sibling, chips = parts()
        for a in range(n):
            copy(ins, outs, sc, a, 0, sibling, me).wait_recv()
            for j, chip in enumerate(chips):
                copy(ins, outs, sc, a, 4 + j, (*chip, 1 - ic), me).wait_recv()
        for a in range(n):
            copy(ins, outs, sc, a, 0, me, sibling, True).wait_send()
            for j, chip in enumerate(chips):
                copy(ins, outs, sc, a, 1 + j, me, (*chip, ic), True).wait_send()
                copy(ins, outs, sc, a, 4 + j, (*chip, ic), sibling).wait_send()
            local(ins, outs, sc, a, me).wait()

    return _Comm(xs, [jax.ShapeDtypeStruct((N_DEV,) + x.shape, x.dtype) for x in xs],
                 [pltpu.SemaphoreType.DMA((7 * n,)), pltpu.SemaphoreType.DMA((7 * n,)), pltpu.SemaphoreType.DMA((n,))],
                 [("first", start), ("mid", forward), ("last", finish)])


def _chips_comm(ps):
    n = len(ps)

    def copies(ins, outs, sc):
        ix, iy, ic = _my_place()
        out = []
        for a in range(n):
            for k in range(1, 4):
                tx, ty = ix ^ (k >> 1), iy ^ (k & 1)
                out.append(pltpu.make_async_remote_copy(
                    src_ref=ins[a].at[2 * tx + ty], dst_ref=outs[a].at[k - 1], send_sem=sc[0].at[3 * a + k - 1],
                    recv_sem=sc[1].at[3 * a + k - 1], device_id=(tx, ty, ic), device_id_type=_MESH))
        return out

    def start(ins, outs, sc):
        for cp in copies(ins, outs, sc):
            cp.start()

    def finish(ins, outs, sc):
        for cp in copies(ins, outs, sc):
            cp.wait()

    return _Comm(ps, [jax.ShapeDtypeStruct((3,) + p.shape[1:], p.dtype) for p in ps],
                 [pltpu.SemaphoreType.DMA((3 * n,)), pltpu.SemaphoreType.DMA((3 * n,))], [("first", start), ("last", finish)])


def _sibling_comm(gs):
    n = len(gs)

    def copies(ins, outs, sc):
        ix, iy, ic = _my_place()
        return [pltpu.make_async_remote_copy(src_ref=ins[a].at[2 * q + 1 - ic], dst_ref=outs[a].at[q], send_sem=sc[0].at[4 * a + q],
                                             recv_sem=sc[1].at[4 * a + q], device_id=(ix, iy, 1 - ic), device_id_type=_MESH)
                for a in range(n) for q in range(4)]

    def start(ins, outs, sc):
        for cp in copies(ins, outs, sc):
            cp.start()

    def finish(ins, outs, sc):
        for cp in copies(ins, outs, sc):
            cp.wait()

    return _Comm(gs, [jax.ShapeDtypeStruct((4,) + g.shape[1:], g.dtype) for g in gs],
                 [pltpu.SemaphoreType.DMA((4 * n,)), pltpu.SemaphoreType.DMA((4 * n,))], [("first", start), ("last", finish)])


def _ada_fwd(c_all, w_ada, b_cols):
    def body(c_ref, w_ref, b_ref, o_ref):
        cs = _silu(c_ref[...])
        for i in range(2):
            o_ref[i] = _dot(cs, w_ref[i]) + b_ref[pl.ds(i, 1), :]
    return pl.pallas_call(body, out_shape=jax.ShapeDtypeStruct((2, N_DEV, w_ada.shape[2]), F32), name="ada_fwd",
                          compiler_params=pltpu.CompilerParams(vmem_limit_bytes=VMEM_LIMIT))(c_all, w_ada, b_cols)


def _ada_bwd(c_all, dm):
    def body(c_ref, d_ref, o_ref):
        cs = _silu(c_ref[...])
        for i in range(2):
            o_ref[i] = _dot(cs, d_ref[i], _TN)
    return pl.pallas_call(body, out_shape=jax.ShapeDtypeStruct((2, D_MODEL, dm.shape[2]), F32), name="ada_bwd",
                          compiler_params=pltpu.CompilerParams(vmem_limit_bytes=VMEM_LIMIT))(c_all, dm)


def _pair_add(gs, recvs, core, *, name):
    n = len(gs)

    def body(core_ref, *refs):
        for a in range(n):
            refs[2 * n + a][...] = (refs[a][...].astype(F32) + refs[n + a][...].astype(F32)).astype(refs[2 * n + a].dtype)

    own = [pl.BlockSpec((1,) + g.shape[1:], lambda q, core_ref: (2 * q + core_ref[0], 0, 0)) for g in gs]
    slab = [pl.BlockSpec((1,) + g.shape[1:], lambda q, core_ref: (q, 0, 0)) for g in gs]
    grid_spec = pltpu.PrefetchScalarGridSpec(num_scalar_prefetch=1, grid=(4,), in_specs=own + slab, out_specs=slab)
    return pl.pallas_call(body, grid_spec=grid_spec, out_shape=[jax.ShapeDtypeStruct((4,) + g.shape[1:], g.dtype) for g in gs],
                          name=name, compiler_params=_params(("parallel",)))(core, *gs, *recvs)


def _sum_slots(x, *, name):
    def body(x_ref, o_ref):
        acc = x_ref[0]
        for s in range(1, N_DEV):
            acc = acc + x_ref[s]
        o_ref[...] = acc
    return pl.pallas_call(body, out_shape=jax.ShapeDtypeStruct(x.shape[1:], F32), name=name)(x)


def _adamw_math(w, m, v, g):
    mn = ADAM_B1 * m + (1.0 - ADAM_B1) * g
    vn = ADAM_B2 * v + (1.0 - ADAM_B2) * (g * g)
    m_hat = mn / (1.0 - ADAM_B1 ** ADAM_STEP)
    v_hat = vn / (1.0 - ADAM_B2 ** ADAM_STEP)
    return -ADAM_LR * (m_hat / (jnp.sqrt(v_hat) + ADAM_EPS) + ADAM_WD * w), mn, vn


def _adamw_multi(ws, ms, vs, gs, *, name):
    n = len(ws)

    def body(*refs):
        for i in range(n):
            g = refs[3 * n + i][...]
            o = refs[4 * n + 4 * i:4 * n + 4 * i + 4]
            o[0][...] = g
            o[1][...], o[2][...], o[3][...] = _adamw_math(refs[i][...], refs[n + i][...], refs[2 * n + i][...], g)

    out_shape = [jax.ShapeDtypeStruct(w.shape, F32) for w in ws for _ in range(4)]
    res = pl.pallas_call(body, out_shape=out_shape, name=name,
                         compiler_params=pltpu.CompilerParams(vmem_limit_bytes=VMEM_LIMIT))(*ws, *ms, *vs, *gs)
    return [res[4 * i:4 * i + 4] for i in range(n)]


def _adamw(w, m, v, gparts, *, tile, name, sel=None):
    r, cdim = w.shape
    ng = len(gparts)
    sel = jnp.zeros((1,), jnp.int32) if sel is None else sel

    def body(*refs):
        w_ref, m_ref, v_ref = refs[1:4]
        g = None
        for p, part in zip(refs[4:4 + ng], gparts):
            pv = (p[0] if isinstance(part, tuple) else p[...]).astype(F32)
            g = pv if g is None else g + pv
        g_ref, d_ref, nm_ref, nv_ref = refs[4 + ng:]
        g_ref[...] = g
        d_ref[...], nm_ref[...], nv_ref[...] = _adamw_math(w_ref[...], m_ref[...], v_ref[...], g)

    spec = pl.BlockSpec((tile, cdim), lambda i, s: (i, 0))

    def part_spec(part):
        if not isinstance(part, tuple):
            return spec
        slab = part[1]
        if slab is None:
            return pl.BlockSpec((1, tile, cdim), lambda i, s: (s[0], i, 0))
        return pl.BlockSpec((1, tile, cdim), lambda i, s: (slab, i, 0))

    grid_spec = pltpu.PrefetchScalarGridSpec(
        num_scalar_prefetch=1, grid=(r // tile,), in_specs=[spec] * 3 + [part_spec(p) for p in gparts], out_specs=[spec] * 4)
    return pl.pallas_call(
        body, grid_spec=grid_spec, out_shape=[jax.ShapeDtypeStruct(w.shape, F32)] * 4, name=name,
        compiler_params=_params(("parallel",)),
    )(sel, w, m, v, *[p[0] if isinstance(p, tuple) else p for p in gparts])


_REP_ROWS = 272
_REP_SIZE = 2 * 1024 * 2 + 4096 * 2 + 64 + 65536 * 4 + 1024 + 1024


def _pad_rows(v, rows):
    return jnp.pad(v.reshape(-1), (0, rows * 128 - v.size)).reshape(rows, 128)


def kernel(x, c, w_ada, b_ada, norm_mix, norm_mlp, s5_a_re, s5_a_im, s5_log_dt, s5_b_re, s5_b_im, s5_c_re, s5_c_im, s5_d, s5_w_glu, gla_w_in, gla_w_gate2, gla_b_gate, gla_g_norm, gla_w_out, w_ff1, w_ff2, norm_final, loss_target, m_w_ada, m_b_ada, m_norm_mix, m_norm_mlp, m_s5_a_re, m_s5_a_im, m_s5_log_dt, m_s5_b_re, m_s5_b_im, m_s5_c_re, m_s5_c_im, m_s5_d, m_s5_w_glu, m_gla_w_in, m_gla_w_gate2, m_gla_b_gate, m_gla_g_norm, m_gla_w_out, m_w_ff1, m_w_ff2, m_norm_final, v_w_ada, v_b_ada, v_norm_mix, v_norm_mlp, v_s5_a_re, v_s5_a_im, v_s5_log_dt, v_s5_b_re, v_s5_b_im, v_s5_c_re, v_s5_c_im, v_s5_d, v_s5_w_glu, v_gla_w_in, v_gla_w_gate2, v_gla_b_gate, v_gla_g_norm, v_gla_w_out, v_w_ff1, v_w_ff2, v_norm_final):
    ix, iy, ic = _my_place()
    me = 4 * ix + 2 * iy + ic
    ada_w = w_ada.shape[2]

    msg = jnp.concatenate([c.reshape(8, 128), gla_w_gate2[0].reshape(8, 128), _pad_rows(gla_b_gate, 1),
                           gla_g_norm.reshape(1, 128), jnp.zeros((6, 128), F32)])
    got = _exchange(msg, gather=True, name="gather_small")
    c_all = got[:, 0:8].reshape(N_DEV, D_MODEL)
    wg2 = got[:, 8:16].reshape(N_DEV, GLA_RANK, 64).transpose(1, 0, 2).reshape(GLA_RANK, GLA_QK)
    bg = got[:, 16, :64].reshape(1, GLA_QK)
    gn = got[:, 17, :].reshape(1, D_MODEL)

    b_cols = lax.dynamic_slice_in_dim(b_ada, me * ada_w, ada_w, axis=1)
    mod_cols = _ada_fwd(c_all, w_ada, b_cols)
    pay = jnp.pad(mod_cols.transpose(1, 0, 2).reshape(N_DEV, 12, 128), ((0, 0), (0, 4), (0, 0)))
    mod = _exchange(pay, gather=False, name="a2a_mod")[:, :12].reshape(N_DEV, 2, ada_w).transpose(1, 0, 2).reshape(2, 6 * D_MODEL)
    mods = [[mod[i:i + 1, j * D_MODEL:(j + 1) * D_MODEL] for j in range(6)] for i in range(2)]

    big_w = [s5_w_glu[0], gla_w_in[0], gla_w_out[0], w_ff1[0], w_ff1[1], w_ff2[0], w_ff2[1]]
    big_m = [m_s5_w_glu[0], m_gla_w_in[0], m_gla_w_out[0], m_w_ff1[0], m_w_ff1[1], m_w_ff2[0], m_w_ff2[1]]
    big_v = [v_s5_w_glu[0], v_gla_w_in[0], v_gla_w_out[0], v_w_ff1[0], v_w_ff1[1], v_w_ff2[0], v_w_ff2[1]]
    rows_i = lax.broadcasted_iota(jnp.int32, (GLA_NB * GLA_CHUNK, GLA_NB * GLA_CHUNK), 0)
    cols_i = lax.broadcasted_iota(jnp.int32, (GLA_NB * GLA_CHUNK, GLA_NB * GLA_CHUNK), 1)
    tri = ((rows_i // GLA_CHUNK == cols_i // GLA_CHUNK) & (cols_i <= rows_i)).astype(F32)
    w = dict(wg2=jnp.pad(wg2, ((0, GLA_RANK_PAD - GLA_RANK), (0, 0))), bg=bg, gn=gn, tri=tri)
    core = ic.reshape(1).astype(jnp.int32)
    chip = (2 * ix + iy).reshape(1).astype(jnp.int32)
    nrm = dict(mix=[norm_mix[i:i + 1] for i in range(2)], mlp=[norm_mlp[i:i + 1] for i in range(2)], final=norm_final.reshape(1, D_MODEL))
    e01 = (lax.broadcasted_iota(jnp.int32, (S5_GROUPS, S5_NSTATE), 1) // S5_STATE
           == lax.broadcasted_iota(jnp.int32, (S5_GROUPS, S5_NSTATE), 0)).astype(F32)
    s5p = dict(a_re=s5_a_re.reshape(1, S5_NSTATE), a_im=s5_a_im.reshape(1, S5_NSTATE), log_dt=s5_log_dt,
               bt_re=s5_b_re[0].transpose(2, 0, 1).reshape(S5_GROUP, S5_NSTATE),
               bt_im=s5_b_im[0].transpose(2, 0, 1).reshape(S5_GROUP, S5_NSTATE),
               c_re=s5_c_re[0], c_im=s5_c_im[0], d=s5_d, e01=e01)

    loss_local, grad_x, g, chip_sum, from_chips = _local_step(
        x[0], loss_target[0], mods, nrm, s5p, w, [a.astype(MXU_DT) for a in big_w], core)
    loss = lax.psum(loss_local, ("x", "y", "c"))
    big = []
    for i in range(len(big_w)):
        parts = [(chip_sum[i], None), (from_chips[i], 0), (from_chips[i], 1), (from_chips[i], 2)]
        big.append(_adamw(big_w[i], big_m[i], big_v[i], parts, tile=min(512, big_w[i].shape[0]), name="adamw_big%d" % i, sel=chip))

    rep = [jnp.concatenate([g["mix0"], g["mix1"]]), jnp.concatenate([g["mlp0"], g["mlp1"]]), g["s5_a_re"], g["s5_a_im"], g["s5_log_dt"],
           g["s5_bt_re"].reshape(S5_GROUP, S5_GROUPS, S5_STATE).transpose(1, 2, 0), g["s5_bt_im"].reshape(S5_GROUP, S5_GROUPS, S5_STATE).transpose(1, 2, 0),
           g["s5_c_re"], g["s5_c_im"], g["s5_d"], g["nf"]]
    rep_shapes = [(2, D_MODEL), (2, D_MODEL), (1, 64, 64), (1, 64, 64), (1, 64), (1, 64, 64, 16), (1, 64, 64, 16), (1, 64, 16, 64), (1, 64, 16, 64), (1, D_MODEL), (D_MODEL,)]
    rep_flat = jnp.concatenate([a.reshape(-1) for a in rep])
    rep_blk = jnp.pad(rep_flat, (0, N_DEV * _REP_ROWS * 128 - _REP_SIZE)).reshape(N_DEV, _REP_ROWS, 128)
    dmod = jnp.stack([jnp.concatenate([g["sh1" + t], g["sc1" + t], g["gt1" + t], g["sh2" + t], g["sc2" + t], g["gt2" + t]], axis=1)[0] for t in "ab"])
    msg = jnp.concatenate([
        rep_blk,
        g["wg2"][:GLA_RANK].reshape(GLA_RANK, N_DEV, 64).transpose(1, 0, 2).reshape(N_DEV, 8, 128),
        jnp.pad(g["bg"].reshape(N_DEV, 1, 64), ((0, 0), (0, 0), (0, 64))),
        g["gn"].reshape(N_DEV, 1, 128),
        dmod.reshape(2, N_DEV, ada_w).transpose(1, 0, 2).reshape(N_DEV, 12, 128),
        jnp.zeros((N_DEV, 2, 128), F32),
    ], axis=1)
    got = _exchange(msg, gather=False, name="a2a_small_grads")
    tot = _sum_slots(got, name="sum_small_grads")
    dm = got[:, 282:294].reshape(N_DEV, 2, ada_w).transpose(1, 0, 2)
    g_w_ada = _ada_bwd(c_all, dm)
    back = _exchange(jnp.concatenate([tot[0:_REP_ROWS], tot[282:294], jnp.zeros((4, 128), F32)]), gather=True, name="gather_small_grads")
    rep_sum = back[:, :_REP_ROWS].reshape(-1)[:_REP_SIZE]
    g_b_ada = back[:, _REP_ROWS:_REP_ROWS + 12].reshape(N_DEV, 2, ada_w).transpose(1, 0, 2).reshape(2, 6 * D_MODEL)
    g_rep, off = [], 0
    for s in rep_shapes:
        n = math.prod(s)
        g_rep.append(rep_sum[off:off + n].reshape(s))
        off += n
    g_small = g_rep + [g_b_ada, tot[272:280].reshape(GLA_RANK, 64)[None], tot[280, :64][None], tot[281][None]]
    p_small = [norm_mix, norm_mlp, s5_a_re, s5_a_im, s5_log_dt, s5_b_re, s5_b_im, s5_c_re, s5_c_im, s5_d, norm_final, b_ada, gla_w_gate2, gla_b_gate, gla_g_norm]
    m_small = [m_norm_mix, m_norm_mlp, m_s5_a_re, m_s5_a_im, m_s5_log_dt, m_s5_b_re, m_s5_b_im, m_s5_c_re, m_s5_c_im, m_s5_d, m_norm_final, m_b_ada, m_gla_w_gate2, m_gla_b_gate, m_gla_g_norm]
    v_small = [v_norm_mix, v_norm_mlp, v_s5_a_re, v_s5_a_im, v_s5_log_dt, v_s5_b_re, v_s5_b_im, v_s5_c_re, v_s5_c_im, v_s5_d, v_norm_final, v_b_ada, v_gla_w_gate2, v_gla_b_gate, v_gla_g_norm]
    as2d = lambda a: a.reshape(1, -1) if a.ndim == 1 else a
    small = _adamw_multi([as2d(a) for a in p_small], [as2d(a) for a in m_small], [as2d(a) for a in v_small],
                         [as2d(a) for a in g_small], name="adamw_small")
    small = [[o.reshape(p.shape) for o in outs] for outs, p in zip(small, p_small)]
    ada = _adamw(w_ada.reshape(2 * D_MODEL, ada_w), m_w_ada.reshape(2 * D_MODEL, ada_w), v_w_ada.reshape(2 * D_MODEL, ada_w),
                 [g_w_ada.reshape(2 * D_MODEL, ada_w)], tile=512, name="adamw_ada")
    ada = [a.reshape(w_ada.shape) for a in ada]

    def leaves(k):
        nm, nl, a_re, a_im, ldt, b_re, b_im, c_re, c_im, dsk, nf, bada, wg2_, bg_, gn_ = [s[k] for s in small]
        glu_, gin_, gout_, ff1a_, ff1b_, ff2a_, ff2b_ = [b[k] for b in big]
        return [ada[k], bada, nm, nl, a_re, a_im, ldt, b_re, b_im, c_re, c_im, dsk, glu_[None], gin_[None], wg2_, bg_, gn_, gout_[None],
                jnp.stack([ff1a_, ff1b_]), jnp.stack([ff2a_, ff2b_]), nf]

    return (loss, grad_x[None], *leaves(0), *leaves(1), *leaves(2), *leaves(3))
```

```python
import functools
import math

import jax
import jax.numpy as jnp
from jax import lax
from jax.experimental import pallas as pl
from jax.experimental.pallas import tpu as pltpu

F32 = jnp.float32
BF16 = jnp.bfloat16
MXU_DT = BF16
EPS = 1e-6
N_DEV = 8
VMEM_LIMIT = 56 * 1024 * 1024

D_MODEL = 1024
S5_GROUP = 16
S5_GROUPS = 64
S5_STATE = 64
S5_NSTATE = S5_GROUPS * S5_STATE
S5_GB = 16
S5_NB = S5_GROUPS // S5_GB
S5_BC = S5_GB * S5_GROUP
S5_BS = S5_GB * S5_STATE
GLA_HEADS = 4
GLA_QK = 512
GLA_DK = 128
GLA_DV = 256
GLA_RANK = 16
GLA_RANK_PAD = 128
GLA_TAU = 16.0
GLA_CHUNK = 64
GLA_NB = 4
GLA_IN = 3088
GLA_INP = 2 * GLA_QK + 2 * D_MODEL + GLA_RANK_PAD
D_FF = 4096

ADAM_LR = 0.001
ADAM_B1 = 0.9
ADAM_B2 = 0.999
ADAM_EPS = 1e-08
ADAM_WD = 0.01
ADAM_STEP = 10

_NN = (((1,), (0,)), ((), ()))
_NT = (((1,), (1,)), ((), ()))
_TN = (((0,), (0,)), ((), ()))


def _dot(a, b, dn=_NN):
    return lax.dot_general(a.astype(MXU_DT), b.astype(MXU_DT), dn, preferred_element_type=F32)


def _dot_exact01(x, m01, dn=_NN):
    x1 = x.astype(BF16)
    r1 = x - x1.astype(F32)
    x2 = r1.astype(BF16)
    x3 = (r1 - x2.astype(F32)).astype(BF16)
    m = m01.astype(BF16)
    d = lambda u: lax.dot_general(u, m, dn, preferred_element_type=F32)
    return d(x1) + d(x2) + d(x3)


def _dot_01_left(m01, x, dn=_NN):
    x1 = x.astype(BF16)
    r1 = x - x1.astype(F32)
    x2 = r1.astype(BF16)
    x3 = (r1 - x2.astype(F32)).astype(BF16)
    m = m01.astype(BF16)
    d = lambda u: lax.dot_general(m, u, dn, preferred_element_type=F32)
    return d(x1) + d(x2) + d(x3)


def _sigmoid(x):
    return 1.0 / (1.0 + jnp.exp(-x))


def _silu(x):
    return x * _sigmoid(x)


def _gelu(x):
    return 0.5 * x * (1.0 + jnp.tanh(math.sqrt(2.0 / math.pi) * (x + 0.044715 * (x * x * x))))


def _logsig(x):
    return jnp.minimum(x, 0.0) - jnp.log(1.0 + jnp.exp(-jnp.abs(x)))


def _rms(x):
    return lax.rsqrt(jnp.mean(x * x, axis=-1, keepdims=True) + EPS)


def _params(sem):
    return pltpu.CompilerParams(dimension_semantics=sem, vmem_limit_bytes=VMEM_LIMIT)


def _mm(a, b, dims, out_dtype, *, tm, tn, name, a_fn=None, out_fn=None, extra=None, by_owner=False, comm=None):
    if dims == "tn":
        k, m = a.shape
        n = b.shape[1]
    else:
        m, k = a.shape
        n = b.shape[0] if dims == "nt" else b.shape[1]
    tm, tn = min(tm, m), min(tn, n)
    assert m % tm == 0 and n % tn == 0, (name, m, n, tm, tn)
    dn = {"nn": _NN, "nt": _NT, "tn": _TN}[dims]
    n_in = 2 if extra is None else 3
    nci, nco = (len(comm.ins), len(comm.out_shapes)) if comm is not None else (0, 0)

    def body(*refs):
        a_ref, b_ref = refs[0], refs[1]
        o_ref = refs[n_in + nci]
        if comm is not None:
            run_comm = functools.partial(
                comm.run, pl.program_id(0) * (n // tn) + pl.program_id(1), (m // tm) * (n // tn), refs[n_in:n_in + nci],
                refs[n_in + nci + 1:n_in + nci + 1 + nco], refs[n_in + nci + 1 + nco:])
            run_comm(("first", "mid"))
        av = a_ref[...]
        if a_fn is not None:
            av = a_fn(av.astype(F32))
        acc = _dot(av, b_ref[...], dn)
        if extra is not None:
            acc = out_fn(acc, refs[2][...].astype(F32))
        elif out_fn is not None:
            acc = out_fn(acc)
        if by_owner:
            o_ref[0] = acc.astype(o_ref.dtype)
        else:
            o_ref[...] = acc.astype(o_ref.dtype)
        if comm is not None:
            run_comm(("last",))

    a_spec = pl.BlockSpec((k, tm), lambda i, j: (0, i)) if dims == "tn" else pl.BlockSpec((tm, k), lambda i, j: (i, 0))
    b_spec = pl.BlockSpec((tn, k), lambda i, j: (j, 0)) if dims == "nt" else pl.BlockSpec((k, tn), lambda i, j: (0, j))
    if by_owner:
        o_spec = pl.BlockSpec((1, tm, tn), lambda i, j: (j, i, 0))
        out_shape = jax.ShapeDtypeStruct((n // tn, m, tn), out_dtype)
    else:
        o_spec = pl.BlockSpec((tm, tn), lambda i, j: (i, j))
        out_shape = jax.ShapeDtypeStruct((m, n), out_dtype)
    in_specs, args = [a_spec, b_spec], [a, b]
    if extra is not None:
        in_specs.append(o_spec)
        args.append(extra)
    if comm is None:
        return pl.pallas_call(
            body, grid=(m // tm, n // tn), in_specs=in_specs, out_specs=o_spec, out_shape=out_shape, name=name,
            compiler_params=_params(("parallel", "parallel")),
        )(*args)
    return pl.pallas_call(
        body, grid=(m // tm, n // tn), in_specs=in_specs + [_ANY_SPEC] * nci, out_specs=[o_spec] + [_ANY_SPEC] * nco,
        out_shape=[out_shape] + comm.out_shapes, scratch_shapes=comm.scratch, name=name,
        compiler_params=_params(("arbitrary", "arbitrary")),
    )(*args, *comm.ins)


def _rowcall(f, rows, pars, outs, accs, *, tile, name):
    length = rows[0][0].shape[0]
    tile = min(tile, length)
    nr, npar, no = len(rows), len(pars), len(outs)

    def body(*refs):
        vals = [r[...].astype(F32) for r in refs[:nr + npar]]
        res = f(*vals)
        o_refs = refs[nr + npar:nr + npar + no]
        a_refs = refs[nr + npar + no:]
        for o, v in zip(o_refs, res[:no]):
            o[...] = v.astype(o.dtype)
        if a_refs:
            @pl.when(pl.program_id(0) == 0)
            def _():
                for a in a_refs:
                    a[...] = jnp.zeros(a.shape, F32)
            for a, v in zip(a_refs, res[no:]):
                a[...] += jnp.broadcast_to(v, a.shape)

    in_specs = [pl.BlockSpec((tile, w), lambda i, cb=cb: (i, cb)) for (_, w, cb) in rows]
    in_specs += [pl.BlockSpec(p.shape, lambda i: (0, 0)) for p in pars]
    out_specs = [pl.BlockSpec((tile, w), lambda i: (i, 0)) for (w, _) in outs]
    out_specs += [pl.BlockSpec(s, lambda i: (0, 0)) for s in accs]
    out_shape = [jax.ShapeDtypeStruct((length, w), dt) for (w, dt) in outs]
    out_shape += [jax.ShapeDtypeStruct(s, F32) for s in accs]
    return pl.pallas_call(
        body, grid=(length // tile,), in_specs=in_specs, out_specs=out_specs, out_shape=out_shape, name=name,
        compiler_params=_params(("arbitrary",)),
    )(*[r[0] for r in rows], *pars)


def _vjp_of(f, n_row, n_cot, row_want):
    def g(*a):
        prow, cots, par = a[:n_row], a[n_row:n_row + n_cot], a[n_row + n_cot:]
        _, vjp = jax.vjp(f, *prow, *par)
        grads = vjp(tuple(cots))
        return tuple(grads[i] for i in row_want) + tuple(grads[n_row:])
    return g


def _f_pn(x, g, sc, sh):
    return (x, x * _rms(x) * g * (1.0 + sc) + sh)


def _f_res_pn(x, y, gt, g, sc, sh):
    xn = x + gt * y
    return (xn, xn * _rms(xn) * g * (1.0 + sc) + sh)


def _f_glu_res_pn(x, val, gate, gt, g, sc, sh):
    xn = x + gt * (val * _sigmoid(gate))
    return (xn, xn * _rms(xn) * g * (1.0 + sc) + sh)


def _f_final(x, y, tgt, gt, g):
    xn = x + gt * y
    err = xn * _rms(xn) * g - tgt
    return 0.5 * jnp.mean(err * err, axis=-1, keepdims=True)


def _g_final(x, y, tgt, gt, g):
    lrow, vjp = jax.vjp(_f_final, x, y, tgt, gt, g)
    dx, dy, _, dgt, dg = vjp(jnp.ones_like(lrow))
    return dx, dy, dgt, dg, jnp.sum(lrow)


def _full(a):
    return (a, a.shape[1], 0)


def _s5_prep_f(a_re, a_im, log_dt, bt_re, bt_im, e01):
    dt = jnp.exp(_dot_exact01(log_dt, e01))
    mag = jnp.exp(a_re * dt)
    ph = a_im * dt
    lb_re = mag * jnp.cos(ph)
    lb_im = mag * jnp.sin(ph)
    den = a_re * a_re + a_im * a_im
    nr = lb_re - 1.0
    ni = lb_im
    f_re = (nr * a_re + ni * a_im) / den
    f_im = (ni * a_re - nr * a_im) / den
    bb_re = f_re * bt_re - f_im * bt_im
    bb_im = f_re * bt_im + f_im * bt_re
    return lb_re, lb_im, bb_re, bb_im


def _s5_prep_outs():
    return [jax.ShapeDtypeStruct((1, S5_NSTATE), F32)] * 2 + [jax.ShapeDtypeStruct((S5_GROUP, S5_NSTATE), F32)] * 2


def _s5_prep(a_re, a_im, log_dt, bt_re, bt_im, e01):
    def body(*refs):
        res = _s5_prep_f(*[r[...] for r in refs[:6]])
        for o, v in zip(refs[6:], res):
            o[...] = v
    return pl.pallas_call(body, out_shape=_s5_prep_outs(), name="s5_prep",
                          compiler_params=pltpu.CompilerParams(vmem_limit_bytes=VMEM_LIMIT))(a_re, a_im, log_dt, bt_re, bt_im, e01)


def _s5_prep_bwd(a_re, a_im, log_dt, bt_re, bt_im, e01, d_lb_re, d_lb_im, d_bb_re, d_bb_im):
    def f(a_re, a_im, log_dt, bt_re, bt_im, e01):
        @jax.custom_vjp
        def expand(v):
            return _dot_exact01(v, e01)
        expand.defvjp(lambda v: (_dot_exact01(v, e01), None), lambda _, ct: (_dot_exact01(ct, e01, _NT),))
        dt = jnp.exp(expand(log_dt))
        mag = jnp.exp(a_re * dt)
        ph = a_im * dt
        lb_re = mag * jnp.cos(ph)
        lb_im = mag * jnp.sin(ph)
        den = a_re * a_re + a_im * a_im
        nr = lb_re - 1.0
        f_re = (nr * a_re + lb_im * a_im) / den
        f_im = (lb_im * a_re - nr * a_im) / den
        return lb_re, lb_im, f_re * bt_re - f_im * bt_im, f_re * bt_im + f_im * bt_re

    def body(*refs):
        ins = [r[...] for r in refs[:5]]
        e = refs[5][...]
        cots = tuple(r[...] for r in refs[6:10])
        _, vjp = jax.vjp(lambda *p: f(*p, e), *ins)
        for o, v in zip(refs[10:], vjp(cots)):
            o[...] = v
    outs = [jax.ShapeDtypeStruct(v.shape, F32) for v in (a_re, a_im, log_dt, bt_re, bt_im)]
    return pl.pallas_call(body, out_shape=outs, name="s5_prep_bwd",
                          compiler_params=pltpu.CompilerParams(vmem_limit_bytes=VMEM_LIMIT))(
        a_re, a_im, log_dt, bt_re, bt_im, e01, d_lb_re, d_lb_im, d_bb_re, d_bb_im)


def _s5_scan(x_re, x_im, a_r, a_i, c_r, c_i, n_tiles, reverse):
    sgn = -1.0 if reverse else 1.0

    def tile(k, carry):
        cr, ci = carry
        i = (n_tiles - 1 - k) if reverse else k
        order = range(7, -1, -1) if reverse else range(8)
        for j in order:
            br = x_re[i, pl.ds(j, 1), :]
            bi = x_im[i, pl.ds(j, 1), :]
            nr = a_r * cr - (sgn * a_i) * ci + br
            ni = a_r * ci + (sgn * a_i) * cr + bi
            x_re[i, pl.ds(j, 1), :] = nr
            x_im[i, pl.ds(j, 1), :] = ni
            cr, ci = nr, ni
        return cr, ci

    return lax.fori_loop(0, n_tiles, tile, (c_r, c_i))


def _s5_fwd(u, lam_re, lam_im, bbd_re, bbd_im, cbd_re, cbd_im, d_skip, comm, *, tc):
    length = u.shape[0]
    tc = min(tc, length)
    nt = length // tc
    nci, nco = len(comm.ins), len(comm.out_shapes)

    def body(*refs):
        u_ref, lr_ref, li_ref, br_ref, bi_ref, cr_ref, ci_ref, d_ref = refs[:8]
        z_ref, sr_ref, si_ref = refs[8 + nci:11 + nci]
        xr, xi, car_r, car_i = refs[11 + nci + nco:15 + nci + nco]
        run_comm = functools.partial(comm.run, pl.program_id(0) * nt + pl.program_id(1), S5_NB * nt, refs[8:8 + nci],
                                     refs[11 + nci:11 + nci + nco], refs[15 + nci + nco:])
        run_comm(("first", "mid"))

        @pl.when(pl.program_id(1) == 0)
        def _():
            car_r[...] = jnp.zeros_like(car_r)
            car_i[...] = jnp.zeros_like(car_i)
        sr_ref[0] = car_r[...]
        si_ref[0] = car_i[...]
        uv = u_ref[...]
        xr[...] = _dot(uv, br_ref[0]).reshape(tc // 8, 8, S5_BS)
        xi[...] = _dot(uv, bi_ref[0]).reshape(tc // 8, 8, S5_BS)
        cr, ci = _s5_scan(xr, xi, lr_ref[...], li_ref[...], car_r[...], car_i[...], tc // 8, False)
        car_r[...] = cr
        car_i[...] = ci
        y = (_dot(xr[...].reshape(tc, S5_BS), cr_ref[0]) - _dot(xi[...].reshape(tc, S5_BS), ci_ref[0]) + d_ref[...] * uv)
        z_ref[...] = _gelu(y).astype(z_ref.dtype)
        run_comm(("last",))

    blk_u = pl.BlockSpec((tc, S5_BC), lambda g, t: (t, g))
    blk_l = pl.BlockSpec((1, S5_BS), lambda g, t: (0, g))
    blk_b = pl.BlockSpec((1, S5_BC, S5_BS), lambda g, t: (g, 0, 0))
    blk_c = pl.BlockSpec((1, S5_BS, S5_BC), lambda g, t: (g, 0, 0))
    blk_d = pl.BlockSpec((1, S5_BC), lambda g, t: (0, g))
    blk_s = pl.BlockSpec((1, 1, S5_BS), lambda g, t: (t, 0, g))
    return pl.pallas_call(
        body, grid=(S5_NB, nt),
        in_specs=[blk_u, blk_l, blk_l, blk_b, blk_b, blk_c, blk_c, blk_d] + [_ANY_SPEC] * nci,
        out_specs=[blk_u, blk_s, blk_s] + [_ANY_SPEC] * nco,
        out_shape=[jax.ShapeDtypeStruct((length, D_MODEL), MXU_DT),
                   jax.ShapeDtypeStruct((nt, 1, S5_NSTATE), F32), jax.ShapeDtypeStruct((nt, 1, S5_NSTATE), F32)] + comm.out_shapes,
        scratch_shapes=[pltpu.VMEM((tc // 8, 8, S5_BS), F32), pltpu.VMEM((tc // 8, 8, S5_BS), F32),
                        pltpu.VMEM((1, S5_BS), F32), pltpu.VMEM((1, S5_BS), F32)] + comm.scratch,
        name="s5_fwd", compiler_params=_params(("arbitrary", "arbitrary")),
    )(u, lam_re, lam_im, bbd_re, bbd_im, cbd_re, cbd_im, d_skip, *comm.ins)


def _s5_bwd(u, dz, st_re, st_im, lam_re, lam_im, bbd_re, bbd_im, cbd_re, cbd_im, d_skip, comm, *, tc):
    length = u.shape[0]
    tc = min(tc, length)
    nt = length // tc
    nci, nco = len(comm.ins), len(comm.out_shapes)

    def body(*refs):
        u_ref, dz_ref, sr_ref, si_ref, lr_ref, li_ref, br_ref, bi_ref, cr_ref, ci_ref, d_ref = refs[:11]
        du_ref, dbr_ref, dbi_ref, dcr_ref, dci_ref, dlr_ref, dli_ref, dd_ref = refs[11 + nci:19 + nci]
        xr, xi, gr, gi, car_r, car_i = refs[19 + nci + nco:25 + nci + nco]
        run_comm = functools.partial(comm.run, pl.program_id(0) * nt + pl.program_id(1), S5_NB * nt, refs[11:11 + nci],
                                     refs[19 + nci:19 + nci + nco], refs[25 + nci + nco:])
        run_comm(("first", "mid"))

        @pl.when(pl.program_id(1) == 0)
        def _():
            car_r[...] = jnp.zeros_like(car_r)
            car_i[...] = jnp.zeros_like(car_i)
            for r in (dbr_ref, dbi_ref, dcr_ref, dci_ref, dlr_ref, dli_ref, dd_ref):
                r[...] = jnp.zeros(r.shape, F32)
        a_r, a_i = lr_ref[...], li_ref[...]
        uv = u_ref[...]
        xr[...] = _dot(uv, br_ref[0]).reshape(tc // 8, 8, S5_BS)
        xi[...] = _dot(uv, bi_ref[0]).reshape(tc // 8, 8, S5_BS)
        _s5_scan(xr, xi, a_r, a_i, sr_ref[0], si_ref[0], tc // 8, False)
        xrv = xr[...].reshape(tc, S5_BS)
        xiv = xi[...].reshape(tc, S5_BS)
        y = _dot(xrv, cr_ref[0]) - _dot(xiv, ci_ref[0]) + d_ref[...] * uv
        _, gelu_vjp = jax.vjp(_gelu, y)
        dy = gelu_vjp(dz_ref[...].astype(F32))[0]
        dd_ref[...] += jnp.sum(dy * uv, axis=0, keepdims=True)
        dcr_ref[0] += _dot(xrv, dy, _TN)
        dci_ref[0] -= _dot(xiv, dy, _TN)
        gr[...] = _dot(dy, cr_ref[0], _NT).reshape(tc // 8, 8, S5_BS)
        gi[...] = (-_dot(dy, ci_ref[0], _NT)).reshape(tc // 8, 8, S5_BS)
        cr, ci = _s5_scan(gr, gi, a_r, a_i, car_r[...], car_i[...], tc // 8, True)
        car_r[...] = cr
        car_i[...] = ci
        grv = gr[...].reshape(tc, S5_BS)
        giv = gi[...].reshape(tc, S5_BS)
        first = lax.broadcasted_iota(jnp.int32, (tc, 1), 0) == 0
        xpr = jnp.where(first, sr_ref[0], pltpu.roll(xrv, 1, 0))
        xpi = jnp.where(first, si_ref[0], pltpu.roll(xiv, 1, 0))
        dlr_ref[...] += jnp.sum(grv * xpr + giv * xpi, axis=0, keepdims=True)
        dli_ref[...] += jnp.sum(giv * xpr - grv * xpi, axis=0, keepdims=True)
        dbr_ref[0] += _dot(uv, grv, _TN)
        dbi_ref[0] += _dot(uv, giv, _TN)
        du_ref[...] = _dot(grv, br_ref[0], _NT) + _dot(giv, bi_ref[0], _NT) + d_ref[...] * dy
        run_comm(("last",))

    rev = lambda t: nt - 1 - t
    blk_u = pl.BlockSpec((tc, S5_BC), lambda g, t: (rev(t), g))
    blk_l = pl.BlockSpec((1, S5_BS), lambda g, t: (0, g))
    blk_b = pl.BlockSpec((1, S5_BC, S5_BS), lambda g, t: (g, 0, 0))
    blk_c = pl.BlockSpec((1, S5_BS, S5_BC), lambda g, t: (g, 0, 0))
    blk_d = pl.BlockSpec((1, S5_BC), lambda g, t: (0, g))
    blk_s = pl.BlockSpec((1, 1, S5_BS), lambda g, t: (rev(t), 0, g))
    return pl.pallas_call(
        body, grid=(S5_NB, nt),
        in_specs=[blk_u, blk_u, blk_s, blk_s, blk_l, blk_l, blk_b, blk_b, blk_c, blk_c, blk_d] + [_ANY_SPEC] * nci,
        out_specs=[blk_u, blk_b, blk_b, blk_c, blk_c, blk_l, blk_l, blk_d] + [_ANY_SPEC] * nco,
        out_shape=[jax.ShapeDtypeStruct((length, D_MODEL), F32),
                   jax.ShapeDtypeStruct((S5_NB, S5_BC, S5_BS), F32), jax.ShapeDtypeStruct((S5_NB, S5_BC, S5_BS), F32),
                   jax.ShapeDtypeStruct((S5_NB, S5_BS, S5_BC), F32), jax.ShapeDtypeStruct((S5_NB, S5_BS, S5_BC), F32),
                   jax.ShapeDtypeStruct((1, S5_NSTATE), F32), jax.ShapeDtypeStruct((1, S5_NSTATE), F32),
                   jax.ShapeDtypeStruct((1, D_MODEL), F32)] + comm.out_shapes,
        scratch_shapes=[pltpu.VMEM((tc // 8, 8, S5_BS), F32), pltpu.VMEM((tc // 8, 8, S5_BS), F32),
                        pltpu.VMEM((tc // 8, 8, S5_BS), F32), pltpu.VMEM((tc // 8, 8, S5_BS), F32),
                        pltpu.VMEM((1, S5_BS), F32), pltpu.VMEM((1, S5_BS), F32)] + comm.scratch,
        name="s5_bwd", compiler_params=_params(("arbitrary", "arbitrary")),
    )(u, dz, st_re, st_im, lam_re, lam_im, bbd_re, bbd_im, cbd_re, cbd_im, d_skip, *comm.ins)


def _blockdiag_b(bt):
    eye = jnp.eye(S5_GB, dtype=bt.dtype)
    t = bt.reshape(S5_GROUP, S5_NB, S5_GB, S5_STATE)
    return jnp.einsum("ab,hnbp->nahbp", eye, t).reshape(S5_NB, S5_BC, S5_BS)


def _unblockdiag_b(m):
    eye = jnp.eye(S5_GB, dtype=m.dtype)
    t = m.reshape(S5_NB, S5_GB, S5_GROUP, S5_GB, S5_STATE)
    return jnp.einsum("ab,nahbp->hnbp", eye, t).reshape(S5_GROUP, S5_NSTATE)


def _blockdiag_c(c):
    eye = jnp.eye(S5_GB, dtype=c.dtype)
    t = c.reshape(S5_NB, S5_GB, S5_GROUP, S5_STATE)
    return jnp.einsum("ab,nbhp->napbh", eye, t).reshape(S5_NB, S5_BS, S5_BC)


def _unblockdiag_c(m):
    eye = jnp.eye(S5_GB, dtype=m.dtype)
    t = m.reshape(S5_NB, S5_GB, S5_STATE, S5_GB, S5_GROUP)
    return jnp.einsum("ab,napbh->nbhp", eye, t).reshape(S5_GROUPS, S5_GROUP, S5_STATE)


def _chunk_rows(c):
    return slice(c * GLA_CHUNK, (c + 1) * GLA_CHUNK)


def _per_chunk(rows, nb):
    return jnp.concatenate([jnp.broadcast_to(r, (GLA_CHUNK, r.shape[1])) for r in rows], axis=0)


def _gla_gates(glr, wg2, bg, tri, nb):
    pre = _dot(glr, wg2) + bg
    la = _logsig(pre) * (1.0 / GLA_TAU)
    gc = _dot_01_left(tri, la)
    gend = _per_chunk([gc[(c + 1) * GLA_CHUNK - 1:(c + 1) * GLA_CHUNK, :] for c in range(nb)], nb)
    return pre, jnp.exp(gend - gc), jnp.exp(gend)


def _gla_specs(nblk, nb, rev):
    ix = (lambda n: nblk - 1 - n) if rev else (lambda n: n)
    c = GLA_CHUNK * nb
    return dict(
        q=pl.BlockSpec((c, GLA_QK), lambda n: (ix(n), 0)),
        k=pl.BlockSpec((c, GLA_QK), lambda n: (ix(n), 1)),
        v=pl.BlockSpec((c, D_MODEL), lambda n: (ix(n), 1)),
        r=pl.BlockSpec((c, D_MODEL), lambda n: (ix(n), 2)),
        glr=pl.BlockSpec((c, GLA_RANK_PAD), lambda n: (ix(n), (2 * GLA_QK + 2 * D_MODEL) // GLA_RANK_PAD)),
        wg2=pl.BlockSpec((GLA_RANK_PAD, GLA_QK), lambda n: (0, 0)),
        bg=pl.BlockSpec((1, GLA_QK), lambda n: (0, 0)),
        gn=pl.BlockSpec((1, D_MODEL), lambda n: (0, 0)),
        tri=pl.BlockSpec((c, c), lambda n: (0, 0)),
        row=pl.BlockSpec((c, D_MODEL), lambda n: (ix(n), 0)),
        rowp=pl.BlockSpec((c, GLA_INP), lambda n: (ix(n), 0)),
        st=pl.BlockSpec((nb, GLA_HEADS, GLA_DV, GLA_DK), lambda n: (ix(n), 0, 0, 0)),
    )


def _gla_fwd(proj, wg2, bg, gn, tri, comm):
    length = proj.shape[0]
    nc = length // GLA_CHUNK
    nb = tri.shape[0] // GLA_CHUNK
    nblk = nc // nb
    scale = GLA_DK ** -0.5
    nci, nco = len(comm.ins), len(comm.out_shapes)

    def body(*refs):
        q_ref, k_ref, v_ref, r_ref, glr_ref, wg2_ref, bg_ref, gn_ref, tri_ref = refs[:9]
        og_ref, sp_ref = refs[9 + nci:11 + nci]
        st = refs[11 + nci + nco]
        run_comm = functools.partial(comm.run, pl.program_id(0), nblk, refs[9:9 + nci], refs[11 + nci:11 + nci + nco],
                                     refs[12 + nci + nco:])
        run_comm(("first", "mid"))

        @pl.when(pl.program_id(0) == 0)
        def _():
            st[...] = jnp.zeros_like(st)
        _, e, dec = _gla_gates(glr_ref[...], wg2_ref[...], bg_ref[...], tri_ref[...], nb)
        kd = k_ref[...].astype(F32) * e
        q = q_ref[...].astype(F32) * scale
        for h in range(GLA_HEADS):
            sk = slice(h * GLA_DK, (h + 1) * GLA_DK)
            sv = slice(h * GLA_DV, (h + 1) * GLA_DV)
            state = st[h]
            for c in range(nb):
                rows = _chunk_rows(c)
                sp_ref[c, h] = state
                state = dec[c * GLA_CHUNK:c * GLA_CHUNK + 1, sk] * state + _dot(v_ref[rows, sv], kd[rows, sk], _TN)
                o = _dot(q[rows, sk], state, _NT)
                on = o * _rms(o)
                og_ref[rows, sv] = (on * gn_ref[:, sv] * _silu(r_ref[rows, sv].astype(F32))).astype(og_ref.dtype)
            st[h] = state
        run_comm(("last",))

    s = _gla_specs(nblk, nb, False)
    return pl.pallas_call(
        body, grid=(nblk,),
        in_specs=[s["q"], s["k"], s["v"], s["r"], s["glr"], s["wg2"], s["bg"], s["gn"], s["tri"]] + [_ANY_SPEC] * nci,
        out_specs=[s["row"], s["st"]] + [_ANY_SPEC] * nco,
        out_shape=[jax.ShapeDtypeStruct((length, D_MODEL), MXU_DT),
                   jax.ShapeDtypeStruct((nc, GLA_HEADS, GLA_DV, GLA_DK), F32)] + comm.out_shapes,
        scratch_shapes=[pltpu.VMEM((GLA_HEADS, GLA_DV, GLA_DK), F32)] + comm.scratch,
        name="gla_fwd", compiler_params=_params(("arbitrary",)),
    )(proj, proj, proj, proj, proj, wg2, bg, gn, tri, *comm.ins)


def _gla_bwd(proj, d_og, s_prev, wg2, bg, gn, tri):
    length = proj.shape[0]
    nc = length // GLA_CHUNK
    nb = tri.shape[0] // GLA_CHUNK
    nblk = nc // nb
    scale = GLA_DK ** -0.5

    def body(q_ref, k_ref, v_ref, r_ref, glr_ref, dog_ref, sp_ref, wg2_ref, bg_ref, gn_ref, tri_ref,
             dp_ref, dwg2_ref, dbg_ref, dgn_ref, dst):
        dq_ref = dp_ref.at[:, pl.ds(0, GLA_QK)]
        dk_ref = dp_ref.at[:, pl.ds(GLA_QK, GLA_QK)]
        dv_ref = dp_ref.at[:, pl.ds(2 * GLA_QK, D_MODEL)]
        dr_ref = dp_ref.at[:, pl.ds(2 * GLA_QK + D_MODEL, D_MODEL)]
        dglr_ref = dp_ref.at[:, pl.ds(2 * GLA_QK + 2 * D_MODEL, GLA_RANK_PAD)]

        @pl.when(pl.program_id(0) == 0)
        def _():
            dst[...] = jnp.zeros_like(dst)
            dwg2_ref[...] = jnp.zeros_like(dwg2_ref)
            dbg_ref[...] = jnp.zeros_like(dbg_ref)
            dgn_ref[...] = jnp.zeros_like(dgn_ref)
        glr = glr_ref[...]
        pre, e, dec = _gla_gates(glr, wg2_ref[...], bg_ref[...], tri_ref[...], nb)
        k = k_ref[...].astype(F32)
        kd = k * e
        q = q_ref[...].astype(F32) * scale
        dkd_heads, ddec_heads = [], []
        for h in range(GLA_HEADS):
            sk = slice(h * GLA_DK, (h + 1) * GLA_DK)
            sv = slice(h * GLA_DV, (h + 1) * GLA_DV)
            gnh = gn_ref[:, sv]
            stps, grads_in = [], []
            dgn = jnp.zeros((1, GLA_DV), F32)
            for c in range(nb):
                rows = _chunk_rows(c)
                stp = sp_ref[c, h]
                stn = dec[c * GLA_CHUNK:c * GLA_CHUNK + 1, sk] * stp + _dot(v_ref[rows, sv], kd[rows, sk], _TN)
                o = _dot(q[rows, sk], stn, _NT)
                rinv = _rms(o)
                on = o * rinv
                rv = r_ref[rows, sv].astype(F32)
                sg = _sigmoid(rv)
                dog = dog_ref[rows, sv].astype(F32)
                d_ong = dog * (rv * sg)
                dr_ref[rows, sv] = (dog * (on * gnh) * (sg * (1.0 + rv * (1.0 - sg)))).astype(dr_ref.dtype)
                dgn = dgn + jnp.sum(d_ong * on, axis=0, keepdims=True)
                d_on = d_ong * gnh
                do = rinv * (d_on - on * jnp.mean(d_on * on, axis=-1, keepdims=True))
                dq_ref[rows, sk] = (_dot(do, stn) * scale).astype(dq_ref.dtype)
                stps.append(stp)
                grads_in.append(_dot(do, q[rows, sk], _TN))
            dgn_ref[:, sv] += dgn
            carry = dst[h]
            dkd_rows, ddec_rows = [None] * nb, [None] * nb
            for c in reversed(range(nb)):
                rows = _chunk_rows(c)
                dstn = carry + grads_in[c]
                carry = dec[c * GLA_CHUNK:c * GLA_CHUNK + 1, sk] * dstn
                ddec_rows[c] = jnp.sum(dstn * stps[c], axis=0, keepdims=True)
                dv_ref[rows, sv] = _dot(kd[rows, sk], dstn, _NT).astype(dv_ref.dtype)
                dkd_rows[c] = _dot(v_ref[rows, sv], dstn)
            dst[h] = carry
            dkd_heads.append(jnp.concatenate(dkd_rows, axis=0))
            ddec_heads.append(_per_chunk(ddec_rows, nb))
        dkd = jnp.concatenate(dkd_heads, axis=1)
        ddec = jnp.concatenate(ddec_heads, axis=1)
        dk_ref[...] = (dkd * e).astype(dk_ref.dtype)
        w = dkd * kd
        dgend = _per_chunk([jnp.sum(w[_chunk_rows(c)], axis=0, keepdims=True) for c in range(nb)], nb) + ddec * dec
        dla = dgend - _dot_01_left(tri_ref[...], w, _TN)
        dpre = dla * (1.0 - _sigmoid(pre)) * (1.0 / GLA_TAU)
        dwg2_ref[...] += _dot(glr, dpre, _TN)
        dbg_ref[...] += jnp.sum(dpre, axis=0, keepdims=True)
        dglr_ref[...] = _dot(dpre, wg2_ref[...], _NT).astype(dglr_ref.dtype)

    s = _gla_specs(nblk, nb, True)
    return pl.pallas_call(
        body, grid=(nblk,),
        in_specs=[s["q"], s["k"], s["v"], s["r"], s["glr"], s["row"], s["st"], s["wg2"], s["bg"], s["gn"], s["tri"]],
        out_specs=[s["rowp"], s["wg2"], s["bg"], s["gn"]],
        out_shape=[jax.ShapeDtypeStruct((length, GLA_INP), MXU_DT),
                   jax.ShapeDtypeStruct((GLA_RANK_PAD, GLA_QK), F32), jax.ShapeDtypeStruct((1, GLA_QK), F32),
                   jax.ShapeDtypeStruct((1, D_MODEL), F32)],
        scratch_shapes=[pltpu.VMEM((GLA_HEADS, GLA_DV, GLA_DK), F32)],
        name="gla_bwd", compiler_params=_params(("arbitrary",)),
    )(proj, proj, proj, proj, proj, d_og, s_prev, wg2, bg, gn, tri)


def _local_step(x, tgt, mods, nrm, s5p, w, shards, core, *, row_tile=256, s5_tc=512):
    length = x.shape[0]
    tmm = 512
    glu_sh, gin_sh, gout_sh, ff1a_sh, ff1b_sh, ff2a_sh, ff2b_sh = shards
    w = dict(w)
    rc = functools.partial(_rowcall, tile=row_tile)
    (sh1a, sc1a, gt1a, sh2a, sc2a, gt2a), (sh1b, sc1b, gt1b, sh2b, sc2b, gt2b) = mods
    vec = (1, D_MODEL)
    row32, row16 = (D_MODEL, F32), (D_MODEL, MXU_DT)

    (h0,) = rc(lambda *a: _f_pn(*a)[1:], [_full(x)], [nrm["mix"][0], sc1a, sh1a], [row32], [], name="pn0")
    lam_re, lam_im, bb_re, bb_im = _s5_prep(s5p["a_re"], s5p["a_im"], s5p["log_dt"], s5p["bt_re"], s5p["bt_im"], s5p["e01"])
    bbd_re = _blockdiag_b(bb_re.reshape(S5_GROUP, S5_GROUPS, S5_STATE)).astype(MXU_DT)
    bbd_im = _blockdiag_b(bb_im.reshape(S5_GROUP, S5_GROUPS, S5_STATE)).astype(MXU_DT)
    cbd_re = _blockdiag_c(s5p["c_re"]).astype(MXU_DT)
    cbd_im = _blockdiag_c(s5p["c_im"]).astype(MXU_DT)
    z0, st_re, st_im, glu_s, ff1a_s, ff2a_s = _s5_fwd(
        h0, lam_re, lam_im, bbd_re, bbd_im, cbd_re, cbd_im, s5p["d"], _ag_comm([glu_sh, ff1a_sh, ff2a_sh]), tc=s5_tc)
    w["glu"] = glu_s.transpose(1, 0, 2).reshape(D_MODEL, 2 * D_MODEL)
    w["ff1"] = [ff1a_s.transpose(1, 0, 2).reshape(D_MODEL, D_FF), None]
    w["ff2"] = [ff2a_s.reshape(D_FF, D_MODEL), None]
    vg = _mm(z0, w["glu"], "nn", F32, tm=tmm, tn=2048, name="glu_mm")
    x1, h1 = rc(_f_glu_res_pn, [_full(x), (vg, D_MODEL, 0), (vg, D_MODEL, 1)], [gt1a, nrm["mlp"][0], sc2a, sh2a],
                [row32, row16], [], name="node1")
    relu = lambda acc: jnp.maximum(acc, 0.0)
    sq = lambda a: a * a
    a0, gin_s = _mm(h1, w["ff1"][0], "nn", MXU_DT, tm=tmm, tn=2048, name="ff1a", out_fn=relu, comm=_ag_comm([gin_sh]))
    f0, ff1b_s = _mm(a0, w["ff2"][0], "nn", F32, tm=tmm, tn=1024, name="ff2a", a_fn=sq, comm=_ag_comm([ff1b_sh]))
    gin_full = gin_s.transpose(1, 0, 2).reshape(D_MODEL, GLA_IN)
    q_, k_, v_, glr_, r_ = jnp.split(gin_full, [GLA_QK, 2 * GLA_QK, 2 * GLA_QK + D_MODEL, 2 * GLA_QK + D_MODEL + GLA_RANK], axis=1)
    w["gin"] = jnp.concatenate([q_, k_, v_, r_, glr_, jnp.zeros((D_MODEL, GLA_RANK_PAD - GLA_RANK), MXU_DT)], axis=1)
    w["ff1"][1] = ff1b_s.transpose(1, 0, 2).reshape(D_MODEL, D_FF)
    x2, h2 = rc(_f_res_pn, [_full(x1), _full(f0)], [gt2a, nrm["mix"][1], sc1b, sh1b], [row32, row16], [], name="node2")
    proj = _mm(h2, w["gin"], "nn", MXU_DT, tm=tmm, tn=GLA_INP, name="gla_in")
    og, s_prev, gout_s, ff2b_s = _gla_fwd(proj, w["wg2"], w["bg"], w["gn"], w["tri"], _ag_comm([gout_sh, ff2b_sh]))
    w["gout"] = gout_s.reshape(D_MODEL, D_MODEL)
    w["ff2"][1] = ff2b_s.reshape(D_FF, D_MODEL)
    y1 = _mm(og, w["gout"], "nn", F32, tm=tmm, tn=1024, name="gla_out")
    x3, h3 = rc(_f_res_pn, [_full(x2), _full(y1)], [gt1b, nrm["mlp"][1], sc2b, sh2b], [row32, row16], [], name="node3")
    a1 = _mm(h3, w["ff1"][1], "nn", MXU_DT, tm=tmm, tn=2048, name="ff1b", out_fn=relu)
    f1 = _mm(a1, w["ff2"][1], "nn", F32, tm=tmm, tn=1024, name="ff2b", a_fn=sq)

    g = {}
    dx, df, g["gt2b"], g["nf"], loss = rc(_g_final, [_full(x3), _full(f1), _full(tgt)], [gt2b, nrm["final"]],
                                          [row32, row16], [vec, vec, (8, 128)], name="final")

    def mlp_bwd(df, a, h, w1, w2, tag):
        dw2 = _mm(a, df, "tn", MXU_DT, tm=1024, tn=1024, name="dff2" + tag, a_fn=sq)
        du = _mm(df, w2, "nt", MXU_DT, tm=tmm, tn=2048, name="dact" + tag, extra=a, out_fn=lambda acc, e: acc * (2.0 * e))
        dw1 = _mm(h, du, "tn", MXU_DT, tm=1024, tn=D_FF // N_DEV, name="dff1" + tag, by_owner=True)
        dh = _mm(du, w1, "nt", F32, tm=tmm, tn=1024, name="dh" + tag)
        return dw1, dw2, dh

    def node_bwd(f, prim_rows, cots, pars, row_want, outs, name):
        nrow = len(prim_rows)
        return rc(_vjp_of(f, nrow, len(cots), row_want), prim_rows + cots, pars, outs, [vec] * len(pars), name=name)

    g["ff1b"], g["ff2b"], dh3 = mlp_bwd(df, a1, h3, w["ff1"][1], w["ff2"][1], "b")
    dx, dy1, g["gt1b"], g["mlp1"], g["sc2b"], g["sh2b"] = node_bwd(
        _f_res_pn, [_full(x2), _full(y1)], [_full(dx), _full(dh3)], [gt1b, nrm["mlp"][1], sc2b, sh2b], (0, 1),
        [row32, row16], "node3_bwd")
    g["gout"] = _mm(og, dy1, "tn", MXU_DT, tm=512, tn=1024, name="dgout")
    d_og = _mm(dy1, w["gout"], "nt", MXU_DT, tm=tmm, tn=1024, name="dog")
    dproj, g["wg2"], g["bg"], g["gn"] = _gla_bwd(proj, d_og, s_prev, w["wg2"], w["bg"], w["gn"], w["tri"])
    g["gin"] = _mm(h2, dproj, "tn", MXU_DT, tm=512, tn=640, name="dgin")
    dh2 = _mm(dproj, w["gin"], "nt", F32, tm=tmm, tn=1024, name="dh2")
    dx, df0, g["gt2a"], g["mix1"], g["sc1b"], g["sh1b"] = node_bwd(
        _f_res_pn, [_full(x1), _full(f0)], [_full(dx), _full(dh2)], [gt2a, nrm["mix"][1], sc1b, sh1b], (0, 1),
        [row32, row16], "node2_bwd")
    g["ff1a"], g["ff2a"], dh1 = mlp_bwd(df0, a0, h1, w["ff1"][0], w["ff2"][0], "a")
    glu_vjp = _vjp_of(_f_glu_res_pn, 3, 2, (0, 1, 2))

    def glu_bwd(*a):
        r = glu_vjp(*a)
        return (r[0], jnp.concatenate([r[1], r[2]], axis=1)) + r[3:]

    dx, dvg, g["gt1a"], g["mlp0"], g["sc2a"], g["sh2a"] = rc(
        glu_bwd, [_full(x), (vg, D_MODEL, 0), (vg, D_MODEL, 1), _full(dx), _full(dh1)], [gt1a, nrm["mlp"][0], sc2a, sh2a],
        [row32, (2 * D_MODEL, MXU_DT)], [vec] * 4, name="node1_bwd")
    gin_g = g.pop("gin")
    gin_g = jnp.concatenate([gin_g[:, :2 * GLA_QK + D_MODEL], gin_g[:, GLA_INP - GLA_RANK_PAD:GLA_INP - GLA_RANK_PAD + GLA_RANK],
                             gin_g[:, 2 * GLA_QK + D_MODEL:2 * GLA_QK + 2 * D_MODEL]], axis=1)
    per_owner = [gin_g.reshape(D_MODEL, N_DEV, GLA_IN // N_DEV).transpose(1, 0, 2),
                 g.pop("gout").reshape(N_DEV, D_MODEL // N_DEV, D_MODEL), g.pop("ff1a"), g.pop("ff1b"),
                 g.pop("ff2a").reshape(N_DEV, D_FF // N_DEV, D_MODEL), g.pop("ff2b").reshape(N_DEV, D_FF // N_DEV, D_MODEL)]
    res = _mm(z0, dvg, "tn", MXU_DT, tm=1024, tn=2 * D_MODEL // N_DEV, name="dglu", by_owner=True, comm=_sibling_comm(per_owner))
    glu_g, from_sibling = res[0], list(res[1:])
    dz0, glu_sib = _mm(dvg, w["glu"], "nt", MXU_DT, tm=tmm, tn=1024, name="dz0", comm=_sibling_comm([glu_g]))
    per_owner = [glu_g] + per_owner
    chip_sum = _pair_add(per_owner, [glu_sib] + from_sibling, core, name="rs_add")
    res = _s5_bwd(h0, dz0, st_re, st_im, lam_re, lam_im, bbd_re, bbd_im, cbd_re, cbd_im, s5p["d"], _chips_comm(chip_sum), tc=s5_tc)
    du0, dbbd_re, dbbd_im, dcbd_re, dcbd_im, dlam_re, dlam_im, g["s5_d"] = res[:8]
    from_chips = res[8:]
    g["s5_c_re"] = _unblockdiag_c(dcbd_re)
    g["s5_c_im"] = _unblockdiag_c(dcbd_im)
    g["s5_a_re"], g["s5_a_im"], g["s5_log_dt"], g["s5_bt_re"], g["s5_bt_im"] = _s5_prep_bwd(
        s5p["a_re"], s5p["a_im"], s5p["log_dt"], s5p["bt_re"], s5p["bt_im"], s5p["e01"],
        dlam_re, dlam_im, _unblockdiag_b(dbbd_re), _unblockdiag_b(dbbd_im))
    grad_x, g["mix0"], g["sc1a"], g["sh1a"] = node_bwd(
        _f_pn, [_full(x)], [_full(dx), _full(du0)], [nrm["mix"][0], sc1a, sh1a], (0,), [row32], "node0_bwd")
    return loss[0, 0], grad_x, g, chip_sum, from_chips


_MESH = pl.DeviceIdType.MESH
_VMEM_SPEC = pl.BlockSpec(memory_space=pltpu.VMEM)
_ANY_SPEC = pl.BlockSpec(memory_space=pl.ANY)


def _my_place():
    ix, iy, ic = lax.axis_index("x"), lax.axis_index("y"), lax.axis_index("c")
    return ix, iy, ic


def _exchange(x, *, gather, name):
    r = x.shape[-2]

    def body(x_ref, o_ref, ssem, rsem):
        ix, iy, ic = _my_place()
        me = 4 * ix + 2 * iy + ic
        if gather:
            o_ref[me] = x_ref[...]
        else:
            o_ref[me] = x_ref[me]
        copies = []
        for k in range(1, N_DEV):
            tx, ty, tc = ix ^ (k >> 2), iy ^ ((k >> 1) & 1), ic ^ (k & 1)
            src = x_ref if gather else x_ref.at[4 * tx + 2 * ty + tc]
            cp = pltpu.make_async_remote_copy(src_ref=src, dst_ref=o_ref.at[me], send_sem=ssem.at[k - 1],
                                              recv_sem=rsem.at[k - 1], device_id=(tx, ty, tc), device_id_type=_MESH)
            cp.start()
            copies.append(cp)
        for cp in copies:
            cp.wait()

    return pl.pallas_call(
        body, out_shape=jax.ShapeDtypeStruct((N_DEV, r, 128), x.dtype), in_specs=[_VMEM_SPEC], out_specs=_VMEM_SPEC,
        scratch_shapes=[pltpu.SemaphoreType.DMA((N_DEV - 1,)), pltpu.SemaphoreType.DMA((N_DEV - 1,))], name=name,
    )(x)


class _Comm:
    def __init__(self, ins, out_shapes, scratch, phases):
        self.ins, self.out_shapes, self.scratch, self.phases = list(ins), list(out_shapes), list(scratch), phases

    def run(self, step, n_steps, in_refs, out_refs, scratch_refs, only):
        when = {"first": 0, "mid": (7 * n_steps) // 8, "last": n_steps - 1}
        for phase, fn in self.phases:
            if phase in only:
                pl.when(step == when[phase])(functools.partial(fn, in_refs, out_refs, scratch_refs))


def _ag_comm(xs):
    n = len(xs)

    def parts():
        ix, iy, ic = _my_place()
        return ic, (ix, iy, ic), (ix, iy, 1 - ic), [(1 - ix, iy), (ix, 1 - iy), (1 - ix, 1 - iy)]

    def copy(ins, outs, sc, a, k, block, to, from_x=False):
        px, py, pc = block
        slot = outs[a].at[4 * px + 2 * py + pc]
        return pltpu.make_async_remote_copy(
            src_ref=ins[a] if from_x else slot, dst_ref=slot, send_sem=sc[0].at[7 * a + k], recv_sem=sc[1].at[7 * a + k],
            device_id=to, device_id_type=_MESH)

    def local(ins, outs, sc, a, me):
        return pltpu.make_async_copy(ins[a], outs[a].at[4 * me[0] + 2 * me[1] + me[2]], sc[2].at[a])

    def start(ins, outs, sc):
        ic, me, sibling, chips = parts()
        for a in range(n):
            local(ins, outs, sc, a, me).start()
            copy(ins, outs, sc, a, 0, me, sibling, True).start()
            for j, chip in enumerate(chips):
                copy(ins, outs, sc, a, 1 + j, me, (*chip, ic), True).start()

    def forward(ins, outs, sc):
        ic, me, sibling, chips = parts()
        for a in range(n):
            for j, chip in enumerate(chips):
                copy(ins, outs, sc, a, 1 + j, (*chip, ic), me).wait_recv()
                copy(ins, outs, sc, a, 4 + j, (*chip, ic), sibling).start()

    def finish(ins, outs, sc):
        ic, me, sibling, chips = parts()
        for a in range(n):
            copy(ins, outs, sc, a, 0, sibling, me).wait_recv()
            for j, chip in enumerate(chips):
                copy(ins, outs, sc, a, 4 + j, (*chip, 1 - ic), me).wait_recv()
        for a in range(n):
            copy(ins, outs, sc, a, 0, me, sibling, True).wait_send()
            for j, chip in enumerate(chips):
                copy(ins, outs, sc, a, 1 + j, me, (*chip, ic), True).wait_send()
                copy(ins, outs, sc, a, 4 + j, (*chip, ic), sibling).wait_send()
            local(ins, outs, sc, a, me).wait()

    return _Comm(xs, [jax.ShapeDtypeStruct((N_DEV,) + x.shape, x.dtype) for x in xs],
                 [pltpu.SemaphoreType.DMA((7 * n,)), pltpu.SemaphoreType.DMA((7 * n,)), pltpu.SemaphoreType.DMA((n,))],
                 [("first", start), ("mid", forward), ("last", finish)])


def _chips_comm(ps):
    n = len(ps)

    def copies(ins, outs, sc):
        ix, iy, ic = _my_place()
        out = []
        for a in range(n):
            for k in range(1, 4):
                tx, ty = ix ^ (k >> 1), iy ^ (k & 1)
                out.append(pltpu.make_async_remote_copy(
                    src_ref=ins[a].at[2 * tx + ty], dst_ref=outs[a].at[k - 1], send_sem=sc[0].at[3 * a + k - 1],
                    recv_sem=sc[1].at[3 * a + k - 1], device_id=(tx, ty, ic), device_id_type=_MESH))
        return out

    def start(ins, outs, sc):
        for cp in copies(ins, outs, sc):
            cp.start()

    def finish(ins, outs, sc):
        for cp in copies(ins, outs, sc):
            cp.wait()

    return _Comm(ps, [jax.ShapeDtypeStruct((3,) + p.shape[1:], p.dtype) for p in ps],
                 [pltpu.SemaphoreType.DMA((3 * n,)), pltpu.SemaphoreType.DMA((3 * n,))], [("first", start), ("last", finish)])


def _sibling_comm(gs):
    n = len(gs)

    def copies(ins, outs, sc):
        ix, iy, ic = _my_place()
        return [pltpu.make_async_remote_copy(src_ref=ins[a].at[2 * q + 1 - ic], dst_ref=outs[a].at[q], send_sem=sc[0].at[4 * a + q],
                                             recv_sem=sc[1].at[4 * a + q], device_id=(ix, iy, 1 - ic), device_id_type=_MESH)
                for a in range(n) for q in range(4)]

    def start(ins, outs, sc):
        for cp in copies(ins, outs, sc):
            cp.start()

    def finish(ins, outs, sc):
        for cp in copies(ins, outs, sc):
            cp.wait()

    return _Comm(gs, [jax.ShapeDtypeStruct((4,) + g.shape[1:], g.dtype) for g in gs],
                 [pltpu.SemaphoreType.DMA((4 * n,)), pltpu.SemaphoreType.DMA((4 * n,))], [("first", start), ("last", finish)])


def _ada_fwd(c_all, w_ada, b_cols):
    def body(c_ref, w_ref, b_ref, o_ref):
        cs = _silu(c_ref[...])
        for i in range(2):
            o_ref[i] = _dot(cs, w_ref[i]) + b_ref[pl.ds(i, 1), :]
    return pl.pallas_call(body, out_shape=jax.ShapeDtypeStruct((2, N_DEV, w_ada.shape[2]), F32), name="ada_fwd",
                          compiler_params=pltpu.CompilerParams(vmem_limit_bytes=VMEM_LIMIT))(c_all, w_ada, b_cols)


def _ada_bwd(c_all, dm):
    def body(c_ref, d_ref, o_ref):
        cs = _silu(c_ref[...])
        for i in range(2):
            o_ref[i] = _dot(cs, d_ref[i], _TN)
    return pl.pallas_call(body, out_shape=jax.ShapeDtypeStruct((2, D_MODEL, dm.shape[2]), F32), name="ada_bwd",
                          compiler_params=pltpu.CompilerParams(vmem_limit_bytes=VMEM_LIMIT))(c_all, dm)


def _pair_add(gs, recvs, core, *, name):
    n = len(gs)

    def body(core_ref, *refs):
        for a in range(n):
            refs[2 * n + a][...] = (refs[a][...].astype(F32) + refs[n + a][...].astype(F32)).astype(refs[2 * n + a].dtype)

    own = [pl.BlockSpec((1,) + g.shape[1:], lambda q, core_ref: (2 * q + core_ref[0], 0, 0)) for g in gs]
    slab = [pl.BlockSpec((1,) + g.shape[1:], lambda q, core_ref: (q, 0, 0)) for g in gs]
    grid_spec = pltpu.PrefetchScalarGridSpec(num_scalar_prefetch=1, grid=(4,), in_specs=own + slab, out_specs=slab)
    return pl.pallas_call(body, grid_spec=grid_spec, out_shape=[jax.ShapeDtypeStruct((4,) + g.shape[1:], g.dtype) for g in gs],
                          name=name, compiler_params=_params(("parallel",)))(core, *gs, *recvs)


def _sum_slots(x, *, name):
    def body(x_ref, o_ref):
        acc = x_ref[0]
        for s in range(1, N_DEV):
            acc = acc + x_ref[s]
        o_ref[...] = acc
    return pl.pallas_call(body, out_shape=jax.ShapeDtypeStruct(x.shape[1:], F32), name=name)(x)


def _adamw_math(w, m, v, g):
    mn = ADAM_B1 * m + (1.0 - ADAM_B1) * g
    vn = ADAM_B2 * v + (1.0 - ADAM_B2) * (g * g)
    m_hat = mn / (1.0 - ADAM_B1 ** ADAM_STEP)
    v_hat = vn / (1.0 - ADAM_B2 ** ADAM_STEP)
    return -ADAM_LR * (m_hat / (jnp.sqrt(v_hat) + ADAM_EPS) + ADAM_WD * w), mn, vn


def _adamw_multi(ws, ms, vs, gs, *, name):
    n = len(ws)

    def body(*refs):
        for i in range(n):
            g = refs[3 * n + i][...]
            o = refs[4 * n + 4 * i:4 * n + 4 * i + 4]
            o[0][...] = g
            o[1][...], o[2][...], o[3][...] = _adamw_math(refs[i][...], refs[n + i][...], refs[2 * n + i][...], g)

    out_shape = [jax.ShapeDtypeStruct(w.shape, F32) for w in ws for _ in range(4)]
    res = pl.pallas_call(body, out_shape=out_shape, name=name,
                         compiler_params=pltpu.CompilerParams(vmem_limit_bytes=VMEM_LIMIT))(*ws, *ms, *vs, *gs)
    return [res[4 * i:4 * i + 4] for i in range(n)]


def _adamw(w, m, v, gparts, *, tile, name, sel=None):
    r, cdim = w.shape
    ng = len(gparts)
    sel = jnp.zeros((1,), jnp.int32) if sel is None else sel

    def body(*refs):
        w_ref, m_ref, v_ref = refs[1:4]
        g = None
        for p, part in zip(refs[4:4 + ng], gparts):
            pv = (p[0] if isinstance(part, tuple) else p[...]).astype(F32)
            g = pv if g is None else g + pv
        g_ref, d_ref, nm_ref, nv_ref = refs[4 + ng:]
        g_ref[...] = g
        d_ref[...], nm_ref[...], nv_ref[...] = _adamw_math(w_ref[...], m_ref[...], v_ref[...], g)

    spec = pl.BlockSpec((tile, cdim), lambda i, s: (i, 0))

    def part_spec(part):
        if not isinstance(part, tuple):
            return spec
        slab = part[1]
        if slab is None:
            return pl.BlockSpec((1, tile, cdim), lambda i, s: (s[0], i, 0))
        return pl.BlockSpec((1, tile, cdim), lambda i, s: (slab, i, 0))

    grid_spec = pltpu.PrefetchScalarGridSpec(
        num_scalar_prefetch=1, grid=(r // tile,), in_specs=[spec] * 3 + [part_spec(p) for p in gparts], out_specs=[spec] * 4)
    return pl.pallas_call(
        body, grid_spec=grid_spec, out_shape=[jax.ShapeDtypeStruct(w.shape, F32)] * 4, name=name,
        compiler_params=_params(("parallel",)),
    )(sel, w, m, v, *[p[0] if isinstance(p, tuple) else p for p in gparts])


_REP_ROWS = 272
_REP_SIZE = 2 * 1024 * 2 + 4096 * 2 + 64 + 65536 * 4 + 1024 + 1024


def _pad_rows(v, rows):
    return jnp.pad(v.reshape(-1), (0, rows * 128 - v.size)).reshape(rows, 128)


def kernel(x, c, w_ada, b_ada, norm_mix, norm_mlp, s5_a_re, s5_a_im, s5_log_dt, s5_b_re, s5_b_im, s5_c_re, s5_c_im, s5_d, s5_w_glu, gla_w_in, gla_w_gate2, gla_b_gate, gla_g_norm, gla_w_out, w_ff1, w_ff2, norm_final, loss_target, m_w_ada, m_b_ada, m_norm_mix, m_norm_mlp, m_s5_a_re, m_s5_a_im, m_s5_log_dt, m_s5_b_re, m_s5_b_im, m_s5_c_re, m_s5_c_im, m_s5_d, m_s5_w_glu, m_gla_w_in, m_gla_w_gate2, m_gla_b_gate, m_gla_g_norm, m_gla_w_out, m_w_ff1, m_w_ff2, m_norm_final, v_w_ada, v_b_ada, v_norm_mix, v_norm_mlp, v_s5_a_re, v_s5_a_im, v_s5_log_dt, v_s5_b_re, v_s5_b_im, v_s5_c_re, v_s5_c_im, v_s5_d, v_s5_w_glu, v_gla_w_in, v_gla_w_gate2, v_gla_b_gate, v_gla_g_norm, v_gla_w_out, v_w_ff1, v_w_ff2, v_norm_final):
    ix, iy, ic = _my_place()
    me = 4 * ix + 2 * iy + ic
    ada_w = w_ada.shape[2]

    msg = jnp.concatenate([c.reshape(8, 128), gla_w_gate2[0].reshape(8, 128), _pad_rows(gla_b_gate, 1),
                           gla_g_norm.reshape(1, 128), jnp.zeros((6, 128), F32)])
    got = _exchange(msg, gather=True, name="gather_small")
    c_all = got[:, 0:8].reshape(N_DEV, D_MODEL)
    wg2 = got[:, 8:16].reshape(N_DEV, GLA_RANK, 64).transpose(1, 0, 2).reshape(GLA_RANK, GLA_QK)
    bg = got[:, 16, :64].reshape(1, GLA_QK)
    gn = got[:, 17, :].reshape(1, D_MODEL)

    b_cols = lax.dynamic_slice_in_dim(b_ada, me * ada_w, ada_w, axis=1)
    mod_cols = _ada_fwd(c_all, w_ada, b_cols)
    pay = jnp.pad(mod_cols.transpose(1, 0, 2).reshape(N_DEV, 12, 128), ((0, 0), (0, 4), (0, 0)))
    mod = _exchange(pay, gather=False, name="a2a_mod")[:, :12].reshape(N_DEV, 2, ada_w).transpose(1, 0, 2).reshape(2, 6 * D_MODEL)
    mods = [[mod[i:i + 1, j * D_MODEL:(j + 1) * D_MODEL] for j in range(6)] for i in range(2)]

    big_w = [s5_w_glu[0], gla_w_in[0], gla_w_out[0], w_ff1[0], w_ff1[1], w_ff2[0], w_ff2[1]]
    big_m = [m_s5_w_glu[0], m_gla_w_in[0], m_gla_w_out[0], m_w_ff1[0], m_w_ff1[1], m_w_ff2[0], m_w_ff2[1]]
    big_v = [v_s5_w_glu[0], v_gla_w_in[0], v_gla_w_out[0], v_w_ff1[0], v_w_ff1[1], v_w_ff2[0], v_w_ff2[1]]
    rows_i = lax.broadcasted_iota(jnp.int32, (GLA_NB * GLA_CHUNK, GLA_NB * GLA_CHUNK), 0)
    cols_i = lax.broadcasted_iota(jnp.int32, (GLA_NB * GLA_CHUNK, GLA_NB * GLA_CHUNK), 1)
    tri = ((rows_i // GLA_CHUNK == cols_i // GLA_CHUNK) & (cols_i <= rows_i)).astype(F32)
    w = dict(wg2=jnp.pad(wg2, ((0, GLA_RANK_PAD - GLA_RANK), (0, 0))), bg=bg, gn=gn, tri=tri)
    core = ic.reshape(1).astype(jnp.int32)
    chip = (2 * ix + iy).reshape(1).astype(jnp.int32)
    nrm = dict(mix=[norm_mix[i:i + 1] for i in range(2)], mlp=[norm_mlp[i:i + 1] for i in range(2)], final=norm_final.reshape(1, D_MODEL))
    e01 = (lax.broadcasted_iota(jnp.int32, (S5_GROUPS, S5_NSTATE), 1) // S5_STATE
           == lax.broadcasted_iota(jnp.int32, (S5_GROUPS, S5_NSTATE), 0)).astype(F32)
    s5p = dict(a_re=s5_a_re.reshape(1, S5_NSTATE), a_im=s5_a_im.reshape(1, S5_NSTATE), log_dt=s5_log_dt,
               bt_re=s5_b_re[0].transpose(2, 0, 1).reshape(S5_GROUP, S5_NSTATE),
               bt_im=s5_b_im[0].transpose(2, 0, 1).reshape(S5_GROUP, S5_NSTATE),
               c_re=s5_c_re[0], c_im=s5_c_im[0], d=s5_d, e01=e01)

    loss_local, grad_x, g, chip_sum, from_chips = _local_step(
        x[0], loss_target[0], mods, nrm, s5p, w, [a.astype(MXU_DT) for a in big_w], core)
    loss = lax.psum(loss_local, ("x", "y", "c"))
    big = []
    for i in range(len(big_w)):
        parts = [(chip_sum[i], None), (from_chips[i], 0), (from_chips[i], 1), (from_chips[i], 2)]
        big.append(_adamw(big_w[i], big_m[i], big_v[i], parts, tile=min(512, big_w[i].shape[0]), name="adamw_big%d" % i, sel=chip))

    rep = [jnp.concatenate([g["mix0"], g["mix1"]]), jnp.concatenate([g["mlp0"], g["mlp1"]]), g["s5_a_re"], g["s5_a_im"], g["s5_log_dt"],
           g["s5_bt_re"].reshape(S5_GROUP, S5_GROUPS, S5_STATE).transpose(1, 2, 0), g["s5_bt_im"].reshape(S5_GROUP, S5_GROUPS, S5_STATE).transpose(1, 2, 0),
           g["s5_c_re"], g["s5_c_im"], g["s5_d"], g["nf"]]
    rep_shapes = [(2, D_MODEL), (2, D_MODEL), (1, 64, 64), (1, 64, 64), (1, 64), (1, 64, 64, 16), (1, 64, 64, 16), (1, 64, 16, 64), (1, 64, 16, 64), (1, D_MODEL), (D_MODEL,)]
    rep_flat = jnp.concatenate([a.reshape(-1) for a in rep])
    rep_blk = jnp.pad(rep_flat, (0, N_DEV * _REP_ROWS * 128 - _REP_SIZE)).reshape(N_DEV, _REP_ROWS, 128)
    dmod = jnp.stack([jnp.concatenate([g["sh1" + t], g["sc1" + t], g["gt1" + t], g["sh2" + t], g["sc2" + t], g["gt2" + t]], axis=1)[0] for t in "ab"])
    msg = jnp.concatenate([
        rep_blk,
        g["wg2"][:GLA_RANK].reshape(GLA_RANK, N_DEV, 64).transpose(1, 0, 2).reshape(N_DEV, 8, 128),
        jnp.pad(g["bg"].reshape(N_DEV, 1, 64), ((0, 0), (0, 0), (0, 64))),
        g["gn"].reshape(N_DEV, 1, 128),
        dmod.reshape(2, N_DEV, ada_w).transpose(1, 0, 2).reshape(N_DEV, 12, 128),
        jnp.zeros((N_DEV, 2, 128), F32),
    ], axis=1)
    got = _exchange(msg, gather=False, name="a2a_small_grads")
    tot = _sum_slots(got, name="sum_small_grads")
    dm = got[:, 282:294].reshape(N_DEV, 2, ada_w).transpose(1, 0, 2)
    g_w_ada = _ada_bwd(c_all, dm)
    back = _exchange(jnp.concatenate([tot[0:_REP_ROWS], tot[282:294], jnp.zeros((4, 128), F32)]), gather=True, name="gather_small_grads")
    rep_sum = back[:, :_REP_ROWS].reshape(-1)[:_REP_SIZE]
    g_b_ada = back[:, _REP_ROWS:_REP_ROWS + 12].reshape(N_DEV, 2, ada_w).transpose(1, 0, 2).reshape(2, 6 * D_MODEL)
    g_rep, off = [], 0
    for s in rep_shapes:
        n = math.prod(s)
        g_rep.append(rep_sum[off:off + n].reshape(s))
        off += n
    g_small = g_rep + [g_b_ada, tot[272:280].reshape(GLA_RANK, 64)[None], tot[280, :64][None], tot[281][None]]
    p_small = [norm_mix, norm_mlp, s5_a_re, s5_a_im, s5_log_dt, s5_b_re, s5_b_im, s5_c_re, s5_c_im, s5_d, norm_final, b_ada, gla_w_gate2, gla_b_gate, gla_g_norm]
    m_small = [m_norm_mix, m_norm_mlp, m_s5_a_re, m_s5_a_im, m_s5_log_dt, m_s5_b_re, m_s5_b_im, m_s5_c_re, m_s5_c_im, m_s5_d, m_norm_final, m_b_ada, m_gla_w_gate2, m_gla_b_gate, m_gla_g_norm]
    v_small = [v_norm_mix, v_norm_mlp, v_s5_a_re, v_s5_a_im, v_s5_log_dt, v_s5_b_re, v_s5_b_im, v_s5_c_re, v_s5_c_im, v_s5_d, v_norm_final, v_b_ada, v_gla_w_gate2, v_gla_b_gate, v_gla_g_norm]
    as2d = lambda a: a.reshape(1, -1) if a.ndim == 1 else a
    small = _adamw_multi([as2d(a) for a in p_small], [as2d(a) for a in m_small], [as2d(a) for a in v_small],
                         [as2d(a) for a in g_small], name="adamw_small")
    small = [[o.reshape(p.shape) for o in outs] for outs, p in zip(small, p_small)]
    ada = _adamw(w_ada.reshape(2 * D_MODEL, ada_w), m_w_ada.reshape(2 * D_MODEL, ada_w), v_w_ada.reshape(2 * D_MODEL, ada_w),
                 [g_w_ada.reshape(2 * D_MODEL, ada_w)], tile=512, name="adamw_ada")
    ada = [a.reshape(w_ada.shape) for a in ada]

    def leaves(k):
        nm, nl, a_re, a_im, ldt, b_re, b_im, c_re, c_im, dsk, nf, bada, wg2_, bg_, gn_ = [s[k] for s in small]
        glu_, gin_, gout_, ff1a_, ff1b_, ff2a_, ff2b_ = [b[k] for b in big]
        return [ada[k], bada, nm, nl, a_re, a_im, ldt, b_re, b_im, c_re, c_im, dsk, glu_[None], gin_[None], wg2_, bg_, gn_, gout_[None],
                jnp.stack([ff1a_, ff1b_]), jnp.stack([ff2a_, ff2b_]), nf]

    return (loss, grad_x[None], *leaves(0), *leaves(1), *leaves(2), *leaves(3))
```

```python
import functools
import math

import jax
import jax.numpy as jnp
from jax import lax
from jax.experimental import pallas as pl
from jax.experimental.pallas import tpu as pltpu

F32 = jnp.float32
BF16 = jnp.bfloat16
MXU_DT = BF16
EPS = 1e-6
N_DEV = 8
VMEM_LIMIT = 56 * 1024 * 1024

D_MODEL = 1024
S5_GROUP = 16
S5_GROUPS = 64
S5_STATE = 64
S5_NSTATE = S5_GROUPS * S5_STATE
S5_GB = 16
S5_NB = S5_GROUPS // S5_GB
S5_BC = S5_GB * S5_GROUP
S5_BS = S5_GB * S5_STATE
GLA_HEADS = 4
GLA_QK = 512
GLA_DK = 128
GLA_DV = 256
GLA_RANK = 16
GLA_RANK_PAD = 128
GLA_TAU = 16.0
GLA_CHUNK = 64
GLA_NB = 4
GLA_IN = 3088
GLA_INP = 2 * GLA_QK + 2 * D_MODEL + GLA_RANK_PAD
D_FF = 4096

ADAM_LR = 0.001
ADAM_B1 = 0.9
ADAM_B2 = 0.999
ADAM_EPS = 1e-08
ADAM_WD = 0.01
ADAM_STEP = 10

_NN = (((1,), (0,)), ((), ()))
_NT = (((1,), (1,)), ((), ()))
_TN = (((0,), (0,)), ((), ()))


def _dot(a, b, dn=_NN):
    return lax.dot_general(a.astype(MXU_DT), b.astype(MXU_DT), dn, preferred_element_type=F32)


def _dot_exact01(x, m01, dn=_NN):
    x1 = x.astype(BF16)
    r1 = x - x1.astype(F32)
    x2 = r1.astype(BF16)
    x3 = (r1 - x2.astype(F32)).astype(BF16)
    m = m01.astype(BF16)
    d = lambda u: lax.dot_general(u, m, dn, preferred_element_type=F32)
    return d(x1) + d(x2) + d(x3)


def _dot_01_left(m01, x, dn=_NN):
    x1 = x.astype(BF16)
    r1 = x - x1.astype(F32)
    x2 = r1.astype(BF16)
    x3 = (r1 - x2.astype(F32)).astype(BF16)
    m = m01.astype(BF16)
    d = lambda u: lax.dot_general(m, u, dn, preferred_element_type=F32)
    return d(x1) + d(x2) + d(x3)


def _sigmoid(x):
    return 1.0 / (1.0 + jnp.exp(-x))


def _silu(x):
    return x * _sigmoid(x)


def _gelu(x):
    return 0.5 * x * (1.0 + jnp.tanh(math.sqrt(2.0 / math.pi) * (x + 0.044715 * (x * x * x))))


def _logsig(x):
    return jnp.minimum(x, 0.0) - jnp.log(1.0 + jnp.exp(-jnp.abs(x)))


def _rms(x):
    return lax.rsqrt(jnp.mean(x * x, axis=-1, keepdims=True) + EPS)


def _params(sem):
    return pltpu.CompilerParams(dimension_semantics=sem, vmem_limit_bytes=VMEM_LIMIT)


def _mm(a, b, dims, out_dtype, *, tm, tn, name, a_fn=None, out_fn=None, extra=None, by_owner=False, comm=None):
    if dims == "tn":
        k, m = a.shape
        n = b.shape[1]
    else:
        m, k = a.shape
        n = b.shape[0] if dims == "nt" else b.shape[1]
    tm, tn = min(tm, m), min(tn, n)
    assert m % tm == 0 and n % tn == 0, (name, m, n, tm, tn)
    dn = {"nn": _NN, "nt": _NT, "tn": _TN}[dims]
    n_in = 2 if extra is None else 3
    nci, nco = (len(comm.ins), len(comm.out_shapes)) if comm is not None else (0, 0)

    def body(*refs):
        a_ref, b_ref = refs[0], refs[1]
        o_ref = refs[n_in + nci]
        if comm is not None:
            run_comm = functools.partial(
                comm.run, pl.program_id(0) * (n // tn) + pl.program_id(1), (m // tm) * (n // tn), refs[n_in:n_in + nci],
                refs[n_in + nci + 1:n_in + nci + 1 + nco], refs[n_in + nci + 1 + nco:])
            run_comm(("first", "mid"))
        av = a_ref[...]
        if a_fn is not None:
            av = a_fn(av.astype(F32))
        acc = _dot(av, b_ref[...], dn)
        if extra is not None:
            acc = out_fn(acc, refs[2][...].astype(F32))
        elif out_fn is not None:
            acc = out_fn(acc)
        if by_owner:
            o_ref[0] = acc.astype(o_ref.dtype)
        else:
            o_ref[...] = acc.astype(o_ref.dtype)
        if comm is not None:
            run_comm(("last",))

    a_spec = pl.BlockSpec((k, tm), lambda i, j: (0, i)) if dims == "tn" else pl.BlockSpec((tm, k), lambda i, j: (i, 0))
    b_spec = pl.BlockSpec((tn, k), lambda i, j: (j, 0)) if dims == "nt" else pl.BlockSpec((k, tn), lambda i, j: (0, j))
    if by_owner:
        o_spec = pl.BlockSpec((1, tm, tn), lambda i, j: (j, i, 0))
        out_shape = jax.ShapeDtypeStruct((n // tn, m, tn), out_dtype)
    else:
        o_spec = pl.BlockSpec((tm, tn), lambda i, j: (i, j))
        out_shape = jax.ShapeDtypeStruct((m, n), out_dtype)
    in_specs, args = [a_spec, b_spec], [a, b]
    if extra is not None:
        in_specs.append(o_spec)
        args.append(extra)
    if comm is None:
        return pl.pallas_call(
            body, grid=(m // tm, n // tn), in_specs=in_specs, out_specs=o_spec, out_shape=out_shape, name=name,
            compiler_params=_params(("parallel", "parallel")),
        )(*args)
    return pl.pallas_call(
        body, grid=(m // tm, n // tn), in_specs=in_specs + [_ANY_SPEC] * nci, out_specs=[o_spec] + [_ANY_SPEC] * nco,
        out_shape=[out_shape] + comm.out_shapes, scratch_shapes=comm.scratch, name=name,
        compiler_params=_params(("arbitrary", "arbitrary")),
    )(*args, *comm.ins)


def _rowcall(f, rows, pars, outs, accs, *, tile, name):
    length = rows[0][0].shape[0]
    tile = min(tile, length)
    nr, npar, no = len(rows), len(pars), len(outs)

    def body(*refs):
        vals = [r[...].astype(F32) for r in refs[:nr + npar]]
        res = f(*vals)
        o_refs = refs[nr + npar:nr + npar + no]
        a_refs = refs[nr + npar + no:]
        for o, v in zip(o_refs, res[:no]):
            o[...] = v.astype(o.dtype)
        if a_refs:
            @pl.when(pl.program_id(0) == 0)
            def _():
                for a in a_refs:
                    a[...] = jnp.zeros(a.shape, F32)
            for a, v in zip(a_refs, res[no:]):
                a[...] += jnp.broadcast_to(v, a.shape)

    in_specs = [pl.BlockSpec((tile, w), lambda i, cb=cb: (i, cb)) for (_, w, cb) in rows]
    in_specs += [pl.BlockSpec(p.shape, lambda i: (0, 0)) for p in pars]
    out_specs = [pl.BlockSpec((tile, w), lambda i: (i, 0)) for (w, _) in outs]
    out_specs += [pl.BlockSpec(s, lambda i: (0, 0)) for s in accs]
    out_shape = [jax.ShapeDtypeStruct((length, w), dt) for (w, dt) in outs]
    out_shape += [jax.ShapeDtypeStruct(s, F32) for s in accs]
    return pl.pallas_call(
        body, grid=(length // tile,), in_specs=in_specs, out_specs=out_specs, out_shape=out_shape, name=name,
        compiler_params=_params(("arbitrary",)),
    )(*[r[0] for r in rows], *pars)


def _vjp_of(f, n_row, n_cot, row_want):
    def g(*a):
        prow, cots, par = a[:n_row], a[n_row:n_row + n_cot], a[n_row + n_cot:]
        _, vjp = jax.vjp(f, *prow, *par)
        grads = vjp(tuple(cots))
        return tuple(grads[i] for i in row_want) + tuple(grads[n_row:])
    return g


def _f_pn(x, g, sc, sh):
    return (x, x * _rms(x) * g * (1.0 + sc) + sh)


def _f_res_pn(x, y, gt, g, sc, sh):
    xn = x + gt * y
    return (xn, xn * _rms(xn) * g * (1.0 + sc) + sh)


def _f_glu_res_pn(x, val, gate, gt, g, sc, sh):
    xn = x + gt * (val * _sigmoid(gate))
    return (xn, xn * _rms(xn) * g * (1.0 + sc) + sh)


def _f_final(x, y, tgt, gt, g):
    xn = x + gt * y
    err = xn * _rms(xn) * g - tgt
    return 0.5 * jnp.mean(err * err, axis=-1, keepdims=True)


def _g_final(x, y, tgt, gt, g):
    lrow, vjp = jax.vjp(_f_final, x, y, tgt, gt, g)
    dx, dy, _, dgt, dg = vjp(jnp.ones_like(lrow))
    return dx, dy, dgt, dg, jnp.sum(lrow)


def _full(a):
    return (a, a.shape[1], 0)


def _s5_prep_f(a_re, a_im, log_dt, bt_re, bt_im, e01):
    dt = jnp.exp(_dot_exact01(log_dt, e01))
    mag = jnp.exp(a_re * dt)
    ph = a_im * dt
    lb_re = mag * jnp.cos(ph)
    lb_im = mag * jnp.sin(ph)
    den = a_re * a_re + a_im * a_im
    nr = lb_re - 1.0
    ni = lb_im
    f_re = (nr * a_re + ni * a_im) / den
    f_im = (ni * a_re - nr * a_im) / den
    bb_re = f_re * bt_re - f_im * bt_im
    bb_im = f_re * bt_im + f_im * bt_re
    return lb_re, lb_im, bb_re, bb_im


def _s5_prep_outs():
    return [jax.ShapeDtypeStruct((1, S5_NSTATE), F32)] * 2 + [jax.ShapeDtypeStruct((S5_GROUP, S5_NSTATE), F32)] * 2


def _s5_prep(a_re, a_im, log_dt, bt_re, bt_im, e01):
    def body(*refs):
        res = _s5_prep_f(*[r[...] for r in refs[:6]])
        for o, v in zip(refs[6:], res):
            o[...] = v
    return pl.pallas_call(body, out_shape=_s5_prep_outs(), name="s5_prep",
                          compiler_params=pltpu.CompilerParams(vmem_limit_bytes=VMEM_LIMIT))(a_re, a_im, log_dt, bt_re, bt_im, e01)


def _s5_prep_bwd(a_re, a_im, log_dt, bt_re, bt_im, e01, d_lb_re, d_lb_im, d_bb_re, d_bb_im):
    def f(a_re, a_im, log_dt, bt_re, bt_im, e01):
        @jax.custom_vjp
        def expand(v):
            return _dot_exact01(v, e01)
        expand.defvjp(lambda v: (_dot_exact01(v, e01), None), lambda _, ct: (_dot_exact01(ct, e01, _NT),))
        dt = jnp.exp(expand(log_dt))
        mag = jnp.exp(a_re * dt)
        ph = a_im * dt
        lb_re = mag * jnp.cos(ph)
        lb_im = mag * jnp.sin(ph)
        den = a_re * a_re + a_im * a_im
        nr = lb_re - 1.0
        f_re = (nr * a_re + lb_im * a_im) / den
        f_im = (lb_im * a_re - nr * a_im) / den
        return lb_re, lb_im, f_re * bt_re - f_im * bt_im, f_re * bt_im + f_im * bt_re

    def body(*refs):
        ins = [r[...] for r in refs[:5]]
        e = refs[5][...]
        cots = tuple(r[...] for r in refs[6:10])
        _, vjp = jax.vjp(lambda *p: f(*p, e), *ins)
        for o, v in zip(refs[10:], vjp(cots)):
            o[...] = v
    outs = [jax.ShapeDtypeStruct(v.shape, F32) for v in (a_re, a_im, log_dt, bt_re, bt_im)]
    return pl.pallas_call(body, out_shape=outs, name="s5_prep_bwd",
                          compiler_params=pltpu.CompilerParams(vmem_limit_bytes=VMEM_LIMIT))(
        a_re, a_im, log_dt, bt_re, bt_im, e01, d_lb_re, d_lb_im, d_bb_re, d_bb_im)


def _s5_scan(x_re, x_im, a_r, a_i, c_r, c_i, n_tiles, reverse):
    sgn = -1.0 if reverse else 1.0

    def tile(k, carry):
        cr, ci = carry
        i = (n_tiles - 1 - k) if reverse else k
        order = range(7, -1, -1) if reverse else range(8)
        for j in order:
            br = x_re[i, pl.ds(j, 1), :]
            bi = x_im[i, pl.ds(j, 1), :]
            nr = a_r * cr - (sgn * a_i) * ci + br
            ni = a_r * ci + (sgn * a_i) * cr + bi
            x_re[i, pl.ds(j, 1), :] = nr
            x_im[i, pl.ds(j, 1), :] = ni
            cr, ci = nr, ni
        return cr, ci

    return lax.fori_loop(0, n_tiles, tile, (c_r, c_i))


def _s5_fwd(u, lam_re, lam_im, bbd_re, bbd_im, cbd_re, cbd_im, d_skip, comm, *, tc):
    length = u.shape[0]
    tc = min(tc, length)
    nt = length // tc
    nci, nco = len(comm.ins), len(comm.out_shapes)

    def body(*refs):
        u_ref, lr_ref, li_ref, br_ref, bi_ref, cr_ref, ci_ref, d_ref = refs[:8]
        z_ref, sr_ref, si_ref = refs[8 + nci:11 + nci]
        xr, xi, car_r, car_i = refs[11 + nci + nco:15 + nci + nco]
        run_comm = functools.partial(comm.run, pl.program_id(0) * nt + pl.program_id(1), S5_NB * nt, refs[8:8 + nci],
                                     refs[11 + nci:11 + nci + nco], refs[15 + nci + nco:])
        run_comm(("first", "mid"))

        @pl.when(pl.program_id(1) == 0)
        def _():
            car_r[...] = jnp.zeros_like(car_r)
            car_i[...] = jnp.zeros_like(car_i)
        sr_ref[0] = car_r[...]
        si_ref[0] = car_i[...]
        uv = u_ref[...]
        xr[...] = _dot(uv, br_ref[0]).reshape(tc // 8, 8, S5_BS)
        xi[...] = _dot(uv, bi_ref[0]).reshape(tc // 8, 8, S5_BS)
        cr, ci = _s5_scan(xr, xi, lr_ref[...], li_ref[...], car_r[...], car_i[...], tc // 8, False)
        car_r[...] = cr
        car_i[...] = ci
        y = (_dot(xr[...].reshape(tc, S5_BS), cr_ref[0]) - _dot(xi[...].reshape(tc, S5_BS), ci_ref[0]) + d_ref[...] * uv)
        z_ref[...] = _gelu(y).astype(z_ref.dtype)
        run_comm(("last",))

    blk_u = pl.BlockSpec((tc, S5_BC), lambda g, t: (t, g))
    blk_l = pl.BlockSpec((1, S5_BS), lambda g, t: (0, g))
    blk_b = pl.BlockSpec((1, S5_BC, S5_BS), lambda g, t: (g, 0, 0))
    blk_c = pl.BlockSpec((1, S5_BS, S5_BC), lambda g, t: (g, 0, 0))
    blk_d = pl.BlockSpec((1, S5_BC), lambda g, t: (0, g))
    blk_s = pl.BlockSpec((1, 1, S5_BS), lambda g, t: (t, 0, g))
    return pl.pallas_call(
        body, grid=(S5_NB, nt),
        in_specs=[blk_u, blk_l, blk_l, blk_b, blk_b, blk_c, blk_c, blk_d] + [_ANY_SPEC] * nci,
        out_specs=[blk_u, blk_s, blk_s] + [_ANY_SPEC] * nco,
        out_shape=[jax.ShapeDtypeStruct((length, D_MODEL), MXU_DT),
                   jax.ShapeDtypeStruct((nt, 1, S5_NSTATE), F32), jax.ShapeDtypeStruct((nt, 1, S5_NSTATE), F32)] + comm.out_shapes,
        scratch_shapes=[pltpu.VMEM((tc // 8, 8, S5_BS), F32), pltpu.VMEM((tc // 8, 8, S5_BS), F32),
                        pltpu.VMEM((1, S5_BS), F32), pltpu.VMEM((1, S5_BS), F32)] + comm.scratch,
        name="s5_fwd", compiler_params=_params(("arbitrary", "arbitrary")),
    )(u, lam_re, lam_im, bbd_re, bbd_im, cbd_re, cbd_im, d_skip, *comm.ins)


def _s5_bwd(u, dz, st_re, st_im, lam_re, lam_im, bbd_re, bbd_im, cbd_re, cbd_im, d_skip, comm, *, tc):
    length = u.shape[0]
    tc = min(tc, length)
    nt = length // tc
    nci, nco = len(comm.ins), len(comm.out_shapes)

    def body(*refs):
        u_ref, dz_ref, sr_ref, si_ref, lr_ref, li_ref, br_ref, bi_ref, cr_ref, ci_ref, d_ref = refs[:11]
        du_ref, dbr_ref, dbi_ref, dcr_ref, dci_ref, dlr_ref, dli_ref, dd_ref = refs[11 + nci:19 + nci]
        xr, xi, gr, gi, car_r, car_i = refs[19 + nci + nco:25 + nci + nco]
        run_comm = functools.partial(comm.run, pl.program_id(0) * nt + pl.program_id(1), S5_NB * nt, refs[11:11 + nci],
                                     refs[19 + nci:19 + nci + nco], refs[25 + nci + nco:])
        run_comm(("first", "mid"))

        @pl.when(pl.program_id(1) == 0)
        def _():
            car_r[...] = jnp.zeros_like(car_r)
            car_i[...] = jnp.zeros_like(car_i)
            for r in (dbr_ref, dbi_ref, dcr_ref, dci_ref, dlr_ref, dli_ref, dd_ref):
                r[...] = jnp.zeros(r.shape, F32)
        a_r, a_i = lr_ref[...], li_ref[...]
        uv = u_ref[...]
        xr[...] = _dot(uv, br_ref[0]).reshape(tc // 8, 8, S5_BS)
        xi[...] = _dot(uv, bi_ref[0]).reshape(tc // 8, 8, S5_BS)
        _s5_scan(xr, xi, a_r, a_i, sr_ref[0], si_ref[0], tc // 8, False)
        xrv = xr[...].reshape(tc, S5_BS)
        xiv = xi[...].reshape(tc, S5_BS)
        y = _dot(xrv, cr_ref[0]) - _dot(xiv, ci_ref[0]) + d_ref[...] * uv
        _, gelu_vjp = jax.vjp(_gelu, y)
        dy = gelu_vjp(dz_ref[...].astype(F32))[0]
        dd_ref[...] += jnp.sum(dy * uv, axis=0, keepdims=True)
        dcr_ref[0] += _dot(xrv, dy, _TN)
        dci_ref[0] -= _dot(xiv, dy, _TN)
        gr[...] = _dot(dy, cr_ref[0], _NT).reshape(tc // 8, 8, S5_BS)
        gi[...] = (-_dot(dy, ci_ref[0], _NT)).reshape(tc // 8, 8, S5_BS)
        cr, ci = _s5_scan(gr, gi, a_r, a_i, car_r[...], car_i[...], tc // 8, True)
        car_r[...] = cr
        car_i[...] = ci
        grv = gr[...].reshape(tc, S5_BS)
        giv = gi[...].reshape(tc, S5_BS)
        first = lax.broadcasted_iota(jnp.int32, (tc, 1), 0) == 0
        xpr = jnp.where(first, sr_ref[0], pltpu.roll(xrv, 1, 0))
        xpi = jnp.where(first, si_ref[0], pltpu.roll(xiv, 1, 0))
        dlr_ref[...] += jnp.sum(grv * xpr + giv * xpi, axis=0, keepdims=True)
        dli_ref[...] += jnp.sum(giv * xpr - grv * xpi, axis=0, keepdims=True)
        dbr_ref[0] += _dot(uv, grv, _TN)
        dbi_ref[0] += _dot(uv, giv, _TN)
        du_ref[...] = _dot(grv, br_ref[0], _NT) + _dot(giv, bi_ref[0], _NT) + d_ref[...] * dy
        run_comm(("last",))

    rev = lambda t: nt - 1 - t
    blk_u = pl.BlockSpec((tc, S5_BC), lambda g, t: (rev(t), g))
    blk_l = pl.BlockSpec((1, S5_BS), lambda g, t: (0, g))
    blk_b = pl.BlockSpec((1, S5_BC, S5_BS), lambda g, t: (g, 0, 0))
    blk_c = pl.BlockSpec((1, S5_BS, S5_BC), lambda g, t: (g, 0, 0))
    blk_d = pl.BlockSpec((1, S5_BC), lambda g, t: (0, g))
    blk_s = pl.BlockSpec((1, 1, S5_BS), lambda g, t: (rev(t), 0, g))
    return pl.pallas_call(
        body, grid=(S5_NB, nt),
        in_specs=[blk_u, blk_u, blk_s, blk_s, blk_l, blk_l, blk_b, blk_b, blk_c, blk_c, blk_d] + [_ANY_SPEC] * nci,
        out_specs=[blk_u, blk_b, blk_b, blk_c, blk_c, blk_l, blk_l, blk_d] + [_ANY_SPEC] * nco,
        out_shape=[jax.ShapeDtypeStruct((length, D_MODEL), F32),
                   jax.ShapeDtypeStruct((S5_NB, S5_BC, S5_BS), F32), jax.ShapeDtypeStruct((S5_NB, S5_BC, S5_BS), F32),
                   jax.ShapeDtypeStruct((S5_NB, S5_BS, S5_BC), F32), jax.ShapeDtypeStruct((S5_NB, S5_BS, S5_BC), F32),
                   jax.ShapeDtypeStruct((1, S5_NSTATE), F32), jax.ShapeDtypeStruct((1, S5_NSTATE), F32),
                   jax.ShapeDtypeStruct((1, D_MODEL), F32)] + comm.out_shapes,
        scratch_shapes=[pltpu.VMEM((tc // 8, 8, S5_BS), F32), pltpu.VMEM((tc // 8, 8, S5_BS), F32),
                        pltpu.VMEM((tc // 8, 8, S5_BS), F32), pltpu.VMEM((tc // 8, 8, S5_BS), F32),
                        pltpu.VMEM((1, S5_BS), F32), pltpu.VMEM((1, S5_BS), F32)] + comm.scratch,
        name="s5_bwd", compiler_params=_params(("arbitrary", "arbitrary")),
    )(u, dz, st_re, st_im, lam_re, lam_im, bbd_re, bbd_im, cbd_re, cbd_im, d_skip, *comm.ins)


def _blockdiag_b(bt):
    eye = jnp.eye(S5_GB, dtype=bt.dtype)
    t = bt.reshape(S5_GROUP, S5_NB, S5_GB, S5_STATE)
    return jnp.einsum("ab,hnbp->nahbp", eye, t).reshape(S5_NB, S5_BC, S5_BS)


def _unblockdiag_b(m):
    eye = jnp.eye(S5_GB, dtype=m.dtype)
    t = m.reshape(S5_NB, S5_GB, S5_GROUP, S5_GB, S5_STATE)
    return jnp.einsum("ab,nahbp->hnbp", eye, t).reshape(S5_GROUP, S5_NSTATE)


def _blockdiag_c(c):
    eye = jnp.eye(S5_GB, dtype=c.dtype)
    t = c.reshape(S5_NB, S5_GB, S5_GROUP, S5_STATE)
    return jnp.einsum("ab,nbhp->napbh", eye, t).reshape(S5_NB, S5_BS, S5_BC)


def _unblockdiag_c(m):
    eye = jnp.eye(S5_GB, dtype=m.dtype)
    t = m.reshape(S5_NB, S5_GB, S5_STATE, S5_GB, S5_GROUP)
    return jnp.einsum("ab,napbh->nbhp", eye, t).reshape(S5_GROUPS, S5_GROUP, S5_STATE)


def _chunk_rows(c):
    return slice(c * GLA_CHUNK, (c + 1) * GLA_CHUNK)


def _per_chunk(rows, nb):
    return jnp.concatenate([jnp.broadcast_to(r, (GLA_CHUNK, r.shape[1])) for r in rows], axis=0)


def _gla_gates(glr, wg2, bg, tri, nb):
    pre = _dot(glr, wg2) + bg
    la = _logsig(pre) * (1.0 / GLA_TAU)
    gc = _dot_01_left(tri, la)
    gend = _per_chunk([gc[(c + 1) * GLA_CHUNK - 1:(c + 1) * GLA_CHUNK, :] for c in range(nb)], nb)
    return pre, jnp.exp(gend - gc), jnp.exp(gend)


def _gla_specs(nblk, nb, rev):
    ix = (lambda n: nblk - 1 - n) if rev else (lambda n: n)
    c = GLA_CHUNK * nb
    return dict(
        q=pl.BlockSpec((c, GLA_QK), lambda n: (ix(n), 0)),
        k=pl.BlockSpec((c, GLA_QK), lambda n: (ix(n), 1)),
        v=pl.BlockSpec((c, D_MODEL), lambda n: (ix(n), 1)),
        r=pl.BlockSpec((c, D_MODEL), lambda n: (ix(n), 2)),
        glr=pl.BlockSpec((c, GLA_RANK_PAD), lambda n: (ix(n), (2 * GLA_QK + 2 * D_MODEL) // GLA_RANK_PAD)),
        wg2=pl.BlockSpec((GLA_RANK_PAD, GLA_QK), lambda n: (0, 0)),
        bg=pl.BlockSpec((1, GLA_QK), lambda n: (0, 0)),
        gn=pl.BlockSpec((1, D_MODEL), lambda n: (0, 0)),
        tri=pl.BlockSpec((c, c), lambda n: (0, 0)),
        row=pl.BlockSpec((c, D_MODEL), lambda n: (ix(n), 0)),
        rowp=pl.BlockSpec((c, GLA_INP), lambda n: (ix(n), 0)),
        st=pl.BlockSpec((nb, GLA_HEADS, GLA_DV, GLA_DK), lambda n: (ix(n), 0, 0, 0)),
    )


def _gla_fwd(proj, wg2, bg, gn, tri, comm):
    length = proj.shape[0]
    nc = length // GLA_CHUNK
    nb = tri.shape[0] // GLA_CHUNK
    nblk = nc // nb
    scale = GLA_DK ** -0.5
    nci, nco = len(comm.ins), len(comm.out_shapes)

    def body(*refs):
        q_ref, k_ref, v_ref, r_ref, glr_ref, wg2_ref, bg_ref, gn_ref, tri_ref = refs[:9]
        og_ref, sp_ref = refs[9 + nci:11 + nci]
        st = refs[11 + nci + nco]
        run_comm = functools.partial(comm.run, pl.program_id(0), nblk, refs[9:9 + nci], refs[11 + nci:11 + nci + nco],
                                     refs[12 + nci + nco:])
        run_comm(("first", "mid"))

        @pl.when(pl.program_id(0) == 0)
        def _():
            st[...] = jnp.zeros_like(st)
        _, e, dec = _gla_gates(glr_ref[...], wg2_ref[...], bg_ref[...], tri_ref[...], nb)
        kd = k_ref[...].astype(F32) * e
        q = q_ref[...].astype(F32) * scale
        for h in range(GLA_HEADS):
            sk = slice(h * GLA_DK, (h + 1) * GLA_DK)
            sv = slice(h * GLA_DV, (h + 1) * GLA_DV)
            state = st[h]
            for c in range(nb):
                rows = _chunk_rows(c)
                sp_ref[c, h] = state
                state = dec[c * GLA_CHUNK:c * GLA_CHUNK + 1, sk] * state + _dot(v_ref[rows, sv], kd[rows, sk], _TN)
                o = _dot(q[rows, sk], state, _NT)
                on = o * _rms(o)
                og_ref[rows, sv] = (on * gn_ref[:, sv] * _silu(r_ref[rows, sv].astype(F32))).astype(og_ref.dtype)
            st[h] = state
        run_comm(("last",))

    s = _gla_specs(nblk, nb, False)
    return pl.pallas_call(
        body, grid=(nblk,),
        in_specs=[s["q"], s["k"], s["v"], s["r"], s["glr"], s["wg2"], s["bg"], s["gn"], s["tri"]] + [_ANY_SPEC] * nci,
        out_specs=[s["row"], s["st"]] + [_ANY_SPEC] * nco,
        out_shape=[jax.ShapeDtypeStruct((length, D_MODEL), MXU_DT),
                   jax.ShapeDtypeStruct((nc, GLA_HEADS, GLA_DV, GLA_DK), F32)] + comm.out_shapes,
        scratch_shapes=[pltpu.VMEM((GLA_HEADS, GLA_DV, GLA_DK), F32)] + comm.scratch,
        name="gla_fwd", compiler_params=_params(("arbitrary",)),
    )(proj, proj, proj, proj, proj, wg2, bg, gn, tri, *comm.ins)


def _gla_bwd(proj, d_og, s_prev, wg2, bg, gn, tri):
    length = proj.shape[0]
    nc = length // GLA_CHUNK
    nb = tri.shape[0] // GLA_CHUNK
    nblk = nc // nb
    scale = GLA_DK ** -0.5

    def body(q_ref, k_ref, v_ref, r_ref, glr_ref, dog_ref, sp_ref, wg2_ref, bg_ref, gn_ref, tri_ref,
             dp_ref, dwg2_ref, dbg_ref, dgn_ref, dst):
        dq_ref = dp_ref.at[:, pl.ds(0, GLA_QK)]
        dk_ref = dp_ref.at[:, pl.ds(GLA_QK, GLA_QK)]
        dv_ref = dp_ref.at[:, pl.ds(2 * GLA_QK, D_MODEL)]
        dr_ref = dp_ref.at[:, pl.ds(2 * GLA_QK + D_MODEL, D_MODEL)]
        dglr_ref = dp_ref.at[:, pl.ds(2 * GLA_QK + 2 * D_MODEL, GLA_RANK_PAD)]

        @pl.when(pl.program_id(0) == 0)
        def _():
            dst[...] = jnp.zeros_like(dst)
            dwg2_ref[...] = jnp.zeros_like(dwg2_ref)
            dbg_ref[...] = jnp.zeros_like(dbg_ref)
            dgn_ref[...] = jnp.zeros_like(dgn_ref)
        glr = glr_ref[...]
        pre, e, dec = _gla_gates(glr, wg2_ref[...], bg_ref[...], tri_ref[...], nb)
        k = k_ref[...].astype(F32)
        kd = k * e
        q = q_ref[...].astype(F32) * scale
        dkd_heads, ddec_heads = [], []
        for h in range(GLA_HEADS):
            sk = slice(h * GLA_DK, (h + 1) * GLA_DK)
            sv = slice(h * GLA_DV, (h + 1) * GLA_DV)
            gnh = gn_ref[:, sv]
            stps, grads_in = [], []
            dgn = jnp.zeros((1, GLA_DV), F32)
            for c in range(nb):
                rows = _chunk_rows(c)
                stp = sp_ref[c, h]
                stn = dec[c * GLA_CHUNK:c * GLA_CHUNK + 1, sk] * stp + _dot(v_ref[rows, sv], kd[rows, sk], _TN)
                o = _dot(q[rows, sk], stn, _NT)
                rinv = _rms(o)
                on = o * rinv
                rv = r_ref[rows, sv].astype(F32)
                sg = _sigmoid(rv)
                dog = dog_ref[rows, sv].astype(F32)
                d_ong = dog * (rv * sg)
                dr_ref[rows, sv] = (dog * (on * gnh) * (sg * (1.0 + rv * (1.0 - sg)))).astype(dr_ref.dtype)
                dgn = dgn + jnp.sum(d_ong * on, axis=0, keepdims=True)
                d_on = d_ong * gnh
                do = rinv * (d_on - on * jnp.mean(d_on * on, axis=-1, keepdims=True))
                dq_ref[rows, sk] = (_dot(do, stn) * scale).astype(dq_ref.dtype)
                stps.append(stp)
                grads_in.append(_dot(do, q[rows, sk], _TN))
            dgn_ref[:, sv] += dgn
            carry = dst[h]
            dkd_rows, ddec_rows = [None] * nb, [None] * nb
            for c in reversed(range(nb)):
                rows = _chunk_rows(c)
                dstn = carry + grads_in[c]
                carry = dec[c * GLA_CHUNK:c * GLA_CHUNK + 1, sk] * dstn
                ddec_rows[c] = jnp.sum(dstn * stps[c], axis=0, keepdims=True)
                dv_ref[rows, sv] = _dot(kd[rows, sk], dstn, _NT).astype(dv_ref.dtype)
                dkd_rows[c] = _dot(v_ref[rows, sv], dstn)
            dst[h] = carry
            dkd_heads.append(jnp.concatenate(dkd_rows, axis=0))
            ddec_heads.append(_per_chunk(ddec_rows, nb))
        dkd = jnp.concatenate(dkd_heads, axis=1)
        ddec = jnp.concatenate(ddec_heads, axis=1)
        dk_ref[...] = (dkd * e).astype(dk_ref.dtype)
        w = dkd * kd
        dgend = _per_chunk([jnp.sum(w[_chunk_rows(c)], axis=0, keepdims=True) for c in range(nb)], nb) + ddec * dec
        dla = dgend - _dot_01_left(tri_ref[...], w, _TN)
        dpre = dla * (1.0 - _sigmoid(pre)) * (1.0 / GLA_TAU)
        dwg2_ref[...] += _dot(glr, dpre, _TN)
        dbg_ref[...] += jnp.sum(dpre, axis=0, keepdims=True)
        dglr_ref[...] = _dot(dpre, wg2_ref[...], _NT).astype(dglr_ref.dtype)

    s = _gla_specs(nblk, nb, True)
    return pl.pallas_call(
        body, grid=(nblk,),
        in_specs=[s["q"], s["k"], s["v"], s["r"], s["glr"], s["row"], s["st"], s["wg2"], s["bg"], s["gn"], s["tri"]],
        out_specs=[s["rowp"], s["wg2"], s["bg"], s["gn"]],
        out_shape=[jax.ShapeDtypeStruct((length, GLA_INP), MXU_DT),
                   jax.ShapeDtypeStruct((GLA_RANK_PAD, GLA_QK), F32), jax.ShapeDtypeStruct((1, GLA_QK), F32),
                   jax.ShapeDtypeStruct((1, D_MODEL), F32)],
        scratch_shapes=[pltpu.VMEM((GLA_HEADS, GLA_DV, GLA_DK), F32)],
        name="gla_bwd", compiler_params=_params(("arbitrary",)),
    )(proj, proj, proj, proj, proj, d_og, s_prev, wg2, bg, gn, tri)


def _local_step(x, tgt, mods, nrm, s5p, w, shards, core, *, row_tile=256, s5_tc=1024):
    length = x.shape[0]
    tmm = 512
    glu_sh, gin_sh, gout_sh, ff1a_sh, ff1b_sh, ff2a_sh, ff2b_sh = shards
    w = dict(w)
    rc = functools.partial(_rowcall, tile=row_tile)
    (sh1a, sc1a, gt1a, sh2a, sc2a, gt2a), (sh1b, sc1b, gt1b, sh2b, sc2b, gt2b) = mods
    vec = (1, D_MODEL)
    row32, row16 = (D_MODEL, F32), (D_MODEL, MXU_DT)

    (h0,) = rc(lambda *a: _f_pn(*a)[1:], [_full(x)], [nrm["mix"][0], sc1a, sh1a], [row32], [], name="pn0")
    lam_re, lam_im, bb_re, bb_im = _s5_prep(s5p["a_re"], s5p["a_im"], s5p["log_dt"], s5p["bt_re"], s5p["bt_im"], s5p["e01"])
    bbd_re = _blockdiag_b(bb_re.reshape(S5_GROUP, S5_GROUPS, S5_STATE)).astype(MXU_DT)
    bbd_im = _blockdiag_b(bb_im.reshape(S5_GROUP, S5_GROUPS, S5_STATE)).astype(MXU_DT)
    cbd_re = _blockdiag_c(s5p["c_re"]).astype(MXU_DT)
    cbd_im = _blockdiag_c(s5p["c_im"]).astype(MXU_DT)
    z0, st_re, st_im, glu_s, ff1a_s, ff2a_s = _s5_fwd(
        h0, lam_re, lam_im, bbd_re, bbd_im, cbd_re, cbd_im, s5p["d"], _ag_comm([glu_sh, ff1a_sh, ff2a_sh]), tc=s5_tc)
    w["glu"] = glu_s.transpose(1, 0, 2).reshape(D_MODEL, 2 * D_MODEL)
    w["ff1"] = [ff1a_s.transpose(1, 0, 2).reshape(D_MODEL, D_FF), None]
    w["ff2"] = [ff2a_s.reshape(D_FF, D_MODEL), None]
    vg = _mm(z0, w["glu"], "nn", F32, tm=tmm, tn=2048, name="glu_mm")
    x1, h1 = rc(_f_glu_res_pn, [_full(x), (vg, D_MODEL, 0), (vg, D_MODEL, 1)], [gt1a, nrm["mlp"][0], sc2a, sh2a],
                [row32, row16], [], name="node1")
    relu = lambda acc: jnp.maximum(acc, 0.0)
    sq = lambda a: a * a
    a0, gin_s = _mm(h1, w["ff1"][0], "nn", MXU_DT, tm=tmm, tn=2048, name="ff1a", out_fn=relu, comm=_ag_comm([gin_sh]))
    f0, ff1b_s = _mm(a0, w["ff2"][0], "nn", F32, tm=tmm, tn=1024, name="ff2a", a_fn=sq, comm=_ag_comm([ff1b_sh]))
    gin_full = gin_s.transpose(1, 0, 2).reshape(D_MODEL, GLA_IN)
    q_, k_, v_, glr_, r_ = jnp.split(gin_full, [GLA_QK, 2 * GLA_QK, 2 * GLA_QK + D_MODEL, 2 * GLA_QK + D_MODEL + GLA_RANK], axis=1)
    w["gin"] = jnp.concatenate([q_, k_, v_, r_, glr_, jnp.zeros((D_MODEL, GLA_RANK_PAD - GLA_RANK), MXU_DT)], axis=1)
    w["ff1"][1] = ff1b_s.transpose(1, 0, 2).reshape(D_MODEL, D_FF)
    x2, h2 = rc(_f_res_pn, [_full(x1), _full(f0)], [gt2a, nrm["mix"][1], sc1b, sh1b], [row32, row16], [], name="node2")
    proj = _mm(h2, w["gin"], "nn", MXU_DT, tm=tmm, tn=GLA_INP, name="gla_in")
    og, s_prev, gout_s, ff2b_s = _gla_fwd(proj, w["wg2"], w["bg"], w["gn"], w["tri"], _ag_comm([gout_sh, ff2b_sh]))
    w["gout"] = gout_s.reshape(D_MODEL, D_MODEL)
    w["ff2"][1] = ff2b_s.reshape(D_FF, D_MODEL)
    y1 = _mm(og, w["gout"], "nn", F32, tm=tmm, tn=1024, name="gla_out")
    x3, h3 = rc(_f_res_pn, [_full(x2), _full(y1)], [gt1b, nrm["mlp"][1], sc2b, sh2b], [row32, row16], [], name="node3")
    a1 = _mm(h3, w["ff1"][1], "nn", MXU_DT, tm=tmm, tn=2048, name="ff1b", out_fn=relu)
    f1 = _mm(a1, w["ff2"][1], "nn", F32, tm=tmm, tn=1024, name="ff2b", a_fn=sq)

    g = {}
    dx, df, g["gt2b"], g["nf"], loss = rc(_g_final, [_full(x3), _full(f1), _full(tgt)], [gt2b, nrm["final"]],
                                          [row32, row16], [vec, vec, (8, 128)], name="final")

    def mlp_bwd(df, a, h, w1, w2, tag):
        dw2 = _mm(a, df, "tn", MXU_DT, tm=1024, tn=1024, name="dff2" + tag, a_fn=sq)
        du = _mm(df, w2, "nt", MXU_DT, tm=tmm, tn=2048, name="dact" + tag, extra=a, out_fn=lambda acc, e: acc * (2.0 * e))
        dw1 = _mm(h, du, "tn", MXU_DT, tm=1024, tn=D_FF // N_DEV, name="dff1" + tag, by_owner=True)
        dh = _mm(du, w1, "nt", F32, tm=tmm, tn=1024, name="dh" + tag)
        return dw1, dw2, dh

    def node_bwd(f, prim_rows, cots, pars, row_want, outs, name):
        nrow = len(prim_rows)
        return rc(_vjp_of(f, nrow, len(cots), row_want), prim_rows + cots, pars, outs, [vec] * len(pars), name=name)

    g["ff1b"], g["ff2b"], dh3 = mlp_bwd(df, a1, h3, w["ff1"][1], w["ff2"][1], "b")
    dx, dy1, g["gt1b"], g["mlp1"], g["sc2b"], g["sh2b"] = node_bwd(
        _f_res_pn, [_full(x2), _full(y1)], [_full(dx), _full(dh3)], [gt1b, nrm["mlp"][1], sc2b, sh2b], (0, 1),
        [row32, row16], "node3_bwd")
    g["gout"] = _mm(og, dy1, "tn", MXU_DT, tm=512, tn=1024, name="dgout")
    d_og = _mm(dy1, w["gout"], "nt", MXU_DT, tm=tmm, tn=1024, name="dog")
    dproj, g["wg2"], g["bg"], g["gn"] = _gla_bwd(proj, d_og, s_prev, w["wg2"], w["bg"], w["gn"], w["tri"])
    g["gin"] = _mm(h2, dproj, "tn", MXU_DT, tm=512, tn=640, name="dgin")
    dh2 = _mm(dproj, w["gin"], "nt", F32, tm=tmm, tn=1024, name="dh2")
    dx, df0, g["gt2a"], g["mix1"], g["sc1b"], g["sh1b"] = node_bwd(
        _f_res_pn, [_full(x1), _full(f0)], [_full(dx), _full(dh2)], [gt2a, nrm["mix"][1], sc1b, sh1b], (0, 1),
        [row32, row16], "node2_bwd")
    g["ff1a"], g["ff2a"], dh1 = mlp_bwd(df0, a0, h1, w["ff1"][0], w["ff2"][0], "a")
    glu_vjp = _vjp_of(_f_glu_res_pn, 3, 2, (0, 1, 2))

    def glu_bwd(*a):
        r = glu_vjp(*a)
        return (r[0], jnp.concatenate([r[1], r[2]], axis=1)) + r[3:]

    dx, dvg, g["gt1a"], g["mlp0"], g["sc2a"], g["sh2a"] = rc(
        glu_bwd, [_full(x), (vg, D_MODEL, 0), (vg, D_MODEL, 1), _full(dx), _full(dh1)], [gt1a, nrm["mlp"][0], sc2a, sh2a],
        [row32, (2 * D_MODEL, MXU_DT)], [vec] * 4, name="node1_bwd")
    gin_g = g.pop("gin")
    gin_g = jnp.concatenate([gin_g[:, :2 * GLA_QK + D_MODEL], gin_g[:, GLA_INP - GLA_RANK_PAD:GLA_INP - GLA_RANK_PAD + GLA_RANK],
                             gin_g[:, 2 * GLA_QK + D_MODEL:2 * GLA_QK + 2 * D_MODEL]], axis=1)
    per_owner = [gin_g.reshape(D_MODEL, N_DEV, GLA_IN // N_DEV).transpose(1, 0, 2),
                 g.pop("gout").reshape(N_DEV, D_MODEL // N_DEV, D_MODEL), g.pop("ff1a"), g.pop("ff1b"),
                 g.pop("ff2a").reshape(N_DEV, D_FF // N_DEV, D_MODEL), g.pop("ff2b").reshape(N_DEV, D_FF // N_DEV, D_MODEL)]
    res = _mm(z0, dvg, "tn", MXU_DT, tm=1024, tn=2 * D_MODEL // N_DEV, name="dglu", by_owner=True, comm=_sibling_comm(per_owner))
    glu_g, from_sibling = res[0], list(res[1:])
    dz0, glu_sib = _mm(dvg, w["glu"], "nt", MXU_DT, tm=tmm, tn=1024, name="dz0", comm=_sibling_comm([glu_g]))
    per_owner = [glu_g] + per_owner
    chip_sum = _pair_add(per_owner, [glu_sib] + from_sibling, core, name="rs_add")
    res = _s5_bwd(h0, dz0, st_re, st_im, lam_re, lam_im, bbd_re, bbd_im, cbd_re, cbd_im, s5p["d"], _chips_comm(chip_sum), tc=s5_tc)
    du0, dbbd_re, dbbd_im, dcbd_re, dcbd_im, dlam_re, dlam_im, g["s5_d"] = res[:8]
    from_chips = res[8:]
    g["s5_c_re"] = _unblockdiag_c(dcbd_re)
    g["s5_c_im"] = _unblockdiag_c(dcbd_im)
    g["s5_a_re"], g["s5_a_im"], g["s5_log_dt"], g["s5_bt_re"], g["s5_bt_im"] = _s5_prep_bwd(
        s5p["a_re"], s5p["a_im"], s5p["log_dt"], s5p["bt_re"], s5p["bt_im"], s5p["e01"],
        dlam_re, dlam_im, _unblockdiag_b(dbbd_re), _unblockdiag_b(dbbd_im))
    grad_x, g["mix0"], g["sc1a"], g["sh1a"] = node_bwd(
        _f_pn, [_full(x)], [_full(dx), _full(du0)], [nrm["mix"][0], sc1a, sh1a], (0,), [row32], "node0_bwd")
    return loss[0, 0], grad_x, g, chip_sum, from_chips


_MESH = pl.DeviceIdType.MESH
_VMEM_SPEC = pl.BlockSpec(memory_space=pltpu.VMEM)
_ANY_SPEC = pl.BlockSpec(memory_space=pl.ANY)


def _my_place():
    ix, iy, ic = lax.axis_index("x"), lax.axis_index("y"), lax.axis_index("c")
    return ix, iy, ic


def _exchange(x, *, gather, name):
    r = x.shape[-2]

    def body(x_ref, o_ref, ssem, rsem):
        ix, iy, ic = _my_place()
        me = 4 * ix + 2 * iy + ic
        if gather:
            o_ref[me] = x_ref[...]
        else:
            o_ref[me] = x_ref[me]
        copies = []
        for k in range(1, N_DEV):
            tx, ty, tc = ix ^ (k >> 2), iy ^ ((k >> 1) & 1), ic ^ (k & 1)
            src = x_ref if gather else x_ref.at[4 * tx + 2 * ty + tc]
            cp = pltpu.make_async_remote_copy(src_ref=src, dst_ref=o_ref.at[me], send_sem=ssem.at[k - 1],
                                              recv_sem=rsem.at[k - 1], device_id=(tx, ty, tc), device_id_type=_MESH)
            cp.start()
            copies.append(cp)
        for cp in copies:
            cp.wait()

    return pl.pallas_call(
        body, out_shape=jax.ShapeDtypeStruct((N_DEV, r, 128), x.dtype), in_specs=[_VMEM_SPEC], out_specs=_VMEM_SPEC,
        scratch_shapes=[pltpu.SemaphoreType.DMA((N_DEV - 1,)), pltpu.SemaphoreType.DMA((N_DEV - 1,))], name=name,
    )(x)


class _Comm:
    def __init__(self, ins, out_shapes, scratch, phases):
        self.ins, self.out_shapes, self.scratch, self.phases = list(ins), list(out_shapes), list(scratch), phases

    def run(self, step, n_steps, in_refs, out_refs, scratch_refs, only):
        when = {"first": 0, "mid": (7 * n_steps) // 8, "last": n_steps - 1}
        for phase, fn in self.phases:
            if phase in only:
                pl.when(step == when[phase])(functools.partial(fn, in_refs, out_refs, scratch_refs))


def _ag_comm(xs):
    n = len(xs)

    def parts():
        ix, iy, ic = _my_place()
        return ic, (ix, iy, ic), (ix, iy, 1 - ic), [(1 - ix, iy), (ix, 1 - iy), (1 - ix, 1 - iy)]

    def copy(ins, outs, sc, a, k, block, to, from_x=False):
        px, py, pc = block
        slot = outs[a].at[4 * px + 2 * py + pc]
        return pltpu.make_async_remote_copy(
            src_ref=ins[a] if from_x else slot, dst_ref=slot, send_sem=sc[0].at[7 * a + k], recv_sem=sc[1].at[7 * a + k],
            device_id=to, device_id_type=_MESH)

    def local(ins, outs, sc, a, me):
        return pltpu.make_async_copy(ins[a], outs[a].at[4 * me[0] + 2 * me[1] + me[2]], sc[2].at[a])

    def start(ins, outs, sc):
        ic, me, sibling, chips = parts()
        for a in range(n):
            local(ins, outs, sc, a, me).start()
            copy(ins, outs, sc, a, 0, me, sibling, True).start()
            for j, chip in enumerate(chips):
                copy(ins, outs, sc, a, 1 + j, me, (*chip, ic), True).start()

    def forward(ins, outs, sc):
        ic, me, sibling, chips = parts()
        for a in range(n):
            for j, chip in enumerate(chips):
                copy(ins, outs, sc, a, 1 + j, (*chip, ic), me).wait_recv()
                copy(ins, outs, sc, a, 4 + j, (*chip, ic), sibling).start()

    def finish(ins, outs, sc):
        ic, me, sibling, chips = parts()
        for a in range(n):
            copy(ins, outs, sc, a, 0, sibling, me).wait_recv()
            for j, chip in enumerate(chips):
                copy(ins, outs, sc, a, 4 + j, (*chip, 1 - ic), me).wait_recv()
        for a in range(n):
            copy(ins, outs, sc, a, 0, me, sibling, True).wait_send()
            for j, chip in enumerate(chips):
                copy(ins, outs, sc, a, 1 + j, me, (*chip, ic), True).wait_send()
                copy(ins, outs, sc, a, 4 + j, (*chip, ic), sibling).wait_send()
            local(ins, outs, sc, a, me).wait()

    return _Comm(xs, [jax.ShapeDtypeStruct((N_DEV,) + x.shape, x.dtype) for x in xs],
                 [pltpu.SemaphoreType.DMA((7 * n,)), pltpu.SemaphoreType.DMA((7 * n,)), pltpu.SemaphoreType.DMA((n,))],
                 [("first", start), ("mid", forward), ("last", finish)])


def _chips_comm(ps):
    n = len(ps)

    def copies(ins, outs, sc):
        ix, iy, ic = _my_place()
        out = []
        for a in range(n):
            for k in range(1, 4):
                tx, ty = ix ^ (k >> 1), iy ^ (k & 1)
                out.append(pltpu.make_async_remote_copy(
                    src_ref=ins[a].at[2 * tx + ty], dst_ref=outs[a].at[k - 1], send_sem=sc[0].at[3 * a + k - 1],
                    recv_sem=sc[1].at[3 * a + k - 1], device_id=(tx, ty, ic), device_id_type=_MESH))
        return out

    def start(ins, outs, sc):
        for cp in copies(ins, outs, sc):
            cp.start()

    def finish(ins, outs, sc):
        for cp in copies(ins, outs, sc):
            cp.wait()

    return _Comm(ps, [jax.ShapeDtypeStruct((3,) + p.shape[1:], p.dtype) for p in ps],
                 [pltpu.SemaphoreType.DMA((3 * n,)), pltpu.SemaphoreType.DMA((3 * n,))], [("first", start), ("last", finish)])


def _sibling_comm(gs):
    n = len(gs)

    def copies(ins, outs, sc):
        ix, iy, ic = _my_place()
        return [pltpu.make_async_remote_copy(src_ref=ins[a].at[2 * q + 1 - ic], dst_ref=outs[a].at[q], send_sem=sc[0].at[4 * a + q],
                                             recv_sem=sc[1].at[4 * a + q], device_id=(ix, iy, 1 - ic), device_id_type=_MESH)
                for a in range(n) for q in range(4)]

    def start(ins, outs, sc):
        for cp in copies(ins, outs, sc):
            cp.start()

    def finish(ins, outs, sc):
        for cp in copies(ins, outs, sc):
            cp.wait()

    return _Comm(gs, [jax.ShapeDtypeStruct((4,) + g.shape[1:], g.dtype) for g in gs],
                 [pltpu.SemaphoreType.DMA((4 * n,)), pltpu.SemaphoreType.DMA((4 * n,))], [("first", start), ("last", finish)])


def _ada_fwd(c_all, w_ada, b_cols):
    def body(c_ref, w_ref, b_ref, o_ref):
        cs = _silu(c_ref[...])
        for i in range(2):
            o_ref[i] = _dot(cs, w_ref[i]) + b_ref[pl.ds(i, 1), :]
    return pl.pallas_call(body, out_shape=jax.ShapeDtypeStruct((2, N_DEV, w_ada.shape[2]), F32), name="ada_fwd",
                          compiler_params=pltpu.CompilerParams(vmem_limit_bytes=VMEM_LIMIT))(c_all, w_ada, b_cols)


def _ada_bwd(c_all, dm):
    def body(c_ref, d_ref, o_ref):
        cs = _silu(c_ref[...])
        for i in range(2):
            o_ref[i] = _dot(cs, d_ref[i], _TN)
    return pl.pallas_call(body, out_shape=jax.ShapeDtypeStruct((2, D_MODEL, dm.shape[2]), F32), name="ada_bwd",
                          compiler_params=pltpu.CompilerParams(vmem_limit_bytes=VMEM_LIMIT))(c_all, dm)


def _pair_add(gs, recvs, core, *, name):
    n = len(gs)

    def body(core_ref, *refs):
        for a in range(n):
            refs[2 * n + a][...] = (refs[a][...].astype(F32) + refs[n + a][...].astype(F32)).astype(refs[2 * n + a].dtype)

    own = [pl.BlockSpec((1,) + g.shape[1:], lambda q, core_ref: (2 * q + core_ref[0], 0, 0)) for g in gs]
    slab = [pl.BlockSpec((1,) + g.shape[1:], lambda q, core_ref: (q, 0, 0)) for g in gs]
    grid_spec = pltpu.PrefetchScalarGridSpec(num_scalar_prefetch=1, grid=(4,), in_specs=own + slab, out_specs=slab)
    return pl.pallas_call(body, grid_spec=grid_spec, out_shape=[jax.ShapeDtypeStruct((4,) + g.shape[1:], g.dtype) for g in gs],
                          name=name, compiler_params=_params(("parallel",)))(core, *gs, *recvs)


def _sum_slots(x, *, name):
    def body(x_ref, o_ref):
        acc = x_ref[0]
        for s in range(1, N_DEV):
            acc = acc + x_ref[s]
        o_ref[...] = acc
    return pl.pallas_call(body, out_shape=jax.ShapeDtypeStruct(x.shape[1:], F32), name=name)(x)


def _adamw_math(w, m, v, g):
    mn = ADAM_B1 * m + (1.0 - ADAM_B1) * g
    vn = ADAM_B2 * v + (1.0 - ADAM_B2) * (g * g)
    m_hat = mn / (1.0 - ADAM_B1 ** ADAM_STEP)
    v_hat = vn / (1.0 - ADAM_B2 ** ADAM_STEP)
    return -ADAM_LR * (m_hat / (jnp.sqrt(v_hat) + ADAM_EPS) + ADAM_WD * w), mn, vn


def _adamw_multi(ws, ms, vs, gs, *, name):
    n = len(ws)

    def body(*refs):
        for i in range(n):
            g = refs[3 * n + i][...]
            o = refs[4 * n + 4 * i:4 * n + 4 * i + 4]
            o[0][...] = g
            o[1][...], o[2][...], o[3][...] = _adamw_math(refs[i][...], refs[n + i][...], refs[2 * n + i][...], g)

    out_shape = [jax.ShapeDtypeStruct(w.shape, F32) for w in ws for _ in range(4)]
    res = pl.pallas_call(body, out_shape=out_shape, name=name,
                         compiler_params=pltpu.CompilerParams(vmem_limit_bytes=VMEM_LIMIT))(*ws, *ms, *vs, *gs)
    return [res[4 * i:4 * i + 4] for i in range(n)]


def _adamw(w, m, v, gparts, *, tile, name, sel=None):
    r, cdim = w.shape
    ng = len(gparts)
    sel = jnp.zeros((1,), jnp.int32) if sel is None else sel

    def body(*refs):
        w_ref, m_ref, v_ref = refs[1:4]
        g = None
        for p, part in zip(refs[4:4 + ng], gparts):
            pv = (p[0] if isinstance(part, tuple) else p[...]).astype(F32)
            g = pv if g is None else g + pv
        g_ref, d_ref, nm_ref, nv_ref = refs[4 + ng:]
        g_ref[...] = g
        d_ref[...], nm_ref[...], nv_ref[...] = _adamw_math(w_ref[...], m_ref[...], v_ref[...], g)

    spec = pl.BlockSpec((tile, cdim), lambda i, s: (i, 0))

    def part_spec(part):
        if not isinstance(part, tuple):
            return spec
        slab = part[1]
        if slab is None:
            return pl.BlockSpec((1, tile, cdim), lambda i, s: (s[0], i, 0))
        return pl.BlockSpec((1, tile, cdim), lambda i, s: (slab, i, 0))

    grid_spec = pltpu.PrefetchScalarGridSpec(
        num_scalar_prefetch=1, grid=(r // tile,), in_specs=[spec] * 3 + [part_spec(p) for p in gparts], out_specs=[spec] * 4)
    return pl.pallas_call(
        body, grid_spec=grid_spec, out_shape=[jax.ShapeDtypeStruct(w.shape, F32)] * 4, name=name,
        compiler_params=_params(("parallel",)),
    )(sel, w, m, v, *[p[0] if isinstance(p, tuple) else p for p in gparts])


_REP_ROWS = 272
_REP_SIZE = 2 * 1024 * 2 + 4096 * 2 + 64 + 65536 * 4 + 1024 + 1024


def _pad_rows(v, rows):
    return jnp.pad(v.reshape(-1), (0, rows * 128 - v.size)).reshape(rows, 128)


def kernel(x, c, w_ada, b_ada, norm_mix, norm_mlp, s5_a_re, s5_a_im, s5_log_dt, s5_b_re, s5_b_im, s5_c_re, s5_c_im, s5_d, s5_w_glu, gla_w_in, gla_w_gate2, gla_b_gate, gla_g_norm, gla_w_out, w_ff1, w_ff2, norm_final, loss_target, m_w_ada, m_b_ada, m_norm_mix, m_norm_mlp, m_s5_a_re, m_s5_a_im, m_s5_log_dt, m_s5_b_re, m_s5_b_im, m_s5_c_re, m_s5_c_im, m_s5_d, m_s5_w_glu, m_gla_w_in, m_gla_w_gate2, m_gla_b_gate, m_gla_g_norm, m_gla_w_out, m_w_ff1, m_w_ff2, m_norm_final, v_w_ada, v_b_ada, v_norm_mix, v_norm_mlp, v_s5_a_re, v_s5_a_im, v_s5_log_dt, v_s5_b_re, v_s5_b_im, v_s5_c_re, v_s5_c_im, v_s5_d, v_s5_w_glu, v_gla_w_in, v_gla_w_gate2, v_gla_b_gate, v_gla_g_norm, v_gla_w_out, v_w_ff1, v_w_ff2, v_norm_final):
    ix, iy, ic = _my_place()
    me = 4 * ix + 2 * iy + ic
    ada_w = w_ada.shape[2]

    msg = jnp.concatenate([c.reshape(8, 128), gla_w_gate2[0].reshape(8, 128), _pad_rows(gla_b_gate, 1),
                           gla_g_norm.reshape(1, 128), jnp.zeros((6, 128), F32)])
    got = _exchange(msg, gather=True, name="gather_small")
    c_all = got[:, 0:8].reshape(N_DEV, D_MODEL)
    wg2 = got[:, 8:16].reshape(N_DEV, GLA_RANK, 64).transpose(1, 0, 2).reshape(GLA_RANK, GLA_QK)
    bg = got[:, 16, :64].reshape(1, GLA_QK)
    gn = got[:, 17, :].reshape(1, D_MODEL)

    b_cols = lax.dynamic_slice_in_dim(b_ada, me * ada_w, ada_w, axis=1)
    mod_cols = _ada_fwd(c_all, w_ada, b_cols)
    pay = jnp.pad(mod_cols.transpose(1, 0, 2).reshape(N_DEV, 12, 128), ((0, 0), (0, 4), (0, 0)))
    mod = _exchange(pay, gather=False, name="a2a_mod")[:, :12].reshape(N_DEV, 2, ada_w).transpose(1, 0, 2).reshape(2, 6 * D_MODEL)
    mods = [[mod[i:i + 1, j * D_MODEL:(j + 1) * D_MODEL] for j in range(6)] for i in range(2)]

    big_w = [s5_w_glu[0], gla_w_in[0], gla_w_out[0], w_ff1[0], w_ff1[1], w_ff2[0], w_ff2[1]]
    big_m = [m_s5_w_glu[0], m_gla_w_in[0], m_gla_w_out[0], m_w_ff1[0], m_w_ff1[1], m_w_ff2[0], m_w_ff2[1]]
    big_v = [v_s5_w_glu[0], v_gla_w_in[0], v_gla_w_out[0], v_w_ff1[0], v_w_ff1[1], v_w_ff2[0], v_w_ff2[1]]
    rows_i = lax.broadcasted_iota(jnp.int32, (GLA_NB * GLA_CHUNK, GLA_NB * GLA_CHUNK), 0)
    cols_i = lax.broadcasted_iota(jnp.int32, (GLA_NB * GLA_CHUNK, GLA_NB * GLA_CHUNK), 1)
    tri = ((rows_i // GLA_CHUNK == cols_i // GLA_CHUNK) & (cols_i <= rows_i)).astype(F32)
    w = dict(wg2=jnp.pad(wg2, ((0, GLA_RANK_PAD - GLA_RANK), (0, 0))), bg=bg, gn=gn, tri=tri)
    core = ic.reshape(1).astype(jnp.int32)
    chip = (2 * ix + iy).reshape(1).astype(jnp.int32)
    nrm = dict(mix=[norm_mix[i:i + 1] for i in range(2)], mlp=[norm_mlp[i:i + 1] for i in range(2)], final=norm_final.reshape(1, D_MODEL))
    e01 = (lax.broadcasted_iota(jnp.int32, (S5_GROUPS, S5_NSTATE), 1) // S5_STATE
           == lax.broadcasted_iota(jnp.int32, (S5_GROUPS, S5_NSTATE), 0)).astype(F32)
    s5p = dict(a_re=s5_a_re.reshape(1, S5_NSTATE), a_im=s5_a_im.reshape(1, S5_NSTATE), log_dt=s5_log_dt,
               bt_re=s5_b_re[0].transpose(2, 0, 1).reshape(S5_GROUP, S5_NSTATE),
               bt_im=s5_b_im[0].transpose(2, 0, 1).reshape(S5_GROUP, S5_NSTATE),
               c_re=s5_c_re[0], c_im=s5_c_im[0], d=s5_d, e01=e01)

    loss_local, grad_x, g, chip_sum, from_chips = _local_step(
        x[0], loss_target[0], mods, nrm, s5p, w, [a.astype(MXU_DT) for a in big_w], core)
    loss = lax.psum(loss_local, ("x", "y", "c"))
    big = []
    for i in range(len(big_w)):
        parts = [(chip_sum[i], None), (from_chips[i], 0), (from_chips[i], 1), (from_chips[i], 2)]
        big.append(_adamw(big_w[i], big_m[i], big_v[i], parts, tile=min(512, big_w[i].shape[0]), name="adamw_big%d" % i, sel=chip))

    rep = [jnp.concatenate([g["mix0"], g["mix1"]]), jnp.concatenate([g["mlp0"], g["mlp1"]]), g["s5_a_re"], g["s5_a_im"], g["s5_log_dt"],
           g["s5_bt_re"].reshape(S5_GROUP, S5_GROUPS, S5_STATE).transpose(1, 2, 0), g["s5_bt_im"].reshape(S5_GROUP, S5_GROUPS, S5_STATE).transpose(1, 2, 0),
           g["s5_c_re"], g["s5_c_im"], g["s5_d"], g["nf"]]
    rep_shapes = [(2, D_MODEL), (2, D_MODEL), (1, 64, 64), (1, 64, 64), (1, 64), (1, 64, 64, 16), (1, 64, 64, 16), (1, 64, 16, 64), (1, 64, 16, 64), (1, D_MODEL), (D_MODEL,)]
    rep_flat = jnp.concatenate([a.reshape(-1) for a in rep])
    rep_blk = jnp.pad(rep_flat, (0, N_DEV * _REP_ROWS * 128 - _REP_SIZE)).reshape(N_DEV, _REP_ROWS, 128)
    dmod = jnp.stack([jnp.concatenate([g["sh1" + t], g["sc1" + t], g["gt1" + t], g["sh2" + t], g["sc2" + t], g["gt2" + t]], axis=1)[0] for t in "ab"])
    msg = jnp.concatenate([
        rep_blk,
        g["wg2"][:GLA_RANK].reshape(GLA_RANK, N_DEV, 64).transpose(1, 0, 2).reshape(N_DEV, 8, 128),
        jnp.pad(g["bg"].reshape(N_DEV, 1, 64), ((0, 0), (0, 0), (0, 64))),
        g["gn"].reshape(N_DEV, 1, 128),
        dmod.reshape(2, N_DEV, ada_w).transpose(1, 0, 2).reshape(N_DEV, 12, 128),
        jnp.zeros((N_DEV, 2, 128), F32),
    ], axis=1)
    got = _exchange(msg, gather=False, name="a2a_small_grads")
    tot = _sum_slots(got, name="sum_small_grads")
    dm = got[:, 282:294].reshape(N_DEV, 2, ada_w).transpose(1, 0, 2)
    g_w_ada = _ada_bwd(c_all, dm)
    back = _exchange(jnp.concatenate([tot[0:_REP_ROWS], tot[282:294], jnp.zeros((4, 128), F32)]), gather=True, name="gather_small_grads")
    rep_sum = back[:, :_REP_ROWS].reshape(-1)[:_REP_SIZE]
    g_b_ada = back[:, _REP_ROWS:_REP_ROWS + 12].reshape(N_DEV, 2, ada_w).transpose(1, 0, 2).reshape(2, 6 * D_MODEL)
    g_rep, off = [], 0
    for s in rep_shapes:
        n = math.prod(s)
        g_rep.append(rep_sum[off:off + n].reshape(s))
        off += n
    g_small = g_rep + [g_b_ada, tot[272:280].reshape(GLA_RANK, 64)[None], tot[280, :64][None], tot[281][None]]
    p_small = [norm_mix, norm_mlp, s5_a_re, s5_a_im, s5_log_dt, s5_b_re, s5_b_im, s5_c_re, s5_c_im, s5_d, norm_final, b_ada, gla_w_gate2, gla_b_gate, gla_g_norm]
    m_small = [m_norm_mix, m_norm_mlp, m_s5_a_re, m_s5_a_im, m_s5_log_dt, m_s5_b_re, m_s5_b_im, m_s5_c_re, m_s5_c_im, m_s5_d, m_norm_final, m_b_ada, m_gla_w_gate2, m_gla_b_gate, m_gla_g_norm]
    v_small = [v_norm_mix, v_norm_mlp, v_s5_a_re, v_s5_a_im, v_s5_log_dt, v_s5_b_re, v_s5_b_im, v_s5_c_re, v_s5_c_im, v_s5_d, v_norm_final, v_b_ada, v_gla_w_gate2, v_gla_b_gate, v_gla_g_norm]
    as2d = lambda a: a.reshape(1, -1) if a.ndim == 1 else a
    small = _adamw_multi([as2d(a) for a in p_small], [as2d(a) for a in m_small], [as2d(a) for a in v_small],
                         [as2d(a) for a in g_small], name="adamw_small")
    small = [[o.reshape(p.shape) for o in outs] for outs, p in zip(small, p_small)]
    ada = _adamw(w_ada.reshape(2 * D_MODEL, ada_w), m_w_ada.reshape(2 * D_MODEL, ada_w), v_w_ada.reshape(2 * D_MODEL, ada_w),
                 [g_w_ada.reshape(2 * D_MODEL, ada_w)], tile=512, name="adamw_ada")
    ada = [a.reshape(w_ada.shape) for a in ada]

    def leaves(k):
        nm, nl, a_re, a_im, ldt, b_re, b_im, c_re, c_im, dsk, nf, bada, wg2_, bg_, gn_ = [s[k] for s in small]
        glu_, gin_, gout_, ff1a_, ff1b_, ff2a_, ff2b_ = [b[k] for b in big]
        return [ada[k], bada, nm, nl, a_re, a_im, ldt, b_re, b_im, c_re, c_im, dsk, glu_[None], gin_[None], wg2_, bg_, gn_, gout_[None],
                jnp.stack([ff1a_, ff1b_]), jnp.stack([ff2a_, ff2b_]), nf]

    return (loss, grad_x[None], *leaves(0), *leaves(1), *leaves(2), *leaves(3))
```

```python
import functools
import math

import jax
import jax.numpy as jnp
from jax import lax
from jax.experimental import pallas as pl
from jax.experimental.pallas import tpu as pltpu

F32 = jnp.float32
BF16 = jnp.bfloat16
MXU_DT = BF16
EPS = 1e-6
N_DEV = 8
VMEM_LIMIT = 56 * 1024 * 1024

D_MODEL = 1024
S5_GROUP = 16
S5_GROUPS = 64
S5_STATE = 64
S5_NSTATE = S5_GROUPS * S5_STATE
S5_GB = 16
S5_NB = S5_GROUPS // S5_GB
S5_BC = S5_GB * S5_GROUP
S5_BS = S5_GB * S5_STATE
GLA_HEADS = 4
GLA_QK = 512
GLA_DK = 128
GLA_DV = 256
GLA_RANK = 16
GLA_RANK_PAD = 128
GLA_TAU = 16.0
GLA_CHUNK = 64
GLA_NB = 8
GLA_IN = 3088
GLA_INP = 2 * GLA_QK + 2 * D_MODEL + GLA_RANK_PAD
D_FF = 4096

ADAM_LR = 0.001
ADAM_B1 = 0.9
ADAM_B2 = 0.999
ADAM_EPS = 1e-08
ADAM_WD = 0.01
ADAM_STEP = 10

_NN = (((1,), (0,)), ((), ()))
_NT = (((1,), (1,)), ((), ()))
_TN = (((0,), (0,)), ((), ()))


def _dot(a, b, dn=_NN):
    return lax.dot_general(a.astype(MXU_DT), b.astype(MXU_DT), dn, preferred_element_type=F32)


def _dot_exact01(x, m01, dn=_NN):
    x1 = x.astype(BF16)
    r1 = x - x1.astype(F32)
    x2 = r1.astype(BF16)
    x3 = (r1 - x2.astype(F32)).astype(BF16)
    m = m01.astype(BF16)
    d = lambda u: lax.dot_general(u, m, dn, preferred_element_type=F32)
    return d(x1) + d(x2) + d(x3)


def _dot_01_left(m01, x, dn=_NN):
    x1 = x.astype(BF16)
    r1 = x - x1.astype(F32)
    x2 = r1.astype(BF16)
    x3 = (r1 - x2.astype(F32)).astype(BF16)
    m = m01.astype(BF16)
    d = lambda u: lax.dot_general(m, u, dn, preferred_element_type=F32)
    return d(x1) + d(x2) + d(x3)


def _sigmoid(x):
    return 1.0 / (1.0 + jnp.exp(-x))


def _silu(x):
    return x * _sigmoid(x)


def _gelu(x):
    return 0.5 * x * (1.0 + jnp.tanh(math.sqrt(2.0 / math.pi) * (x + 0.044715 * (x * x * x))))


def _logsig(x):
    return jnp.minimum(x, 0.0) - jnp.log(1.0 + jnp.exp(-jnp.abs(x)))


def _rms(x):
    return lax.rsqrt(jnp.mean(x * x, axis=-1, keepdims=True) + EPS)


def _params(sem):
    return pltpu.CompilerParams(dimension_semantics=sem, vmem_limit_bytes=VMEM_LIMIT)


def _mm(a, b, dims, out_dtype, *, tm, tn, name, a_fn=None, out_fn=None, extra=None, by_owner=False, comm=None):
    if dims == "tn":
        k, m = a.shape
        n = b.shape[1]
    else:
        m, k = a.shape
        n = b.shape[0] if dims == "nt" else b.shape[1]
    tm, tn = min(tm, m), min(tn, n)
    assert m % tm == 0 and n % tn == 0, (name, m, n, tm, tn)
    dn = {"nn": _NN, "nt": _NT, "tn": _TN}[dims]
    n_in = 2 if extra is None else 3
    nci, nco = (len(comm.ins), len(comm.out_shapes)) if comm is not None else (0, 0)

    def body(*refs):
        a_ref, b_ref = refs[0], refs[1]
        o_ref = refs[n_in + nci]
        if comm is not None:
            run_comm = functools.partial(
                comm.run, pl.program_id(0) * (n // tn) + pl.program_id(1), (m // tm) * (n // tn), refs[n_in:n_in + nci],
                refs[n_in + nci + 1:n_in + nci + 1 + nco], refs[n_in + nci + 1 + nco:])
            run_comm(("first", "mid"))
        av = a_ref[...]
        if a_fn is not None:
            av = a_fn(av.astype(F32))
        acc = _dot(av, b_ref[...], dn)
        if extra is not None:
            acc = out_fn(acc, refs[2][...].astype(F32))
        elif out_fn is not None:
            acc = out_fn(acc)
        if by_owner:
            o_ref[0] = acc.astype(o_ref.dtype)
        else:
            o_ref[...] = acc.astype(o_ref.dtype)
        if comm is not None:
            run_comm(("last",))

    a_spec = pl.BlockSpec((k, tm), lambda i, j: (0, i)) if dims == "tn" else pl.BlockSpec((tm, k), lambda i, j: (i, 0))
    b_spec = pl.BlockSpec((tn, k), lambda i, j: (j, 0)) if dims == "nt" else pl.BlockSpec((k, tn), lambda i, j: (0, j))
    if by_owner:
        o_spec = pl.BlockSpec((1, tm, tn), lambda i, j: (j, i, 0))
        out_shape = jax.ShapeDtypeStruct((n // tn, m, tn), out_dtype)
    else:
        o_spec = pl.BlockSpec((tm, tn), lambda i, j: (i, j))
        out_shape = jax.ShapeDtypeStruct((m, n), out_dtype)
    in_specs, args = [a_spec, b_spec], [a, b]
    if extra is not None:
        in_specs.append(o_spec)
        args.append(extra)
    if comm is None:
        return pl.pallas_call(
            body, grid=(m // tm, n // tn), in_specs=in_specs, out_specs=o_spec, out_shape=out_shape, name=name,
            compiler_params=_params(("parallel", "parallel")),
        )(*args)
    return pl.pallas_call(
        body, grid=(m // tm, n // tn), in_specs=in_specs + [_ANY_SPEC] * nci, out_specs=[o_spec] + [_ANY_SPEC] * nco,
        out_shape=[out_shape] + comm.out_shapes, scratch_shapes=comm.scratch, name=name,
        compiler_params=_params(("arbitrary", "arbitrary")),
    )(*args, *comm.ins)


def _rowcall(f, rows, pars, outs, accs, *, tile, name):
    length = rows[0][0].shape[0]
    tile = min(tile, length)
    nr, npar, no = len(rows), len(pars), len(outs)

    def body(*refs):
        vals = [r[...].astype(F32) for r in refs[:nr + npar]]
        res = f(*vals)
        o_refs = refs[nr + npar:nr + npar + no]
        a_refs = refs[nr + npar + no:]
        for o, v in zip(o_refs, res[:no]):
            o[...] = v.astype(o.dtype)
        if a_refs:
            @pl.when(pl.program_id(0) == 0)
            def _():
                for a in a_refs:
                    a[...] = jnp.zeros(a.shape, F32)
            for a, v in zip(a_refs, res[no:]):
                a[...] += jnp.broadcast_to(v, a.shape)

    in_specs = [pl.BlockSpec((tile, w), lambda i, cb=cb: (i, cb)) for (_, w, cb) in rows]
    in_specs += [pl.BlockSpec(p.shape, lambda i: (0, 0)) for p in pars]
    out_specs = [pl.BlockSpec((tile, w), lambda i: (i, 0)) for (w, _) in outs]
    out_specs += [pl.BlockSpec(s, lambda i: (0, 0)) for s in accs]
    out_shape = [jax.ShapeDtypeStruct((length, w), dt) for (w, dt) in outs]
    out_shape += [jax.ShapeDtypeStruct(s, F32) for s in accs]
    return pl.pallas_call(
        body, grid=(length // tile,), in_specs=in_specs, out_specs=out_specs, out_shape=out_shape, name=name,
        compiler_params=_params(("arbitrary",)),
    )(*[r[0] for r in rows], *pars)


def _vjp_of(f, n_row, n_cot, row_want):
    def g(*a):
        prow, cots, par = a[:n_row], a[n_row:n_row + n_cot], a[n_row + n_cot:]
        _, vjp = jax.vjp(f, *prow, *par)
        grads = vjp(tuple(cots))
        return tuple(grads[i] for i in row_want) + tuple(grads[n_row:])
    return g


def _f_pn(x, g, sc, sh):
    return (x, x * _rms(x) * g * (1.0 + sc) + sh)


def _f_res_pn(x, y, gt, g, sc, sh):
    xn = x + gt * y
    return (xn, xn * _rms(xn) * g * (1.0 + sc) + sh)


def _f_glu_res_pn(x, val, gate, gt, g, sc, sh):
    xn = x + gt * (val * _sigmoid(gate))
    return (xn, xn * _rms(xn) * g * (1.0 + sc) + sh)


def _f_final(x, y, tgt, gt, g):
    xn = x + gt * y
    err = xn * _rms(xn) * g - tgt
    return 0.5 * jnp.mean(err * err, axis=-1, keepdims=True)


def _g_final(x, y, tgt, gt, g):
    lrow, vjp = jax.vjp(_f_final, x, y, tgt, gt, g)
    dx, dy, _, dgt, dg = vjp(jnp.ones_like(lrow))
    return dx, dy, dgt, dg, jnp.sum(lrow)


def _full(a):
    return (a, a.shape[1], 0)


def _s5_prep_f(a_re, a_im, log_dt, bt_re, bt_im, e01):
    dt = jnp.exp(_dot_exact01(log_dt, e01))
    mag = jnp.exp(a_re * dt)
    ph = a_im * dt
    lb_re = mag * jnp.cos(ph)
    lb_im = mag * jnp.sin(ph)
    den = a_re * a_re + a_im * a_im
    nr = lb_re - 1.0
    ni = lb_im
    f_re = (nr * a_re + ni * a_im) / den
    f_im = (ni * a_re - nr * a_im) / den
    bb_re = f_re * bt_re - f_im * bt_im
    bb_im = f_re * bt_im + f_im * bt_re
    return lb_re, lb_im, bb_re, bb_im


def _s5_prep_outs():
    return [jax.ShapeDtypeStruct((1, S5_NSTATE), F32)] * 2 + [jax.ShapeDtypeStruct((S5_GROUP, S5_NSTATE), F32)] * 2


def _s5_prep(a_re, a_im, log_dt, bt_re, bt_im, e01):
    def body(*refs):
        res = _s5_prep_f(*[r[...] for r in refs[:6]])
        for o, v in zip(refs[6:], res):
            o[...] = v
    return pl.pallas_call(body, out_shape=_s5_prep_outs(), name="s5_prep",
                          compiler_params=pltpu.CompilerParams(vmem_limit_bytes=VMEM_LIMIT))(a_re, a_im, log_dt, bt_re, bt_im, e01)


def _s5_prep_bwd(a_re, a_im, log_dt, bt_re, bt_im, e01, d_lb_re, d_lb_im, d_bb_re, d_bb_im):
    def f(a_re, a_im, log_dt, bt_re, bt_im, e01):
        @jax.custom_vjp
        def expand(v):
            return _dot_exact01(v, e01)
        expand.defvjp(lambda v: (_dot_exact01(v, e01), None), lambda _, ct: (_dot_exact01(ct, e01, _NT),))
        dt = jnp.exp(expand(log_dt))
        mag = jnp.exp(a_re * dt)
        ph = a_im * dt
        lb_re = mag * jnp.cos(ph)
        lb_im = mag * jnp.sin(ph)
        den = a_re * a_re + a_im * a_im
        nr = lb_re - 1.0
        f_re = (nr * a_re + lb_im * a_im) / den
        f_im = (lb_im * a_re - nr * a_im) / den
        return lb_re, lb_im, f_re * bt_re - f_im * bt_im, f_re * bt_im + f_im * bt_re

    def body(*refs):
        ins = [r[...] for r in refs[:5]]
        e = refs[5][...]
        cots = tuple(r[...] for r in refs[6:10])
        _, vjp = jax.vjp(lambda *p: f(*p, e), *ins)
        for o, v in zip(refs[10:], vjp(cots)):
            o[...] = v
    outs = [jax.ShapeDtypeStruct(v.shape, F32) for v in (a_re, a_im, log_dt, bt_re, bt_im)]
    return pl.pallas_call(body, out_shape=outs, name="s5_prep_bwd",
                          compiler_params=pltpu.CompilerParams(vmem_limit_bytes=VMEM_LIMIT))(
        a_re, a_im, log_dt, bt_re, bt_im, e01, d_lb_re, d_lb_im, d_bb_re, d_bb_im)


def _s5_scan(x_re, x_im, a_r, a_i, c_r, c_i, n_tiles, reverse):
    sgn = -1.0 if reverse else 1.0

    def tile(k, carry):
        cr, ci = carry
        i = (n_tiles - 1 - k) if reverse else k
        order = range(7, -1, -1) if reverse else range(8)
        for j in order:
            br = x_re[i, pl.ds(j, 1), :]
            bi = x_im[i, pl.ds(j, 1), :]
            nr = a_r * cr - (sgn * a_i) * ci + br
            ni = a_r * ci + (sgn * a_i) * cr + bi
            x_re[i, pl.ds(j, 1), :] = nr
            x_im[i, pl.ds(j, 1), :] = ni
            cr, ci = nr, ni
        return cr, ci

    return lax.fori_loop(0, n_tiles, tile, (c_r, c_i))


def _s5_fwd(u, lam_re, lam_im, bbd_re, bbd_im, cbd_re, cbd_im, d_skip, comm, *, tc):
    length = u.shape[0]
    tc = min(tc, length)
    nt = length // tc
    nci, nco = len(comm.ins), len(comm.out_shapes)

    def body(*refs):
        u_ref, lr_ref, li_ref, br_ref, bi_ref, cr_ref, ci_ref, d_ref = refs[:8]
        z_ref, sr_ref, si_ref = refs[8 + nci:11 + nci]
        xr, xi, car_r, car_i = refs[11 + nci + nco:15 + nci + nco]
        run_comm = functools.partial(comm.run, pl.program_id(0) * nt + pl.program_id(1), S5_NB * nt, refs[8:8 + nci],
                                     refs[11 + nci:11 + nci + nco], refs[15 + nci + nco:])
        run_comm(("first", "mid"))

        @pl.when(pl.program_id(1) == 0)
        def _():
            car_r[...] = jnp.zeros_like(car_r)
            car_i[...] = jnp.zeros_like(car_i)
        sr_ref[0] = car_r[...]
        si_ref[0] = car_i[...]
        uv = u_ref[...]
        xr[...] = _dot(uv, br_ref[0]).reshape(tc // 8, 8, S5_BS)
        xi[...] = _dot(uv, bi_ref[0]).reshape(tc // 8, 8, S5_BS)
        cr, ci = _s5_scan(xr, xi, lr_ref[...], li_ref[...], car_r[...], car_i[...], tc // 8, False)
        car_r[...] = cr
        car_i[...] = ci
        y = (_dot(xr[...].reshape(tc, S5_BS), cr_ref[0]) - _dot(xi[...].reshape(tc, S5_BS), ci_ref[0]) + d_ref[...] * uv)
        z_ref[...] = _gelu(y).astype(z_ref.dtype)
        run_comm(("last",))

    blk_u = pl.BlockSpec((tc, S5_BC), lambda g, t: (t, g))
    blk_l = pl.BlockSpec((1, S5_BS), lambda g, t: (0, g))
    blk_b = pl.BlockSpec((1, S5_BC, S5_BS), lambda g, t: (g, 0, 0))
    blk_c = pl.BlockSpec((1, S5_BS, S5_BC), lambda g, t: (g, 0, 0))
    blk_d = pl.BlockSpec((1, S5_BC), lambda g, t: (0, g))
    blk_s = pl.BlockSpec((1, 1, S5_BS), lambda g, t: (t, 0, g))
    return pl.pallas_call(
        body, grid=(S5_NB, nt),
        in_specs=[blk_u, blk_l, blk_l, blk_b, blk_b, blk_c, blk_c, blk_d] + [_ANY_SPEC] * nci,
        out_specs=[blk_u, blk_s, blk_s] + [_ANY_SPEC] * nco,
        out_shape=[jax.ShapeDtypeStruct((length, D_MODEL), MXU_DT),
                   jax.ShapeDtypeStruct((nt, 1, S5_NSTATE), F32), jax.ShapeDtypeStruct((nt, 1, S5_NSTATE), F32)] + comm.out_shapes,
        scratch_shapes=[pltpu.VMEM((tc // 8, 8, S5_BS), F32), pltpu.VMEM((tc // 8, 8, S5_BS), F32),
                        pltpu.VMEM((1, S5_BS), F32), pltpu.VMEM((1, S5_BS), F32)] + comm.scratch,
        name="s5_fwd", compiler_params=_params(("arbitrary", "arbitrary")),
    )(u, lam_re, lam_im, bbd_re, bbd_im, cbd_re, cbd_im, d_skip, *comm.ins)


def _s5_bwd(u, dz, st_re, st_im, lam_re, lam_im, bbd_re, bbd_im, cbd_re, cbd_im, d_skip, comm, *, tc):
    length = u.shape[0]
    tc = min(tc, length)
    nt = length // tc
    nci, nco = len(comm.ins), len(comm.out_shapes)

    def body(*refs):
        u_ref, dz_ref, sr_ref, si_ref, lr_ref, li_ref, br_ref, bi_ref, cr_ref, ci_ref, d_ref = refs[:11]
        du_ref, dbr_ref, dbi_ref, dcr_ref, dci_ref, dlr_ref, dli_ref, dd_ref = refs[11 + nci:19 + nci]
        xr, xi, gr, gi, car_r, car_i = refs[19 + nci + nco:25 + nci + nco]
        run_comm = functools.partial(comm.run, pl.program_id(0) * nt + pl.program_id(1), S5_NB * nt, refs[11:11 + nci],
                                     refs[19 + nci:19 + nci + nco], refs[25 + nci + nco:])
        run_comm(("first", "mid"))

        @pl.when(pl.program_id(1) == 0)
        def _():
            car_r[...] = jnp.zeros_like(car_r)
            car_i[...] = jnp.zeros_like(car_i)
            for r in (dbr_ref, dbi_ref, dcr_ref, dci_ref, dlr_ref, dli_ref, dd_ref):
                r[...] = jnp.zeros(r.shape, F32)
        a_r, a_i = lr_ref[...], li_ref[...]
        uv = u_ref[...]
        xr[...] = _dot(uv, br_ref[0]).reshape(tc // 8, 8, S5_BS)
        xi[...] = _dot(uv, bi_ref[0]).reshape(tc // 8, 8, S5_BS)
        _s5_scan(xr, xi, a_r, a_i, sr_ref[0], si_ref[0], tc // 8, False)
        xrv = xr[...].reshape(tc, S5_BS)
        xiv = xi[...].reshape(tc, S5_BS)
        y = _dot(xrv, cr_ref[0]) - _dot(xiv, ci_ref[0]) + d_ref[...] * uv
        _, gelu_vjp = jax.vjp(_gelu, y)
        dy = gelu_vjp(dz_ref[...].astype(F32))[0]
        dd_ref[...] += jnp.sum(dy * uv, axis=0, keepdims=True)
        dcr_ref[0] += _dot(xrv, dy, _TN)
        dci_ref[0] -= _dot(xiv, dy, _TN)
        gr[...] = _dot(dy, cr_ref[0], _NT).reshape(tc // 8, 8, S5_BS)
        gi[...] = (-_dot(dy, ci_ref[0], _NT)).reshape(tc // 8, 8, S5_BS)
        cr, ci = _s5_scan(gr, gi, a_r, a_i, car_r[...], car_i[...], tc // 8, True)
        car_r[...] = cr
        car_i[...] = ci
        grv = gr[...].reshape(tc, S5_BS)
        giv = gi[...].reshape(tc, S5_BS)
        first = lax.broadcasted_iota(jnp.int32, (tc, 1), 0) == 0
        xpr = jnp.where(first, sr_ref[0], pltpu.roll(xrv, 1, 0))
        xpi = jnp.where(first, si_ref[0], pltpu.roll(xiv, 1, 0))
        dlr_ref[...] += jnp.sum(grv * xpr + giv * xpi, axis=0, keepdims=True)
        dli_ref[...] += jnp.sum(giv * xpr - grv * xpi, axis=0, keepdims=True)
        dbr_ref[0] += _dot(uv, grv, _TN)
        dbi_ref[0] += _dot(uv, giv, _TN)
        du_ref[...] = _dot(grv, br_ref[0], _NT) + _dot(giv, bi_ref[0], _NT) + d_ref[...] * dy
        run_comm(("last",))

    rev = lambda t: nt - 1 - t
    blk_u = pl.BlockSpec((tc, S5_BC), lambda g, t: (rev(t), g))
    blk_l = pl.BlockSpec((1, S5_BS), lambda g, t: (0, g))
    blk_b = pl.BlockSpec((1, S5_BC, S5_BS), lambda g, t: (g, 0, 0))
    blk_c = pl.BlockSpec((1, S5_BS, S5_BC), lambda g, t: (g, 0, 0))
    blk_d = pl.BlockSpec((1, S5_BC), lambda g, t: (0, g))
    blk_s = pl.BlockSpec((1, 1, S5_BS), lambda g, t: (rev(t), 0, g))
    return pl.pallas_call(
        body, grid=(S5_NB, nt),
        in_specs=[blk_u, blk_u, blk_s, blk_s, blk_l, blk_l, blk_b, blk_b, blk_c, blk_c, blk_d] + [_ANY_SPEC] * nci,
        out_specs=[blk_u, blk_b, blk_b, blk_c, blk_c, blk_l, blk_l, blk_d] + [_ANY_SPEC] * nco,
        out_shape=[jax.ShapeDtypeStruct((length, D_MODEL), F32),
                   jax.ShapeDtypeStruct((S5_NB, S5_BC, S5_BS), F32), jax.ShapeDtypeStruct((S5_NB, S5_BC, S5_BS), F32),
                   jax.ShapeDtypeStruct((S5_NB, S5_BS, S5_BC), F32), jax.ShapeDtypeStruct((S5_NB, S5_BS, S5_BC), F32),
                   jax.ShapeDtypeStruct((1, S5_NSTATE), F32), jax.ShapeDtypeStruct((1, S5_NSTATE), F32),
                   jax.ShapeDtypeStruct((1, D_MODEL), F32)] + comm.out_shapes,
        scratch_shapes=[pltpu.VMEM((tc // 8, 8, S5_BS), F32), pltpu.VMEM((tc // 8, 8, S5_BS), F32),
                        pltpu.VMEM((tc // 8, 8, S5_BS), F32), pltpu.VMEM((tc // 8, 8, S5_BS), F32),
                        pltpu.VMEM((1, S5_BS), F32), pltpu.VMEM((1, S5_BS), F32)] + comm.scratch,
        name="s5_bwd", compiler_params=_params(("arbitrary", "arbitrary")),
    )(u, dz, st_re, st_im, lam_re, lam_im, bbd_re, bbd_im, cbd_re, cbd_im, d_skip, *comm.ins)


def _blockdiag_b(bt):
    eye = jnp.eye(S5_GB, dtype=bt.dtype)
    t = bt.reshape(S5_GROUP, S5_NB, S5_GB, S5_STATE)
    return jnp.einsum("ab,hnbp->nahbp", eye, t).reshape(S5_NB, S5_BC, S5_BS)


def _unblockdiag_b(m):
    eye = jnp.eye(S5_GB, dtype=m.dtype)
    t = m.reshape(S5_NB, S5_GB, S5_GROUP, S5_GB, S5_STATE)
    return jnp.einsum("ab,nahbp->hnbp", eye, t).reshape(S5_GROUP, S5_NSTATE)


def _blockdiag_c(c):
    eye = jnp.eye(S5_GB, dtype=c.dtype)
    t = c.reshape(S5_NB, S5_GB, S5_GROUP, S5_STATE)
    return jnp.einsum("ab,nbhp->napbh", eye, t).reshape(S5_NB, S5_BS, S5_BC)


def _unblockdiag_c(m):
    eye = jnp.eye(S5_GB, dtype=m.dtype)
    t = m.reshape(S5_NB, S5_GB, S5_STATE, S5_GB, S5_GROUP)
    return jnp.einsum("ab,napbh->nbhp", eye, t).reshape(S5_GROUPS, S5_GROUP, S5_STATE)


def _chunk_rows(c):
    return slice(c * GLA_CHUNK, (c + 1) * GLA_CHUNK)


def _per_chunk(rows, nb):
    return jnp.concatenate([jnp.broadcast_to(r, (GLA_CHUNK, r.shape[1])) for r in rows], axis=0)


def _gla_gates(glr, wg2, bg, tri, nb):
    pre = _dot(glr, wg2) + bg
    la = _logsig(pre) * (1.0 / GLA_TAU)
    gc = _dot_01_left(tri, la)
    gend = _per_chunk([gc[(c + 1) * GLA_CHUNK - 1:(c + 1) * GLA_CHUNK, :] for c in range(nb)], nb)
    return pre, jnp.exp(gend - gc), jnp.exp(gend)


def _gla_specs(nblk, nb, rev):
    ix = (lambda n: nblk - 1 - n) if rev else (lambda n: n)
    c = GLA_CHUNK * nb
    return dict(
        q=pl.BlockSpec((c, GLA_QK), lambda n: (ix(n), 0)),
        k=pl.BlockSpec((c, GLA_QK), lambda n: (ix(n), 1)),
        v=pl.BlockSpec((c, D_MODEL), lambda n: (ix(n), 1)),
        r=pl.BlockSpec((c, D_MODEL), lambda n: (ix(n), 2)),
        glr=pl.BlockSpec((c, GLA_RANK_PAD), lambda n: (ix(n), (2 * GLA_QK + 2 * D_MODEL) // GLA_RANK_PAD)),
        wg2=pl.BlockSpec((GLA_RANK_PAD, GLA_QK), lambda n: (0, 0)),
        bg=pl.BlockSpec((1, GLA_QK), lambda n: (0, 0)),
        gn=pl.BlockSpec((1, D_MODEL), lambda n: (0, 0)),
        tri=pl.BlockSpec((c, c), lambda n: (0, 0)),
        row=pl.BlockSpec((c, D_MODEL), lambda n: (ix(n), 0)),
        rowp=pl.BlockSpec((c, GLA_INP), lambda n: (ix(n), 0)),
        st=pl.BlockSpec((nb, GLA_HEADS, GLA_DV, GLA_DK), lambda n: (ix(n), 0, 0, 0)),
    )


def _gla_fwd(proj, wg2, bg, gn, tri, comm):
    length = proj.shape[0]
    nc = length // GLA_CHUNK
    nb = tri.shape[0] // GLA_CHUNK
    nblk = nc // nb
    scale = GLA_DK ** -0.5
    nci, nco = len(comm.ins), len(comm.out_shapes)

    def body(*refs):
        q_ref, k_ref, v_ref, r_ref, glr_ref, wg2_ref, bg_ref, gn_ref, tri_ref = refs[:9]
        og_ref, sp_ref = refs[9 + nci:11 + nci]
        st = refs[11 + nci + nco]
        run_comm = functools.partial(comm.run, pl.program_id(0), nblk, refs[9:9 + nci], refs[11 + nci:11 + nci + nco],
                                     refs[12 + nci + nco:])
        run_comm(("first", "mid"))

        @pl.when(pl.program_id(0) == 0)
        def _():
            st[...] = jnp.zeros_like(st)
        _, e, dec = _gla_gates(glr_ref[...], wg2_ref[...], bg_ref[...], tri_ref[...], nb)
        kd = k_ref[...].astype(F32) * e
        q = q_ref[...].astype(F32) * scale
        for h in range(GLA_HEADS):
            sk = slice(h * GLA_DK, (h + 1) * GLA_DK)
            sv = slice(h * GLA_DV, (h + 1) * GLA_DV)
            state = st[h]
            for c in range(nb):
                rows = _chunk_rows(c)
                sp_ref[c, h] = state
                state = dec[c * GLA_CHUNK:c * GLA_CHUNK + 1, sk] * state + _dot(v_ref[rows, sv], kd[rows, sk], _TN)
                o = _dot(q[rows, sk], state, _NT)
                on = o * _rms(o)
                og_ref[rows, sv] = (on * gn_ref[:, sv] * _silu(r_ref[rows, sv].astype(F32))).astype(og_ref.dtype)
            st[h] = state
        run_comm(("last",))

    s = _gla_specs(nblk, nb, False)
    return pl.pallas_call(
        body, grid=(nblk,),
        in_specs=[s["q"], s["k"], s["v"], s["r"], s["glr"], s["wg2"], s["bg"], s["gn"], s["tri"]] + [_ANY_SPEC] * nci,
        out_specs=[s["row"], s["st"]] + [_ANY_SPEC] * nco,
        out_shape=[jax.ShapeDtypeStruct((length, D_MODEL), MXU_DT),
                   jax.ShapeDtypeStruct((nc, GLA_HEADS, GLA_DV, GLA_DK), F32)] + comm.out_shapes,
        scratch_shapes=[pltpu.VMEM((GLA_HEADS, GLA_DV, GLA_DK), F32)] + comm.scratch,
        name="gla_fwd", compiler_params=_params(("arbitrary",)),
    )(proj, proj, proj, proj, proj, wg2, bg, gn, tri, *comm.ins)


def _gla_bwd(proj, d_og, s_prev, wg2, bg, gn, tri):
    length = proj.shape[0]
    nc = length // GLA_CHUNK
    nb = tri.shape[0] // GLA_CHUNK
    nblk = nc // nb
    scale = GLA_DK ** -0.5

    def body(q_ref, k_ref, v_ref, r_ref, glr_ref, dog_ref, sp_ref, wg2_ref, bg_ref, gn_ref, tri_ref,
             dp_ref, dwg2_ref, dbg_ref, dgn_ref, dst):
        dq_ref = dp_ref.at[:, pl.ds(0, GLA_QK)]
        dk_ref = dp_ref.at[:, pl.ds(GLA_QK, GLA_QK)]
        dv_ref = dp_ref.at[:, pl.ds(2 * GLA_QK, D_MODEL)]
        dr_ref = dp_ref.at[:, pl.ds(2 * GLA_QK + D_MODEL, D_MODEL)]
        dglr_ref = dp_ref.at[:, pl.ds(2 * GLA_QK + 2 * D_MODEL, GLA_RANK_PAD)]

        @pl.when(pl.program_id(0) == 0)
        def _():
            dst[...] = jnp.zeros_like(dst)
            dwg2_ref[...] = jnp.zeros_like(dwg2_ref)
            dbg_ref[...] = jnp.zeros_like(dbg_ref)
            dgn_ref[...] = jnp.zeros_like(dgn_ref)
        glr = glr_ref[...]
        pre, e, dec = _gla_gates(glr, wg2_ref[...], bg_ref[...], tri_ref[...], nb)
        k = k_ref[...].astype(F32)
        kd = k * e
        q = q_ref[...].astype(F32) * scale
        dkd_heads, ddec_heads = [], []
        for h in range(GLA_HEADS):
            sk = slice(h * GLA_DK, (h + 1) * GLA_DK)
            sv = slice(h * GLA_DV, (h + 1) * GLA_DV)
            gnh = gn_ref[:, sv]
            stps, grads_in = [], []
            dgn = jnp.zeros((1, GLA_DV), F32)
            for c in range(nb):
                rows = _chunk_rows(c)
                stp = sp_ref[c, h]
                stn = dec[c * GLA_CHUNK:c * GLA_CHUNK + 1, sk] * stp + _dot(v_ref[rows, sv], kd[rows, sk], _TN)
                o = _dot(q[rows, sk], stn, _NT)
                rinv = _rms(o)
                on = o * rinv
                rv = r_ref[rows, sv].astype(F32)
                sg = _sigmoid(rv)
                dog = dog_ref[rows, sv].astype(F32)
                d_ong = dog * (rv * sg)
                dr_ref[rows, sv] = (dog * (on * gnh) * (sg * (1.0 + rv * (1.0 - sg)))).astype(dr_ref.dtype)
                dgn = dgn + jnp.sum(d_ong * on, axis=0, keepdims=True)
                d_on = d_ong * gnh
                do = rinv * (d_on - on * jnp.mean(d_on * on, axis=-1, keepdims=True))
                dq_ref[rows, sk] = (_dot(do, stn) * scale).astype(dq_ref.dtype)
                stps.append(stp)
                grads_in.append(_dot(do, q[rows, sk], _TN))
            dgn_ref[:, sv] += dgn
            carry = dst[h]
            dkd_rows, ddec_rows = [None] * nb, [None] * nb
            for c in reversed(range(nb)):
                rows = _chunk_rows(c)
                dstn = carry + grads_in[c]
                carry = dec[c * GLA_CHUNK:c * GLA_CHUNK + 1, sk] * dstn
                ddec_rows[c] = jnp.sum(dstn * stps[c], axis=0, keepdims=True)
                dv_ref[rows, sv] = _dot(kd[rows, sk], dstn, _NT).astype(dv_ref.dtype)
                dkd_rows[c] = _dot(v_ref[rows, sv], dstn)
            dst[h] = carry
            dkd_heads.append(jnp.concatenate(dkd_rows, axis=0))
            ddec_heads.append(_per_chunk(ddec_rows, nb))
        dkd = jnp.concatenate(dkd_heads, axis=1)
        ddec = jnp.concatenate(ddec_heads, axis=1)
        dk_ref[...] = (dkd * e).astype(dk_ref.dtype)
        w = dkd * kd
        dgend = _per_chunk([jnp.sum(w[_chunk_rows(c)], axis=0, keepdims=True) for c in range(nb)], nb) + ddec * dec
        dla = dgend - _dot_01_left(tri_ref[...], w, _TN)
        dpre = dla * (1.0 - _sigmoid(pre)) * (1.0 / GLA_TAU)
        dwg2_ref[...] += _dot(glr, dpre, _TN)
        dbg_ref[...] += jnp.sum(dpre, axis=0, keepdims=True)
        dglr_ref[...] = _dot(dpre, wg2_ref[...], _NT).astype(dglr_ref.dtype)

    s = _gla_specs(nblk, nb, True)
    return pl.pallas_call(
        body, grid=(nblk,),
        in_specs=[s["q"], s["k"], s["v"], s["r"], s["glr"], s["row"], s["st"], s["wg2"], s["bg"], s["gn"], s["tri"]],
        out_specs=[s["rowp"], s["wg2"], s["bg"], s["gn"]],
        out_shape=[jax.ShapeDtypeStruct((length, GLA_INP), MXU_DT),
                   jax.ShapeDtypeStruct((GLA_RANK_PAD, GLA_QK), F32), jax.ShapeDtypeStruct((1, GLA_QK), F32),
                   jax.ShapeDtypeStruct((1, D_MODEL), F32)],
        scratch_shapes=[pltpu.VMEM((GLA_HEADS, GLA_DV, GLA_DK), F32)],
        name="gla_bwd", compiler_params=_params(("arbitrary",)),
    )(proj, proj, proj, proj, proj, d_og, s_prev, wg2, bg, gn, tri)


def _local_step(x, tgt, mods, nrm, s5p, w, shards, core, *, row_tile=256, s5_tc=1024):
    length = x.shape[0]
    tmm = 512
    glu_sh, gin_sh, gout_sh, ff1a_sh, ff1b_sh, ff2a_sh, ff2b_sh = shards
    w = dict(w)
    rc = functools.partial(_rowcall, tile=row_tile)
    (sh1a, sc1a, gt1a, sh2a, sc2a, gt2a), (sh1b, sc1b, gt1b, sh2b, sc2b, gt2b) = mods
    vec = (1, D_MODEL)
    row32, row16 = (D_MODEL, F32), (D_MODEL, MXU_DT)

    (h0,) = rc(lambda *a: _f_pn(*a)[1:], [_full(x)], [nrm["mix"][0], sc1a, sh1a], [row32], [], name="pn0")
    lam_re, lam_im, bb_re, bb_im = _s5_prep(s5p["a_re"], s5p["a_im"], s5p["log_dt"], s5p["bt_re"], s5p["bt_im"], s5p["e01"])
    bbd_re = _blockdiag_b(bb_re.reshape(S5_GROUP, S5_GROUPS, S5_STATE)).astype(MXU_DT)
    bbd_im = _blockdiag_b(bb_im.reshape(S5_GROUP, S5_GROUPS, S5_STATE)).astype(MXU_DT)
    cbd_re = _blockdiag_c(s5p["c_re"]).astype(MXU_DT)
    cbd_im = _blockdiag_c(s5p["c_im"]).astype(MXU_DT)
    z0, st_re, st_im, glu_s, ff1a_s, ff2a_s = _s5_fwd(
        h0, lam_re, lam_im, bbd_re, bbd_im, cbd_re, cbd_im, s5p["d"], _ag_comm([glu_sh, ff1a_sh, ff2a_sh]), tc=s5_tc)
    w["glu"] = glu_s.transpose(1, 0, 2).reshape(D_MODEL, 2 * D_MODEL)
    w["ff1"] = [ff1a_s.transpose(1, 0, 2).reshape(D_MODEL, D_FF), None]
    w["ff2"] = [ff2a_s.reshape(D_FF, D_MODEL), None]
    vg = _mm(z0, w["glu"], "nn", F32, tm=tmm, tn=2048, name="glu_mm")
    x1, h1 = rc(_f_glu_res_pn, [_full(x), (vg, D_MODEL, 0), (vg, D_MODEL, 1)], [gt1a, nrm["mlp"][0], sc2a, sh2a],
                [row32, row16], [], name="node1")
    relu = lambda acc: jnp.maximum(acc, 0.0)
    sq = lambda a: a * a
    a0, gin_s = _mm(h1, w["ff1"][0], "nn", MXU_DT, tm=tmm, tn=2048, name="ff1a", out_fn=relu, comm=_ag_comm([gin_sh]))
    f0, ff1b_s = _mm(a0, w["ff2"][0], "nn", F32, tm=tmm, tn=1024, name="ff2a", a_fn=sq, comm=_ag_comm([ff1b_sh]))
    gin_full = gin_s.transpose(1, 0, 2).reshape(D_MODEL, GLA_IN)
    q_, k_, v_, glr_, r_ = jnp.split(gin_full, [GLA_QK, 2 * GLA_QK, 2 * GLA_QK + D_MODEL, 2 * GLA_QK + D_MODEL + GLA_RANK], axis=1)
    w["gin"] = jnp.concatenate([q_, k_, v_, r_, glr_, jnp.zeros((D_MODEL, GLA_RANK_PAD - GLA_RANK), MXU_DT)], axis=1)
    w["ff1"][1] = ff1b_s.transpose(1, 0, 2).reshape(D_MODEL, D_FF)
    x2, h2 = rc(_f_res_pn, [_full(x1), _full(f0)], [gt2a, nrm["mix"][1], sc1b, sh1b], [row32, row16], [], name="node2")
    proj = _mm(h2, w["gin"], "nn", MXU_DT, tm=tmm, tn=GLA_INP, name="gla_in")
    og, s_prev, gout_s, ff2b_s = _gla_fwd(proj, w["wg2"], w["bg"], w["gn"], w["tri"], _ag_comm([gout_sh, ff2b_sh]))
    w["gout"] = gout_s.reshape(D_MODEL, D_MODEL)
    w["ff2"][1] = ff2b_s.reshape(D_FF, D_MODEL)
    y1 = _mm(og, w["gout"], "nn", F32, tm=tmm, tn=1024, name="gla_out")
    x3, h3 = rc(_f_res_pn, [_full(x2), _full(y1)], [gt1b, nrm["mlp"][1], sc2b, sh2b], [row32, row16], [], name="node3")
    a1 = _mm(h3, w["ff1"][1], "nn", MXU_DT, tm=tmm, tn=2048, name="ff1b", out_fn=relu)
    f1 = _mm(a1, w["ff2"][1], "nn", F32, tm=tmm, tn=1024, name="ff2b", a_fn=sq)

    g = {}
    dx, df, g["gt2b"], g["nf"], loss = rc(_g_final, [_full(x3), _full(f1), _full(tgt)], [gt2b, nrm["final"]],
                                          [row32, row16], [vec, vec, (8, 128)], name="final")

    def mlp_bwd(df, a, h, w1, w2, tag):
        dw2 = _mm(a, df, "tn", MXU_DT, tm=1024, tn=1024, name="dff2" + tag, a_fn=sq)
        du = _mm(df, w2, "nt", MXU_DT, tm=tmm, tn=2048, name="dact" + tag, extra=a, out_fn=lambda acc, e: acc * (2.0 * e))
        dw1 = _mm(h, du, "tn", MXU_DT, tm=1024, tn=D_FF // N_DEV, name="dff1" + tag, by_owner=True)
        dh = _mm(du, w1, "nt", F32, tm=tmm, tn=1024, name="dh" + tag)
        return dw1, dw2, dh

    def node_bwd(f, prim_rows, cots, pars, row_want, outs, name):
        nrow = len(prim_rows)
        return rc(_vjp_of(f, nrow, len(cots), row_want), prim_rows + cots, pars, outs, [vec] * len(pars), name=name)

    g["ff1b"], g["ff2b"], dh3 = mlp_bwd(df, a1, h3, w["ff1"][1], w["ff2"][1], "b")
    dx, dy1, g["gt1b"], g["mlp1"], g["sc2b"], g["sh2b"] = node_bwd(
        _f_res_pn, [_full(x2), _full(y1)], [_full(dx), _full(dh3)], [gt1b, nrm["mlp"][1], sc2b, sh2b], (0, 1),
        [row32, row16], "node3_bwd")
    g["gout"] = _mm(og, dy1, "tn", MXU_DT, tm=512, tn=1024, name="dgout")
    d_og = _mm(dy1, w["gout"], "nt", MXU_DT, tm=tmm, tn=1024, name="dog")
    dproj, g["wg2"], g["bg"], g["gn"] = _gla_bwd(proj, d_og, s_prev, w["wg2"], w["bg"], w["gn"], w["tri"])
    g["gin"] = _mm(h2, dproj, "tn", MXU_DT, tm=512, tn=640, name="dgin")
    dh2 = _mm(dproj, w["gin"], "nt", F32, tm=tmm, tn=1024, name="dh2")
    dx, df0, g["gt2a"], g["mix1"], g["sc1b"], g["sh1b"] = node_bwd(
        _f_res_pn, [_full(x1), _full(f0)], [_full(dx), _full(dh2)], [gt2a, nrm["mix"][1], sc1b, sh1b], (0, 1),
        [row32, row16], "node2_bwd")
    g["ff1a"], g["ff2a"], dh1 = mlp_bwd(df0, a0, h1, w["ff1"][0], w["ff2"][0], "a")
    glu_vjp = _vjp_of(_f_glu_res_pn, 3, 2, (0, 1, 2))

    def glu_bwd(*a):
        r = glu_vjp(*a)
        return (r[0], jnp.concatenate([r[1], r[2]], axis=1)) + r[3:]

    dx, dvg, g["gt1a"], g["mlp0"], g["sc2a"], g["sh2a"] = rc(
        glu_bwd, [_full(x), (vg, D_MODEL, 0), (vg, D_MODEL, 1), _full(dx), _full(dh1)], [gt1a, nrm["mlp"][0], sc2a, sh2a],
        [row32, (2 * D_MODEL, MXU_DT)], [vec] * 4, name="node1_bwd")
    gin_g = g.pop("gin")
    gin_g = jnp.concatenate([gin_g[:, :2 * GLA_QK + D_MODEL], gin_g[:, GLA_INP - GLA_RANK_PAD:GLA_INP - GLA_RANK_PAD + GLA_RANK],
                             gin_g[:, 2 * GLA_QK + D_MODEL:2 * GLA_QK + 2 * D_MODEL]], axis=1)
    per_owner = [gin_g.reshape(D_MODEL, N_DEV, GLA_IN // N_DEV).transpose(1, 0, 2),
                 g.pop("gout").reshape(N_DEV, D_MODEL // N_DEV, D_MODEL), g.pop("ff1a"), g.pop("ff1b"),
                 g.pop("ff2a").reshape(N_DEV, D_FF // N_DEV, D_MODEL), g.pop("ff2b").reshape(N_DEV, D_FF // N_DEV, D_MODEL)]
    res = _mm(z0, dvg, "tn", MXU_DT, tm=1024, tn=2 * D_MODEL // N_DEV, name="dglu", by_owner=True, comm=_sibling_comm(per_owner))
    glu_g, from_sibling = res[0], list(res[1:])
    dz0, glu_sib = _mm(dvg, w["glu"], "nt", MXU_DT, tm=tmm, tn=1024, name="dz0", comm=_sibling_comm([glu_g]))
    per_owner = [glu_g] + per_owner
    chip_sum = _pair_add(per_owner, [glu_sib] + from_sibling, core, name="rs_add")
    res = _s5_bwd(h0, dz0, st_re, st_im, lam_re, lam_im, bbd_re, bbd_im, cbd_re, cbd_im, s5p["d"], _chips_comm(chip_sum), tc=s5_tc)
    du0, dbbd_re, dbbd_im, dcbd_re, dcbd_im, dlam_re, dlam_im, g["s5_d"] = res[:8]
    from_chips = res[8:]
    g["s5_c_re"] = _unblockdiag_c(dcbd_re)
    g["s5_c_im"] = _unblockdiag_c(dcbd_im)
    g["s5_a_re"], g["s5_a_im"], g["s5_log_dt"], g["s5_bt_re"], g["s5_bt_im"] = _s5_prep_bwd(
        s5p["a_re"], s5p["a_im"], s5p["log_dt"], s5p["bt_re"], s5p["bt_im"], s5p["e01"],
        dlam_re, dlam_im, _unblockdiag_b(dbbd_re), _unblockdiag_b(dbbd_im))
    grad_x, g["mix0"], g["sc1a"], g["sh1a"] = node_bwd(
        _f_pn, [_full(x)], [_full(dx), _full(du0)], [nrm["mix"][0], sc1a, sh1a], (0,), [row32], "node0_bwd")
    return loss[0, 0], grad_x, g, chip_sum, from_chips


_MESH = pl.DeviceIdType.MESH
_VMEM_SPEC = pl.BlockSpec(memory_space=pltpu.VMEM)
_ANY_SPEC = pl.BlockSpec(memory_space=pl.ANY)


def _my_place():
    ix, iy, ic = lax.axis_index("x"), lax.axis_index("y"), lax.axis_index("c")
    return ix, iy, ic


def _exchange(x, *, gather, name):
    r = x.shape[-2]

    def body(x_ref, o_ref, ssem, rsem):
        ix, iy, ic = _my_place()
        me = 4 * ix + 2 * iy + ic
        if gather:
            o_ref[me] = x_ref[...]
        else:
            o_ref[me] = x_ref[me]
        copies = []
        for k in range(1, N_DEV):
            tx, ty, tc = ix ^ (k >> 2), iy ^ ((k >> 1) & 1), ic ^ (k & 1)
            src = x_ref if gather else x_ref.at[4 * tx + 2 * ty + tc]
            cp = pltpu.make_async_remote_copy(src_ref=src, dst_ref=o_ref.at[me], send_sem=ssem.at[k - 1],
                                              recv_sem=rsem.at[k - 1], device_id=(tx, ty, tc), device_id_type=_MESH)
            cp.start()
            copies.append(cp)
        for cp in copies:
            cp.wait()

    return pl.pallas_call(
        body, out_shape=jax.ShapeDtypeStruct((N_DEV, r, 128), x.dtype), in_specs=[_VMEM_SPEC], out_specs=_VMEM_SPEC,
        scratch_shapes=[pltpu.SemaphoreType.DMA((N_DEV - 1,)), pltpu.SemaphoreType.DMA((N_DEV - 1,))], name=name,
    )(x)


class _Comm:
    def __init__(self, ins, out_shapes, scratch, phases):
        self.ins, self.out_shapes, self.scratch, self.phases = list(ins), list(out_shapes), list(scratch), phases

    def run(self, step, n_steps, in_refs, out_refs, scratch_refs, only):
        when = {"first": 0, "mid": (7 * n_steps) // 8, "last": n_steps - 1}
        for phase, fn in self.phases:
            if phase in only:
                pl.when(step == when[phase])(functools.partial(fn, in_refs, out_refs, scratch_refs))


def _ag_comm(xs):
    n = len(xs)

    def parts():
        ix, iy, ic = _my_place()
        return ic, (ix, iy, ic), (ix, iy, 1 - ic), [(1 - ix, iy), (ix, 1 - iy), (1 - ix, 1 - iy)]

    def copy(ins, outs, sc, a, k, block, to, from_x=False):
        px, py, pc = block
        slot = outs[a].at[4 * px + 2 * py + pc]
        return pltpu.make_async_remote_copy(
            src_ref=ins[a] if from_x else slot, dst_ref=slot, send_sem=sc[0].at[7 * a + k], recv_sem=sc[1].at[7 * a + k],
            device_id=to, device_id_type=_MESH)

    def local(ins, outs, sc, a, me):
        return pltpu.make_async_copy(ins[a], outs[a].at[4 * me[0] + 2 * me[1] + me[2]], sc[2].at[a])

    def start(ins, outs, sc):
        ic, me, sibling, chips = parts()
        for a in range(n):
            local(ins, outs, sc, a, me).start()
            copy(ins, outs, sc, a, 0, me, sibling, True).start()
            for j, chip in enumerate(chips):
                copy(ins, outs, sc, a, 1 + j, me, (*chip, ic), True).start()

    def forward(ins, outs, sc):
        ic, me, sibling, chips = parts()
        for a in range(n):
            for j, chip in enumerate(chips):
                copy(ins, outs, sc, a, 1 + j, (*chip, ic), me).wait_recv()
                copy(ins, outs, sc, a, 4 + j, (*chip, ic), sibling).start()

    def finish(ins, outs, sc):
        ic, me, sibling, chips = parts()
        for a in range(n):
            copy(ins, outs, sc, a, 0, sibling, me).wait_recv()
            for j, chip in enumerate(chips):
                copy(ins, outs, sc, a, 4 + j, (*chip, 1 - ic), me).wait_recv()
        for a in range(n):
            copy(ins, outs, sc, a, 0, me, sibling, True).wait_send()
            for j, chip in enumerate(chips):
                copy(ins, outs, sc, a, 1 + j, me, (*chip, ic), True).wait_send()
                copy(ins, outs, sc, a, 4 + j, (*chip, ic), sibling).wait_send()
            local(ins, outs, sc, a, me).wait()

    return _Comm(xs, [jax.ShapeDtypeStruct((N_DEV,) + x.shape, x.dtype) for x in xs],
                 [pltpu.SemaphoreType.DMA((7 * n,)), pltpu.SemaphoreType.DMA((7 * n,)), pltpu.SemaphoreType.DMA((n,))],
                 [("first", start), ("mid", forward), ("last", finish)])


def _chips_comm(ps):
    n = len(ps)

    def copies(ins, outs, sc):
        ix, iy, ic = _my_place()
        out = []
        for a in range(n):
            for k in range(1, 4):
                tx, ty = ix ^ (k >> 1), iy ^ (k & 1)
                out.append(pltpu.make_async_remote_copy(
                    src_ref=ins[a].at[2 * tx + ty], dst_ref=outs[a].at[k - 1], send_sem=sc[0].at[3 * a + k - 1],
                    recv_sem=sc[1].at[3 * a + k - 1], device_id=(tx, ty, ic), device_id_type=_MESH))
        return out

    def start(ins, outs, sc):
        for cp in copies(ins, outs, sc):
            cp.start()

    def finish(ins, outs, sc):
        for cp in copies(ins, outs, sc):
            cp.wait()

    return _Comm(ps, [jax.ShapeDtypeStruct((3,) + p.shape[1:], p.dtype) for p in ps],
                 [pltpu.SemaphoreType.DMA((3 * n,)), pltpu.SemaphoreType.DMA((3 * n,))], [("first", start), ("last", finish)])


def _sibling_comm(gs):
    n = len(gs)

    def copies(ins, outs, sc):
        ix, iy, ic = _my_place()
        return [pltpu.make_async_remote_copy(src_ref=ins[a].at[2 * q + 1 - ic], dst_ref=outs[a].at[q], send_sem=sc[0].at[4 * a + q],
                                             recv_sem=sc[1].at[4 * a + q], device_id=(ix, iy, 1 - ic), device_id_type=_MESH)
                for a in range(n) for q in range(4)]

    def start(ins, outs, sc):
        for cp in copies(ins, outs, sc):
            cp.start()

    def finish(ins, outs, sc):
        for cp in copies(ins, outs, sc):
            cp.wait()

    return _Comm(gs, [jax.ShapeDtypeStruct((4,) + g.shape[1:], g.dtype) for g in gs],
                 [pltpu.SemaphoreType.DMA((4 * n,)), pltpu.SemaphoreType.DMA((4 * n,))], [("first", start), ("last", finish)])


def _ada_fwd(c_all, w_ada, b_cols):
    def body(c_ref, w_ref, b_ref, o_ref):
        cs = _silu(c_ref[...])
        for i in range(2):
            o_ref[i] = _dot(cs, w_ref[i]) + b_ref[pl.ds(i, 1), :]
    return pl.pallas_call(body, out_shape=jax.ShapeDtypeStruct((2, N_DEV, w_ada.shape[2]), F32), name="ada_fwd",
                          compiler_params=pltpu.CompilerParams(vmem_limit_bytes=VMEM_LIMIT))(c_all, w_ada, b_cols)


def _ada_bwd(c_all, dm):
    def body(c_ref, d_ref, o_ref):
        cs = _silu(c_ref[...])
        for i in range(2):
            o_ref[i] = _dot(cs, d_ref[i], _TN)
    return pl.pallas_call(body, out_shape=jax.ShapeDtypeStruct((2, D_MODEL, dm.shape[2]), F32), name="ada_bwd",
                          compiler_params=pltpu.CompilerParams(vmem_limit_bytes=VMEM_LIMIT))(c_all, dm)


def _pair_add(gs, recvs, core, *, name):
    n = len(gs)

    def body(core_ref, *refs):
        for a in range(n):
            refs[2 * n + a][...] = (refs[a][...].astype(F32) + refs[n + a][...].astype(F32)).astype(refs[2 * n + a].dtype)

    own = [pl.BlockSpec((1,) + g.shape[1:], lambda q, core_ref: (2 * q + core_ref[0], 0, 0)) for g in gs]
    slab = [pl.BlockSpec((1,) + g.shape[1:], lambda q, core_ref: (q, 0, 0)) for g in gs]
    grid_spec = pltpu.PrefetchScalarGridSpec(num_scalar_prefetch=1, grid=(4,), in_specs=own + slab, out_specs=slab)
    return pl.pallas_call(body, grid_spec=grid_spec, out_shape=[jax.ShapeDtypeStruct((4,) + g.shape[1:], g.dtype) for g in gs],
                          name=name, compiler_params=_params(("parallel",)))(core, *gs, *recvs)


def _sum_slots(x, *, name):
    def body(x_ref, o_ref):
        acc = x_ref[0]
        for s in range(1, N_DEV):
            acc = acc + x_ref[s]
        o_ref[...] = acc
    return pl.pallas_call(body, out_shape=jax.ShapeDtypeStruct(x.shape[1:], F32), name=name)(x)


def _adamw_math(w, m, v, g):
    mn = ADAM_B1 * m + (1.0 - ADAM_B1) * g
    vn = ADAM_B2 * v + (1.0 - ADAM_B2) * (g * g)
    m_hat = mn / (1.0 - ADAM_B1 ** ADAM_STEP)
    v_hat = vn / (1.0 - ADAM_B2 ** ADAM_STEP)
    return -ADAM_LR * (m_hat / (jnp.sqrt(v_hat) + ADAM_EPS) + ADAM_WD * w), mn, vn


def _adamw_multi(ws, ms, vs, gs, *, name):
    n = len(ws)

    def body(*refs):
        for i in range(n):
            g = refs[3 * n + i][...]
            o = refs[4 * n + 4 * i:4 * n + 4 * i + 4]
            o[0][...] = g
            o[1][...], o[2][...], o[3][...] = _adamw_math(refs[i][...], refs[n + i][...], refs[2 * n + i][...], g)

    out_shape = [jax.ShapeDtypeStruct(w.shape, F32) for w in ws for _ in range(4)]
    res = pl.pallas_call(body, out_shape=out_shape, name=name,
                         compiler_params=pltpu.CompilerParams(vmem_limit_bytes=VMEM_LIMIT))(*ws, *ms, *vs, *gs)
    return [res[4 * i:4 * i + 4] for i in range(n)]


def _adamw(w, m, v, gparts, *, tile, name, sel=None):
    r, cdim = w.shape
    ng = len(gparts)
    sel = jnp.zeros((1,), jnp.int32) if sel is None else sel

    def body(*refs):
        w_ref, m_ref, v_ref = refs[1:4]
        g = None
        for p, part in zip(refs[4:4 + ng], gparts):
            pv = (p[0] if isinstance(part, tuple) else p[...]).astype(F32)
            g = pv if g is None else g + pv
        g_ref, d_ref, nm_ref, nv_ref = refs[4 + ng:]
        g_ref[...] = g
        d_ref[...], nm_ref[...], nv_ref[...] = _adamw_math(w_ref[...], m_ref[...], v_ref[...], g)

    spec = pl.BlockSpec((tile, cdim), lambda i, s: (i, 0))

    def part_spec(part):
        if not isinstance(part, tuple):
            return spec
        slab = part[1]
        if slab is None:
            return pl.BlockSpec((1, tile, cdim), lambda i, s: (s[0], i, 0))
        return pl.BlockSpec((1, tile, cdim), lambda i, s: (slab, i, 0))

    grid_spec = pltpu.PrefetchScalarGridSpec(
        num_scalar_prefetch=1, grid=(r // tile,), in_specs=[spec] * 3 + [part_spec(p) for p in gparts], out_specs=[spec] * 4)
    return pl.pallas_call(
        body, grid_spec=grid_spec, out_shape=[jax.ShapeDtypeStruct(w.shape, F32)] * 4, name=name,
        compiler_params=_params(("parallel",)),
    )(sel, w, m, v, *[p[0] if isinstance(p, tuple) else p for p in gparts])


_REP_ROWS = 272
_REP_SIZE = 2 * 1024 * 2 + 4096 * 2 + 64 + 65536 * 4 + 1024 + 1024


def _pad_rows(v, rows):
    return jnp.pad(v.reshape(-1), (0, rows * 128 - v.size)).reshape(rows, 128)


def kernel(x, c, w_ada, b_ada, norm_mix, norm_mlp, s5_a_re, s5_a_im, s5_log_dt, s5_b_re, s5_b_im, s5_c_re, s5_c_im, s5_d, s5_w_glu, gla_w_in, gla_w_gate2, gla_b_gate, gla_g_norm, gla_w_out, w_ff1, w_ff2, norm_final, loss_target, m_w_ada, m_b_ada, m_norm_mix, m_norm_mlp, m_s5_a_re, m_s5_a_im, m_s5_log_dt, m_s5_b_re, m_s5_b_im, m_s5_c_re, m_s5_c_im, m_s5_d, m_s5_w_glu, m_gla_w_in, m_gla_w_gate2, m_gla_b_gate, m_gla_g_norm, m_gla_w_out, m_w_ff1, m_w_ff2, m_norm_final, v_w_ada, v_b_ada, v_norm_mix, v_norm_mlp, v_s5_a_re, v_s5_a_im, v_s5_log_dt, v_s5_b_re, v_s5_b_im, v_s5_c_re, v_s5_c_im, v_s5_d, v_s5_w_glu, v_gla_w_in, v_gla_w_gate2, v_gla_b_gate, v_gla_g_norm, v_gla_w_out, v_w_ff1, v_w_ff2, v_norm_final):
    ix, iy, ic = _my_place()
    me = 4 * ix + 2 * iy + ic
    ada_w = w_ada.shape[2]

    msg = jnp.concatenate([c.reshape(8, 128), gla_w_gate2[0].reshape(8, 128), _pad_rows(gla_b_gate, 1),
                           gla_g_norm.reshape(1, 128), jnp.zeros((6, 128), F32)])
    got = _exchange(msg, gather=True, name="gather_small")
    c_all = got[:, 0:8].reshape(N_DEV, D_MODEL)
    wg2 = got[:, 8:16].reshape(N_DEV, GLA_RANK, 64).transpose(1, 0, 2).reshape(GLA_RANK, GLA_QK)
    bg = got[:, 16, :64].reshape(1, GLA_QK)
    gn = got[:, 17, :].reshape(1, D_MODEL)

    b_cols = lax.dynamic_slice_in_dim(b_ada, me * ada_w, ada_w, axis=1)
    mod_cols = _ada_fwd(c_all, w_ada, b_cols)
    pay = jnp.pad(mod_cols.transpose(1, 0, 2).reshape(N_DEV, 12, 128), ((0, 0), (0, 4), (0, 0)))
    mod = _exchange(pay, gather=False, name="a2a_mod")[:, :12].reshape(N_DEV, 2, ada_w).transpose(1, 0, 2).reshape(2, 6 * D_MODEL)
    mods = [[mod[i:i + 1, j * D_MODEL:(j + 1) * D_MODEL] for j in range(6)] for i in range(2)]

    big_w = [s5_w_glu[0], gla_w_in[0], gla_w_out[0], w_ff1[0], w_ff1[1], w_ff2[0], w_ff2[1]]
    big_m = [m_s5_w_glu[0], m_gla_w_in[0], m_gla_w_out[0], m_w_ff1[0], m_w_ff1[1], m_w_ff2[0], m_w_ff2[1]]
    big_v = [v_s5_w_glu[0], v_gla_w_in[0], v_gla_w_out[0], v_w_ff1[0], v_w_ff1[1], v_w_ff2[0], v_w_ff2[1]]
    rows_i = lax.broadcasted_iota(jnp.int32, (GLA_NB * GLA_CHUNK, GLA_NB * GLA_CHUNK), 0)
    cols_i = lax.broadcasted_iota(jnp.int32, (GLA_NB * GLA_CHUNK, GLA_NB * GLA_CHUNK), 1)
    tri = ((rows_i // GLA_CHUNK == cols_i // GLA_CHUNK) & (cols_i <= rows_i)).astype(F32)
    w = dict(wg2=jnp.pad(wg2, ((0, GLA_RANK_PAD - GLA_RANK), (0, 0))), bg=bg, gn=gn, tri=tri)
    core = ic.reshape(1).astype(jnp.int32)
    chip = (2 * ix + iy).reshape(1).astype(jnp.int32)
    nrm = dict(mix=[norm_mix[i:i + 1] for i in range(2)], mlp=[norm_mlp[i:i + 1] for i in range(2)], final=norm_final.reshape(1, D_MODEL))
    e01 = (lax.broadcasted_iota(jnp.int32, (S5_GROUPS, S5_NSTATE), 1) // S5_STATE
           == lax.broadcasted_iota(jnp.int32, (S5_GROUPS, S5_NSTATE), 0)).astype(F32)
    s5p = dict(a_re=s5_a_re.reshape(1, S5_NSTATE), a_im=s5_a_im.reshape(1, S5_NSTATE), log_dt=s5_log_dt,
               bt_re=s5_b_re[0].transpose(2, 0, 1).reshape(S5_GROUP, S5_NSTATE),
               bt_im=s5_b_im[0].transpose(2, 0, 1).reshape(S5_GROUP, S5_NSTATE),
               c_re=s5_c_re[0], c_im=s5_c_im[0], d=s5_d, e01=e01)

    loss_local, grad_x, g, chip_sum, from_chips = _local_step(
        x[0], loss_target[0], mods, nrm, s5p, w, [a.astype(MXU_DT) for a in big_w], core)
    loss = lax.psum(loss_local, ("x", "y", "c"))
    big = []
    for i in range(len(big_w)):
        parts = [(chip_sum[i], None), (from_chips[i], 0), (from_chips[i], 1), (from_chips[i], 2)]
        big.append(_adamw(big_w[i], big_m[i], big_v[i], parts, tile=min(512, big_w[i].shape[0]), name="adamw_big%d" % i, sel=chip))

    rep = [jnp.concatenate([g["mix0"], g["mix1"]]), jnp.concatenate([g["mlp0"], g["mlp1"]]), g["s5_a_re"], g["s5_a_im"], g["s5_log_dt"],
           g["s5_bt_re"].reshape(S5_GROUP, S5_GROUPS, S5_STATE).transpose(1, 2, 0), g["s5_bt_im"].reshape(S5_GROUP, S5_GROUPS, S5_STATE).transpose(1, 2, 0),
           g["s5_c_re"], g["s5_c_im"], g["s5_d"], g["nf"]]
    rep_shapes = [(2, D_MODEL), (2, D_MODEL), (1, 64, 64), (1, 64, 64), (1, 64), (1, 64, 64, 16), (1, 64, 64, 16), (1, 64, 16, 64), (1, 64, 16, 64), (1, D_MODEL), (D_MODEL,)]
    rep_flat = jnp.concatenate([a.reshape(-1) for a in rep])
    rep_blk = jnp.pad(rep_flat, (0, N_DEV * _REP_ROWS * 128 - _REP_SIZE)).reshape(N_DEV, _REP_ROWS, 128)
    dmod = jnp.stack([jnp.concatenate([g["sh1" + t], g["sc1" + t], g["gt1" + t], g["sh2" + t], g["sc2" + t], g["gt2" + t]], axis=1)[0] for t in "ab"])
    msg = jnp.concatenate([
        rep_blk,
        g["wg2"][:GLA_RANK].reshape(GLA_RANK, N_DEV, 64).transpose(1, 0, 2).reshape(N_DEV, 8, 128),
        jnp.pad(g["bg"].reshape(N_DEV, 1, 64), ((0, 0), (0, 0), (0, 64))),
        g["gn"].reshape(N_DEV, 1, 128),
        dmod.reshape(2, N_DEV, ada_w).transpose(1, 0, 2).reshape(N_DEV, 12, 128),
        jnp.zeros((N_DEV, 2, 128), F32),
    ], axis=1)
    got = _exchange(msg, gather=False, name="a2a_small_grads")
    tot = _sum_slots(got, name="sum_small_grads")
    dm = got[:, 282:294].reshape(N_DEV, 2, ada_w).transpose(1, 0, 2)
    g_w_ada = _ada_bwd(c_all, dm)
    back = _exchange(jnp.concatenate([tot[0:_REP_ROWS], tot[282:294], jnp.zeros((4, 128), F32)]), gather=True, name="gather_small_grads")
    rep_sum = back[:, :_REP_ROWS].reshape(-1)[:_REP_SIZE]
    g_b_ada = back[:, _REP_ROWS:_REP_ROWS + 12].reshape(N_DEV, 2, ada_w).transpose(1, 0, 2).reshape(2, 6 * D_MODEL)
    g_rep, off = [], 0
    for s in rep_shapes:
        n = math.prod(s)
        g_rep.append(rep_sum[off:off + n].reshape(s))
        off += n
    g_small = g_rep + [g_b_ada, tot[272:280].reshape(GLA_RANK, 64)[None], tot[280, :64][None], tot[281][None]]
    p_small = [norm_mix, norm_mlp, s5_a_re, s5_a_im, s5_log_dt, s5_b_re, s5_b_im, s5_c_re, s5_c_im, s5_d, norm_final, b_ada, gla_w_gate2, gla_b_gate, gla_g_norm]
    m_small = [m_norm_mix, m_norm_mlp, m_s5_a_re, m_s5_a_im, m_s5_log_dt, m_s5_b_re, m_s5_b_im, m_s5_c_re, m_s5_c_im, m_s5_d, m_norm_final, m_b_ada, m_gla_w_gate2, m_gla_b_gate, m_gla_g_norm]
    v_small = [v_norm_mix, v_norm_mlp, v_s5_a_re, v_s5_a_im, v_s5_log_dt, v_s5_b_re, v_s5_b_im, v_s5_c_re, v_s5_c_im, v_s5_d, v_norm_final, v_b_ada, v_gla_w_gate2, v_gla_b_gate, v_gla_g_norm]
    as2d = lambda a: a.reshape(1, -1) if a.ndim == 1 else a
    small = _adamw_multi([as2d(a) for a in p_small], [as2d(a) for a in m_small], [as2d(a) for a in v_small],
                         [as2d(a) for a in g_small], name="adamw_small")
    small = [[o.reshape(p.shape) for o in outs] for outs, p in zip(small, p_small)]
    ada = _adamw(w_ada.reshape(2 * D_MODEL, ada_w), m_w_ada.reshape(2 * D_MODEL, ada_w), v_w_ada.reshape(2 * D_MODEL, ada_w),
                 [g_w_ada.reshape(2 * D_MODEL, ada_w)], tile=512, name="adamw_ada")
    ada = [a.reshape(w_ada.shape) for a in ada]

    def leaves(k):
        nm, nl, a_re, a_im, ldt, b_re, b_im, c_re, c_im, dsk, nf, bada, wg2_, bg_, gn_ = [s[k] for s in small]
        glu_, gin_, gout_, ff1a_, ff1b_, ff2a_, ff2b_ = [b[k] for b in big]
        return [ada[k], bada, nm, nl, a_re, a_im, ldt, b_re, b_im, c_re, c_im, dsk, glu_[None], gin_[None], wg2_, bg_, gn_, gout_[None],
                jnp.stack([ff1a_, ff1b_]), jnp.stack([ff2a_, ff2b_]), nf]

    return (loss, grad_x[None], *leaves(0), *leaves(1), *leaves(2), *leaves(3))
```
